```python
import jax, jax.numpy as jnp
from jax import lax
import numpy as np

D_MODEL = 2048
BATCH = 4
SEQ = 2048
DEPTH = 4

RMS_EPS = 1e-6
SSM_WIDTH = D_MODEL // 2
SSM_GROUP_SIZE = 16
SSM_GROUPS = SSM_WIDTH // SSM_GROUP_SIZE
SSM_STATE = 64
SSM_DT_MIN = 0.001
SSM_DT_MAX = 0.1
GLA_HEADS = 4
GLA_KEY_WIDTH = D_MODEL // 4
GLA_VALUE_WIDTH = D_MODEL // 2
GLA_DK = GLA_KEY_WIDTH // GLA_HEADS
GLA_DV = GLA_VALUE_WIDTH // GLA_HEADS
GLA_GATE_RANK = 16
GLA_GATE_NORM = 16.0
GLA_CHUNK = 64
MOBA_WIDTH = D_MODEL // 2
MOBA_HEAD_DIM = 128
MOBA_HEADS = MOBA_WIDTH // MOBA_HEAD_DIM
MOBA_BLOCK = 256
MOBA_TOPK = 3
MOBA_Q_CHUNK = 32
FFN_HIDDEN = -(-8 * D_MODEL // (3 * 256)) * 256
IN_SPLIT_SIZES = (SSM_WIDTH, GLA_KEY_WIDTH, GLA_KEY_WIDTH, GLA_VALUE_WIDTH, GLA_VALUE_WIDTH,
                  GLA_GATE_RANK, MOBA_WIDTH, MOBA_WIDTH, MOBA_WIDTH, D_MODEL, D_MODEL, D_MODEL)
IN_WIDTH = sum(IN_SPLIT_SIZES)

kernel_name = "hybrid_s5_gla_moba_gated_trunk"


def rms_norm(x, g):
    xf = x.astype(jnp.float32)
    y = xf * lax.rsqrt(jnp.mean(xf * xf, axis=-1, keepdims=True) + RMS_EPS)
    return (y * g.astype(jnp.float32)).astype(x.dtype)


def s5_mixer(u, lam_re, lam_im, log_dt, b_re, b_im, c_re, c_im, d_skip, w_glu, b_glu):
    bsz, L, _ = u.shape
    ug = u.reshape(bsz, L, SSM_GROUPS, SSM_GROUP_SIZE)
    dt = jnp.exp(log_dt)[:, None]
    mag = jnp.exp(lam_re * dt)
    ang = lam_im * dt
    a_re = mag * jnp.cos(ang)
    a_im = mag * jnp.sin(ang)
    den = lam_re * lam_re + lam_im * lam_im
    n_re = a_re - 1.0
    n_im = a_im
    z_re = (n_re * lam_re + n_im * lam_im) / den
    z_im = (n_im * lam_re - n_re * lam_im) / den
    bb_re = z_re[..., None] * b_re - z_im[..., None] * b_im
    bb_im = z_re[..., None] * b_im + z_im[..., None] * b_re
    bu_re = jnp.einsum('blgc,gpc->blgp', ug, bb_re)
    bu_im = jnp.einsum('blgc,gpc->blgp', ug, bb_im)
    ar = jnp.broadcast_to(a_re, bu_re.shape)
    ai = jnp.broadcast_to(a_im, bu_im.shape)

    def combine(e1, e2):
        ar1, ai1, br1, bi1 = e1
        ar2, ai2, br2, bi2 = e2
        return (ar1 * ar2 - ai1 * ai2,
                ar1 * ai2 + ai1 * ar2,
                ar2 * br1 - ai2 * bi1 + br2,
                ar2 * bi1 + ai2 * br1 + bi2)

    _, _, x_re, x_im = lax.associative_scan(combine, (ar, ai, bu_re, bu_im), axis=1)
    y = (jnp.einsum('blgp,gcp->blgc', x_re, c_re)
         - jnp.einsum('blgp,gcp->blgc', x_im, c_im))
    y = y.reshape(bsz, L, SSM_WIDTH) + d_skip * u
    z = jax.nn.gelu(y)
    return z * jax.nn.sigmoid(z @ w_glu + b_glu)


def gla_mixer(q, k, v, r, gate_lr, w_gate2, b_gate2, norm_g):
    bsz, L, _ = q.shape
    H, C = GLA_HEADS, GLA_CHUNK
    N = L // C
    f32 = jnp.float32

    def heads(t, dh):
        return t.astype(f32).reshape(bsz, N, C, H, dh).transpose(0, 3, 1, 2, 4)

    g = jax.nn.log_sigmoid((gate_lr @ w_gate2 + b_gate2).astype(f32)) / GLA_GATE_NORM
    qh = heads(q, GLA_DK) * (GLA_DK ** -0.5)
    kh = heads(k, GLA_DK)
    vh = heads(v, GLA_DV)
    gh = heads(g, GLA_DK)
    bcum = jnp.cumsum(gh, axis=3)
    b_last = bcum[:, :, :, -1:, :]
    q_dec = qh * jnp.exp(bcum)
    k_inv = kh * jnp.exp(-bcum)
    causal = jnp.tril(jnp.ones((C, C), dtype=bool))
    attn = jnp.einsum('bhncd,bhnjd->bhncj', q_dec, k_inv)
    attn = jnp.where(causal, attn, 0.0)
    o_intra = jnp.einsum('bhncj,bhnje->bhnce', attn, vh)
    k_dec = kh * jnp.exp(b_last - bcum)
    kv = jnp.einsum('bhncd,bhnce->bhnde', k_dec, vh)
    decay = jnp.exp(b_last[:, :, :, 0, :])

    def step(s, inp):
        dec, kv_n = inp
        return dec[..., None] * s + kv_n, s

    s0 = jnp.zeros((bsz, H, GLA_DK, GLA_DV), f32)
    _, s_prev = lax.scan(step, s0, (jnp.moveaxis(decay, 2, 0), jnp.moveaxis(kv, 2, 0)))
    s_prev = jnp.moveaxis(s_prev, 0, 2)
    o = o_intra + jnp.einsum('bhncd,bhnde->bhnce', q_dec, s_prev)
    o = rms_norm(o, norm_g)
    o = o.transpose(0, 2, 3, 1, 4).reshape(bsz, L, GLA_VALUE_WIDTH).astype(q.dtype)
    return o * jax.nn.silu(r)


def moba_mixer(q, k, v, q_g, k_g):
    bsz, L, _ = q.shape
    H, hd, BS, QC = MOBA_HEADS, MOBA_HEAD_DIM, MOBA_BLOCK, MOBA_Q_CHUNK
    Lp = -(-L // BS) * BS
    nb = Lp // BS
    topk = min(MOBA_TOPK, nb)
    pad = Lp - L

    def heads(t):
        t = jnp.pad(t, ((0, 0), (0, pad), (0, 0), (0, 0)))
        return t.transpose(0, 2, 1, 3)

    qh = heads(rms_norm(q.reshape(bsz, L, H, hd), q_g) * (hd ** -0.5))
    kh = heads(rms_norm(k.reshape(bsz, L, H, hd), k_g))
    vh = heads(v.reshape(bsz, L, H, hd))
    k_blocks = kh.reshape(bsz, H, nb, BS, hd)
    v_blocks = vh.reshape(bsz, H, nb, BS, hd)
    k_mean = jnp.mean(k_blocks, axis=3)

    blk_scores = jnp.einsum('bhqd,bhnd->bhqn', qh, k_mean).astype(jnp.float32)
    q_blk = jnp.arange(Lp) // BS
    past = jnp.arange(nb)[None, :] < q_blk[:, None]
    blk_scores = jnp.where(past, blk_scores, -jnp.inf)
    _, sel = lax.top_k(blk_scores, topk)
    sel_valid = jnp.arange(topk)[None, :] < q_blk[:, None]
    gather = jax.vmap(jax.vmap(lambda blocks, idx: blocks[idx]))

    def chunk(c):
        start = c * QC
        blk = start // BS
        q_c = lax.dynamic_slice_in_dim(qh, start, QC, axis=2)
        sel_c = lax.dynamic_slice_in_dim(sel, start, QC, axis=2)
        valid_c = lax.dynamic_slice_in_dim(sel_valid, start, QC, axis=0)
        k_sel = gather(k_blocks, sel_c)
        v_sel = gather(v_blocks, sel_c)
        s_past = jnp.einsum('bhqd,bhqkjd->bhqkj', q_c, k_sel).astype(jnp.float32)
        s_past = jnp.where(valid_c[None, None, :, :, None], s_past, -jnp.inf)
        s_past = s_past.reshape(bsz, H, QC, topk * BS)
        k_own = lax.dynamic_slice_in_dim(kh, blk * BS, BS, axis=2)
        v_own = lax.dynamic_slice_in_dim(vh, blk * BS, BS, axis=2)
        s_own = jnp.einsum('bhqd,bhjd->bhqj', q_c, k_own).astype(jnp.float32)
        q_pos = start + jnp.arange(QC)
        k_pos = blk * BS + jnp.arange(BS)
        s_own = jnp.where(k_pos[None, :] <= q_pos[:, None], s_own, -jnp.inf)
        p = jax.nn.softmax(jnp.concatenate([s_past, s_own], axis=-1), axis=-1).astype(v.dtype)
        p_past = p[..., :topk * BS].reshape(bsz, H, QC, topk, BS)
        p_own = p[..., topk * BS:]
        return (jnp.einsum('bhqkj,bhqkjd->bhqd', p_past, v_sel)
                + jnp.einsum('bhqj,bhjd->bhqd', p_own, v_own))

    o = lax.map(chunk, jnp.arange(Lp // QC))
    o = o.transpose(1, 0, 3, 2, 4).reshape(bsz, Lp, H * hd)
    return o[:, :L]


def setup_inputs(seed: int = 0) -> dict:
    key = jax.random.key(seed)
    ks = jax.random.split(key, 26)
    f32 = jnp.float32

    def nrm(k, shape, scale):
        return jax.random.normal(k, shape, f32) * scale

    G, P, S = SSM_GROUPS, SSM_STATE, SSM_GROUP_SIZE
    lam_im0 = jnp.pi * jnp.arange(P, dtype=f32)
    return {
        "x": nrm(ks[0], (BATCH, SEQ, D_MODEL), 1.0),
        "norm1_g": 1.0 + nrm(ks[1], (DEPTH, D_MODEL), 0.01),
        "w_in": nrm(ks[2], (DEPTH, D_MODEL, IN_WIDTH), D_MODEL ** -0.5),
        "ssm_lambda_re": -0.5 + nrm(ks[3], (DEPTH, G, P), 0.01),
        "ssm_lambda_im": lam_im0 + nrm(ks[4], (DEPTH, G, P), 0.01),
        "ssm_log_dt": jax.random.uniform(ks[5], (DEPTH, G), f32, np.log(SSM_DT_MIN), np.log(SSM_DT_MAX)),
        "ssm_b_re": nrm(ks[6], (DEPTH, G, P, S), (2.0 * S) ** -0.5),
        "ssm_b_im": nrm(ks[7], (DEPTH, G, P, S), (2.0 * S) ** -0.5),
        "ssm_c_re": nrm(ks[8], (DEPTH, G, S, P), (2.0 * P) ** -0.5),
        "ssm_c_im": nrm(ks[9], (DEPTH, G, S, P), (2.0 * P) ** -0.5),
        "ssm_d": nrm(ks[10], (DEPTH, SSM_WIDTH), 1.0),
        "ssm_w_glu": nrm(ks[11], (DEPTH, SSM_WIDTH, SSM_WIDTH), SSM_WIDTH ** -0.5),
        "ssm_b_glu": nrm(ks[12], (DEPTH, SSM_WIDTH), 0.01),
        "gla_w_gate2": nrm(ks[13], (DEPTH, GLA_GATE_RANK, GLA_KEY_WIDTH), GLA_GATE_RANK ** -0.5),
        "gla_b_gate2": nrm(ks[14], (DEPTH, GLA_KEY_WIDTH), 0.01),
        "gla_norm_g": 1.0 + nrm(ks[15], (DEPTH, GLA_DV), 0.01),
        "moba_q_norm_g": 1.0 + nrm(ks[16], (DEPTH, MOBA_HEAD_DIM), 0.01),
        "moba_k_norm_g": 1.0 + nrm(ks[17], (DEPTH, MOBA_HEAD_DIM), 0.01),
        "w_branch_ssm": nrm(ks[18], (DEPTH, SSM_WIDTH, D_MODEL), SSM_WIDTH ** -0.5),
        "w_branch_gla": nrm(ks[19], (DEPTH, GLA_VALUE_WIDTH, D_MODEL), GLA_VALUE_WIDTH ** -0.5),
        "w_branch_moba": nrm(ks[20], (DEPTH, MOBA_WIDTH, D_MODEL), MOBA_WIDTH ** -0.5),
        "w_out": nrm(ks[21], (DEPTH, D_MODEL, D_MODEL), D_MODEL ** -0.5),
        "norm2_g": 1.0 + nrm(ks[22], (DEPTH, D_MODEL), 0.01),
        "ffn_w_gate": nrm(ks[23], (DEPTH, D_MODEL, FFN_HIDDEN), D_MODEL ** -0.5),
        "ffn_w_up": nrm(ks[24], (DEPTH, D_MODEL, FFN_HIDDEN), D_MODEL ** -0.5),
        "ffn_w_down": nrm(ks[25], (DEPTH, FFN_HIDDEN, D_MODEL), FFN_HIDDEN ** -0.5),
    }


def reference(x, norm1_g, w_in, ssm_lambda_re, ssm_lambda_im, ssm_log_dt, ssm_b_re, ssm_b_im,
              ssm_c_re, ssm_c_im, ssm_d, ssm_w_glu, ssm_b_glu, gla_w_gate2, gla_b_gate2, gla_norm_g,
              moba_q_norm_g, moba_k_norm_g, w_branch_ssm, w_branch_gla, w_branch_moba, w_out,
              norm2_g, ffn_w_gate, ffn_w_up, ffn_w_down):
    split_points = np.cumsum(IN_SPLIT_SIZES)[:-1].tolist()
    for l in range(DEPTH):
        xn = rms_norm(x, norm1_g[l])
        proj = xn @ w_in[l]
        (ssm_u, gla_q, gla_k, gla_v, gla_r, gla_glr, moba_q, moba_k, moba_v,
         gate_ssm, gate_gla, gate_moba) = jnp.split(proj, split_points, axis=-1)
        y_ssm = s5_mixer(ssm_u, ssm_lambda_re[l], ssm_lambda_im[l], ssm_log_dt[l], ssm_b_re[l],
                         ssm_b_im[l], ssm_c_re[l], ssm_c_im[l], ssm_d[l], ssm_w_glu[l], ssm_b_glu[l])
        y_gla = gla_mixer(gla_q, gla_k, gla_v, gla_r, gla_glr, gla_w_gate2[l], gla_b_gate2[l],
                          gla_norm_g[l])
        y_moba = moba_mixer(moba_q, moba_k, moba_v, moba_q_norm_g[l], moba_k_norm_g[l])
        merged = (jax.nn.sigmoid(gate_ssm) * (y_ssm @ w_branch_ssm[l])
                  + jax.nn.sigmoid(gate_gla) * (y_gla @ w_branch_gla[l])
                  + jax.nn.sigmoid(gate_moba) * (y_moba @ w_branch_moba[l]))
        x = x + merged @ w_out[l]
        h = rms_norm(x, norm2_g[l])
        x = x + (jax.nn.silu(h @ ffn_w_gate[l]) * (h @ ffn_w_up[l])) @ ffn_w_down[l]
    return x
```

```python
import functools

import jax
import jax.numpy as jnp
import numpy as np
from jax import lax
from jax.experimental import pallas as pl
from jax.experimental.pallas import tpu as pltpu

F32 = jnp.float32
BF16 = jnp.bfloat16

D_MODEL = 2048
DEPTH = 4
RMS_EPS = 1e-6
SSM_WIDTH = D_MODEL // 2
SSM_GROUP_SIZE = 16
SSM_GROUPS = SSM_WIDTH // SSM_GROUP_SIZE
SSM_STATE = 64
GLA_HEADS = 4
GLA_KEY_WIDTH = D_MODEL // 4
GLA_VALUE_WIDTH = D_MODEL // 2
GLA_DK = GLA_KEY_WIDTH // GLA_HEADS
GLA_DV = GLA_VALUE_WIDTH // GLA_HEADS
GLA_GATE_RANK = 16
GLA_GATE_NORM = 16.0
GLA_CHUNK = 64
MOBA_WIDTH = D_MODEL // 2
MOBA_HEAD_DIM = 128
MOBA_HEADS = MOBA_WIDTH // MOBA_HEAD_DIM
MOBA_BLOCK = 256
MOBA_TOPK = 3
FFN_HIDDEN = -(-8 * D_MODEL // (3 * 256)) * 256

LANES = 128
SUBLANES = 8
MXU_DIM = 256
VMEM_LIMIT = 56 * 1024 * 1024

COL_SSM = 0
COL_GLA_Q = COL_SSM + SSM_WIDTH
COL_GLA_K = COL_GLA_Q + GLA_KEY_WIDTH
COL_GLA_V = COL_GLA_K + GLA_KEY_WIDTH
COL_GLA_R = COL_GLA_V + GLA_VALUE_WIDTH
COL_MOBA_Q = COL_GLA_R + GLA_VALUE_WIDTH
COL_MOBA_K = COL_MOBA_Q + MOBA_WIDTH
COL_MOBA_V = COL_MOBA_K + MOBA_WIDTH
COL_GATE_SSM = COL_MOBA_V + MOBA_WIDTH
COL_GATE_GLA = COL_GATE_SSM + D_MODEL
COL_GATE_MOBA = COL_GATE_GLA + D_MODEL
COL_GLR = COL_GATE_MOBA + D_MODEL
PROJ_WIDTH = COL_GLR + LANES
ORIG_GLR = COL_MOBA_Q

S5_PAIRS = SSM_GROUPS // 2
S5_OCTETS = S5_PAIRS // SUBLANES
S5_OCT_CH = SSM_WIDTH // S5_OCTETS
S5_PITCH = S5_PAIRS + 4

NEG_BIG = -1e30


def _cparams(sem):
    return pltpu.CompilerParams(dimension_semantics=sem, vmem_limit_bytes=VMEM_LIMIT)


def _const_spec(shape, index_map):
    return pl.BlockSpec(shape, index_map, pipeline_mode=pl.Buffered(1))


def _sigmoid(x):
    return 1.0 / (1.0 + jnp.exp(-x))


def _silu(x):
    return x * _sigmoid(x)


def _gelu_tanh(x):
    c = np.float32(np.sqrt(2.0 / np.pi))
    return 0.5 * x * (1.0 + jnp.tanh(c * (x + np.float32(0.044715) * (x * x * x))))


def _rms_scale(x):
    return lax.rsqrt(jnp.mean(x * x, axis=-1, keepdims=True) + RMS_EPS)


def _norm_matmul_kernel(x_ref, g_ref, w_ref, o_ref, xn_ref):
    @pl.when(pl.program_id(1) == 0)
    def _():
        x = x_ref[...]
        xn_ref[...] = (x * _rms_scale(x) * g_ref[...]).astype(BF16)

    o_ref[...] = jnp.dot(xn_ref[...], w_ref[...], preferred_element_type=F32).astype(o_ref.dtype)


def norm_matmul(x, g, w_all, layer, *, tm, tn):
    m, d = x.shape
    n = w_all.shape[-1]
    return pl.pallas_call(
        _norm_matmul_kernel,
        out_shape=jax.ShapeDtypeStruct((m, n), BF16),
        grid=(m // tm, n // tn),
        in_specs=[
            pl.BlockSpec((tm, d), lambda i, j: (i, 0)),
            pl.BlockSpec((None, 1, d), lambda i, j: (layer, 0, 0)),
            pl.BlockSpec((None, d, tn), lambda i, j: (layer, 0, j)),
        ],
        out_specs=pl.BlockSpec((tm, tn), lambda i, j: (i, j)),
        scratch_shapes=[pltpu.VMEM((tm, d), BF16)],
        compiler_params=_cparams(("parallel", "arbitrary")),
        name="in_proj",
    )(x, g, w_all)


def _s5_prep_kernel(lre_ref, lim_ref, ldt_ref, bre_ref, bim_ref, are_ref, aim_ref, bbre_ref, bbim_ref):
    lre = lre_ref[...]
    lim = lim_ref[...]
    dt = jnp.exp(ldt_ref[...])
    mag = jnp.exp(lre * dt)
    ang = lim * dt
    a_re = mag * jnp.cos(ang)
    a_im = mag * jnp.sin(ang)
    den = lre * lre + lim * lim
    n_re = a_re - 1.0
    n_im = a_im
    z_re = (n_re * lre + n_im * lim) / den
    z_im = (n_im * lre - n_re * lim) / den
    b_re = bre_ref[...]
    b_im = bim_ref[...]
    are_ref[...] = a_re
    aim_ref[...] = a_im
    bbre_ref[...] = z_re * b_re - z_im * b_im
    bbim_ref[...] = z_re * b_im + z_im * b_re


def s5_prep(lam_re, lam_im, log_dt, b_re, b_im):
    depth, g, p = lam_re.shape
    s = b_re.shape[-1]
    rows, cols = depth * g, p * s
    rep = lambda t: jnp.broadcast_to(t[..., None], (depth, g, p, s)).reshape(rows, cols)
    ldt = jnp.broadcast_to(log_dt[..., None, None], (depth, g, p, s)).reshape(rows, cols)
    sds = jax.ShapeDtypeStruct((rows, cols), F32)
    a_re, a_im, bb_re, bb_im = pl.pallas_call(
        _s5_prep_kernel,
        out_shape=(sds, sds, sds, sds),
        compiler_params=pltpu.CompilerParams(vmem_limit_bytes=VMEM_LIMIT),
        name="s5_prep",
    )(rep(lam_re), rep(lam_im), ldt, b_re.reshape(rows, cols), b_im.reshape(rows, cols))
    unrep = lambda t: t.reshape(depth, g, p, s)[..., 0]
    return unrep(a_re), unrep(a_im), bb_re.reshape(depth, g, p, s), bb_im.reshape(depth, g, p, s)


def s5_layout(a_re, a_im, bb_re, bb_im, c_re, c_im):
    depth = a_re.shape[0]
    gpo = SSM_GROUPS // S5_OCTETS
    eye = jnp.eye(gpo, dtype=F32)

    def in_blocks(bb):
        t = bb.reshape(depth, S5_OCTETS, gpo, SSM_STATE, SSM_GROUP_SIZE)
        w = jnp.einsum('logpc,gh->logchp', t, eye)
        return w.reshape(depth, S5_OCTETS, S5_OCT_CH, SUBLANES, LANES)

    w_in = jnp.stack([in_blocks(bb_re), in_blocks(bb_im)], axis=4)
    w_in = w_in.reshape(depth, S5_OCTETS, S5_OCT_CH, SUBLANES * 2 * LANES).astype(BF16)

    def out_blocks(c):
        t = c.reshape(depth, S5_OCTETS, gpo, SSM_GROUP_SIZE, SSM_STATE)
        w = jnp.einsum('logcp,gh->logphc', t, eye)
        return w.reshape(depth, S5_OCTETS, SUBLANES, LANES, S5_OCT_CH)

    w_out = jnp.stack([out_blocks(c_re), out_blocks(-c_im)], axis=3)
    w_out = w_out.reshape(depth, S5_OCTETS, SUBLANES * 2 * LANES, S5_OCT_CH).astype(BF16)
    tab = lambda a: a.reshape(depth, S5_PAIRS, LANES)
    return w_in, w_out, tab(a_re), tab(a_im)


def _s5_kernel(u_ref, win_ref, wout_ref, are_ref, aim_ref, d_ref, wglu_ref, bglu_ref, o_ref,
               bur_ref, bui_ref, sre_ref, sim_ref, y_ref, *, tc):
    @pl.when(pl.program_id(1) == 0)
    def _():
        sre_ref[...] = jnp.zeros_like(sre_ref)
        sim_ref[...] = jnp.zeros_like(sim_ref)

    u = u_ref[...]
    pair_w = 2 * LANES
    for o in range(S5_OCTETS):
        uo = u[:, o * S5_OCT_CH:(o + 1) * S5_OCT_CH]
        for jl in range(SUBLANES):
            res = jnp.dot(uo, win_ref[o, :, jl * pair_w:(jl + 1) * pair_w],
                          preferred_element_type=F32)
            j = o * SUBLANES + jl
            bur_ref[pl.ds(j, tc, stride=S5_PITCH), :] = res[:, :LANES]
            bui_ref[pl.ds(j, tc, stride=S5_PITCH), :] = res[:, LANES:]

    a_re = [are_ref[q * SUBLANES:(q + 1) * SUBLANES, :] for q in range(S5_OCTETS)]
    a_im = [aim_ref[q * SUBLANES:(q + 1) * SUBLANES, :] for q in range(S5_OCTETS)]

    def step(t, carry):
        xr, xi = carry
        nr, ni = [], []
        for q in range(S5_OCTETS):
            row = t * S5_PITCH + q * SUBLANES
            br = bur_ref[pl.ds(row, SUBLANES), :]
            bi = bui_ref[pl.ds(row, SUBLANES), :]
            r = a_re[q] * xr[q] - a_im[q] * xi[q] + br
            i = a_re[q] * xi[q] + a_im[q] * xr[q] + bi
            bur_ref[pl.ds(row, SUBLANES), :] = r
            bui_ref[pl.ds(row, SUBLANES), :] = i
            nr.append(r)
            ni.append(i)
        return tuple(nr), tuple(ni)

    init = (tuple(sre_ref[q * SUBLANES:(q + 1) * SUBLANES, :] for q in range(S5_OCTETS)),
            tuple(sim_ref[q * SUBLANES:(q + 1) * SUBLANES, :] for q in range(S5_OCTETS)))
    xr, xi = lax.fori_loop(0, tc, step, init, unroll=2)
    for q in range(S5_OCTETS):
        sre_ref[q * SUBLANES:(q + 1) * SUBLANES, :] = xr[q]
        sim_ref[q * SUBLANES:(q + 1) * SUBLANES, :] = xi[q]

    for o in range(S5_OCTETS):
        acc = None
        for jl in range(SUBLANES):
            j = o * SUBLANES + jl
            xr_j = bur_ref[pl.ds(j, tc, stride=S5_PITCH), :]
            xi_j = bui_ref[pl.ds(j, tc, stride=S5_PITCH), :]
            lhs = jnp.concatenate([xr_j, xi_j], axis=1).astype(BF16)
            part = jnp.dot(lhs, wout_ref[o, jl * pair_w:(jl + 1) * pair_w, :],
                           preferred_element_type=F32)
            acc = part if acc is None else acc + part
        y_ref[:, o * S5_OCT_CH:(o + 1) * S5_OCT_CH] = acc

    y = y_ref[...] + d_ref[...] * u.astype(F32)
    z = _gelu_tanh(y)
    gate = jnp.dot(z.astype(BF16), wglu_ref[...], preferred_element_type=F32) + bglu_ref[...]
    o_ref[...] = (z * _sigmoid(gate)).astype(o_ref.dtype)


def s5_mixer(proj, w_in_blk, w_out_blk, a_re, a_im, d_skip, w_glu, b_glu, layer, *, batch, seq, tc):
    nt = seq // tc
    w = SSM_WIDTH
    kern = functools.partial(_s5_kernel, tc=tc)
    return pl.pallas_call(
        kern,
        out_shape=jax.ShapeDtypeStruct((batch * seq, w), BF16),
        grid=(batch, nt),
        in_specs=[
            pl.BlockSpec((tc, w), lambda b, c: (b * nt + c, COL_SSM // w)),
            _const_spec((None,) + w_in_blk.shape[1:], lambda b, c: (layer, 0, 0, 0)),
            _const_spec((None,) + w_out_blk.shape[1:], lambda b, c: (layer, 0, 0, 0)),
            _const_spec((None, S5_PAIRS, LANES), lambda b, c: (layer, 0, 0)),
            _const_spec((None, S5_PAIRS, LANES), lambda b, c: (layer, 0, 0)),
            _const_spec((None, 1, w), lambda b, c: (layer, 0, 0)),
            _const_spec((None, w, w), lambda b, c: (layer, 0, 0)),
            _const_spec((None, 1, w), lambda b, c: (layer, 0, 0)),
        ],
        out_specs=pl.BlockSpec((tc, w), lambda b, c: (b * nt + c, 0)),
        scratch_shapes=[
            pltpu.VMEM((tc * S5_PITCH, LANES), F32),
            pltpu.VMEM((tc * S5_PITCH, LANES), F32),
            pltpu.VMEM((S5_PAIRS, LANES), F32),
            pltpu.VMEM((S5_PAIRS, LANES), F32),
            pltpu.VMEM((tc, w), F32),
        ],
        compiler_params=_cparams(("parallel", "arbitrary")),
        name="s5_mixer",
    )(proj, w_in_blk, w_out_blk, a_re, a_im, d_skip, w_glu, b_glu)


def _split3(x):
    hi = x.astype(BF16)
    r1 = x - hi.astype(F32)
    mid = r1.astype(BF16)
    lo = (r1 - mid.astype(F32)).astype(BF16)
    return hi, mid, lo


def _gla_kernel(q_ref, k_ref, v_ref, r_ref, glr_ref, wg_ref, bg_ref, ng_ref, tri_ref, o_ref, st_ref, *, tg):
    @pl.when(pl.program_id(1) == 0)
    def _():
        st_ref[...] = jnp.zeros_like(st_ref)

    c = GLA_CHUNK
    glog = jnp.dot(glr_ref[...], wg_ref[...], preferred_element_type=F32) + bg_ref[...]
    g = (jnp.minimum(glog, 0.0) - jnp.log1p(jnp.exp(-jnp.abs(glog)))) / GLA_GATE_NORM
    tri = tri_ref[...]
    bcum = None
    for piece in _split3(g):
        part = jnp.dot(tri, piece, preferred_element_type=F32)
        bcum = part if bcum is None else bcum + part

    row = lax.broadcasted_iota(jnp.int32, (c, c), 0)
    col = lax.broadcasted_iota(jnp.int32, (c, c), 1)
    causal = col <= row
    scale = np.float32(GLA_DK ** -0.5)
    ng = ng_ref[...]
    for n in range(tg // c):
        rs = slice(n * c, (n + 1) * c)
        for h in range(GLA_HEADS):
            ks = slice(h * GLA_DK, (h + 1) * GLA_DK)
            vs = slice(h * GLA_DV, (h + 1) * GLA_DV)
            b = bcum[rs, ks]
            bl = b[c - 1:c, :]
            qh = q_ref[rs, ks].astype(F32) * scale
            kh = k_ref[rs, ks].astype(F32)
            vh = v_ref[rs, vs]
            q_dec = (qh * jnp.exp(b)).astype(BF16)
            k_inv = (kh * jnp.exp(-b)).astype(BF16)
            k_dec = (kh * jnp.exp(bl - b)).astype(BF16)
            attn = lax.dot_general(q_dec, k_inv, (((1,), (1,)), ((), ())), preferred_element_type=F32)
            attn = jnp.where(causal, attn, 0.0).astype(BF16)
            st = st_ref[h]
            o = (jnp.dot(attn, vh, preferred_element_type=F32)
                 + lax.dot_general(q_dec, st.astype(BF16), (((1,), (1,)), ((), ())),
                                   preferred_element_type=F32))
            kvt = lax.dot_general(vh, k_dec, (((0,), (0,)), ((), ())), preferred_element_type=F32)
            st_ref[h] = jnp.exp(bl) * st + kvt
            o = o * _rms_scale(o) * ng
            o_ref[rs, vs] = (o * _silu(r_ref[rs, vs].astype(F32))).astype(o_ref.dtype)


def gla_mixer(proj, w_gate2_pad, b_gate2, norm_g, tri, layer, *, batch, seq, tg):
    nt = seq // tg
    kw, vw = GLA_KEY_WIDTH, GLA_VALUE_WIDTH
    kern = functools.partial(_gla_kernel, tg=tg)
    return pl.pallas_call(
        kern,
        out_shape=jax.ShapeDtypeStruct((batch * seq, vw), BF16),
        grid=(batch, nt),
        in_specs=[
            pl.BlockSpec((tg, kw), lambda b, c: (b * nt + c, COL_GLA_Q // kw)),
            pl.BlockSpec((tg, kw), lambda b, c: (b * nt + c, COL_GLA_K // kw)),
            pl.BlockSpec((tg, vw), lambda b, c: (b * nt + c, COL_GLA_V // vw)),
            pl.BlockSpec((tg, vw), lambda b, c: (b * nt + c, COL_GLA_R // vw)),
            pl.BlockSpec((tg, LANES), lambda b, c: (b * nt + c, COL_GLR // LANES)),
            _const_spec((None, LANES, kw), lambda b, c: (layer, 0, 0)),
            _const_spec((None, 1, kw), lambda b, c: (layer, 0, 0)),
            _const_spec((None, 1, GLA_DV), lambda b, c: (layer, 0, 0)),
            _const_spec((tg, tg), lambda b, c: (0, 0)),
        ],
        out_specs=pl.BlockSpec((tg, vw), lambda b, c: (b * nt + c, 0)),
        scratch_shapes=[pltpu.VMEM((GLA_HEADS, GLA_DV, GLA_DK), F32)],
        compiler_params=_cparams(("parallel", "arbitrary")),
        name="gla_mixer",
    )(proj, proj, proj, proj, proj, w_gate2_pad, b_gate2, norm_g, tri)


def _moba_kernel(q_ref, k_ref, v_ref, qg_ref, kg_ref, o_ref, kn_ref, km_ref, *, nb):
    i = pl.program_id(2)
    bs = MOBA_BLOCK

    @pl.when(i == 0)
    def _():
        kg = kg_ref[...]
        for jb in range(nb):
            kb = k_ref[jb * bs:(jb + 1) * bs, :].astype(F32)
            kn = kb * _rms_scale(kb) * kg
            kn_ref[jb * bs:(jb + 1) * bs, :] = kn.astype(BF16)
            km_ref[jb:jb + 1, :] = jnp.mean(kn, axis=0, keepdims=True)

    q = q_ref[...].astype(F32)
    qn = (q * _rms_scale(q) * qg_ref[...]) * np.float32(MOBA_HEAD_DIM ** -0.5)
    sb = lax.dot_general(qn, km_ref[...], (((1,), (1,)), ((), ())),
                         precision=lax.Precision.HIGHEST, preferred_element_type=F32)
    jidx = lax.broadcasted_iota(jnp.int32, (bs, nb), 1)
    cnt = jnp.zeros((bs, nb), F32)
    for jp in range(nb):
        sp = sb[:, jp:jp + 1]
        beats = jnp.where(sp > sb, 1.0, jnp.where(sp == sb, jnp.where(jidx > jp, 1.0, 0.0), 0.0))
        cnt = cnt + jnp.where(i > jp, beats, 0.0)
    keep = jnp.where(jidx < i, jnp.where(cnt < MOBA_TOPK, 1.0, 0.0), 0.0)
    selbias = jnp.where(keep > 0.5, 0.0, NEG_BIG)

    qb = qn.astype(BF16)

    def block_scores(j):
        start = pl.multiple_of(j * bs, bs)
        kj = kn_ref[pl.ds(start, bs), :]
        vj = v_ref[pl.ds(start, bs), :]
        s = lax.dot_general(qb, kj, (((1,), (1,)), ((), ())), preferred_element_type=F32)
        return s, vj

    row = lax.broadcasted_iota(jnp.int32, (bs, bs), 0)
    col = lax.broadcasted_iota(jnp.int32, (bs, bs), 1)
    s, vj = block_scores(i)
    s = jnp.where(col <= row, s, NEG_BIG)
    m = jnp.max(s, axis=-1, keepdims=True)
    p = jnp.exp(s - m)
    l = jnp.sum(p, axis=-1, keepdims=True)
    acc = jnp.dot(p.astype(BF16), vj, preferred_element_type=F32)

    def body(j, carry):
        m, l, acc = carry
        bj = jnp.sum(jnp.where(jidx == j, selbias, 0.0), axis=-1, keepdims=True)
        s, vj = block_scores(j)
        s = s + bj
        m_new = jnp.maximum(m, jnp.max(s, axis=-1, keepdims=True))
        alpha = jnp.exp(m - m_new)
        p = jnp.exp(s - m_new)
        l = alpha * l + jnp.sum(p, axis=-1, keepdims=True)
        acc = alpha * acc + jnp.dot(p.astype(BF16), vj, preferred_element_type=F32)
        return m_new, l, acc

    m, l, acc = lax.fori_loop(0, i, body, (m, l, acc))
    o_ref[...] = (acc / l).astype(o_ref.dtype)


def moba_mixer(proj, q_g, k_g, layer, *, batch, seq):
    bs, hd, nh = MOBA_BLOCK, MOBA_HEAD_DIM, MOBA_HEADS
    nb = seq // bs
    kern = functools.partial(_moba_kernel, nb=nb)
    return pl.pallas_call(
        kern,
        out_shape=jax.ShapeDtypeStruct((batch * seq, MOBA_WIDTH), BF16),
        grid=(batch, nh, nb),
        in_specs=[
            pl.BlockSpec((bs, hd), lambda b, h, i: (b * nb + i, COL_MOBA_Q // hd + h)),
            pl.BlockSpec((seq, hd), lambda b, h, i: (b, COL_MOBA_K // hd + h)),
            pl.BlockSpec((seq, hd), lambda b, h, i: (b, COL_MOBA_V // hd + h)),
            _const_spec((None, 1, hd), lambda b, h, i: (layer, 0, 0)),
            _const_spec((None, 1, hd), lambda b, h, i: (layer, 0, 0)),
        ],
        out_specs=pl.BlockSpec((bs, hd), lambda b, h, i: (b * nb + i, h)),
        scratch_shapes=[pltpu.VMEM((seq, hd), BF16), pltpu.VMEM((nb, hd), F32)],
        compiler_params=_cparams(("parallel", "parallel", "arbitrary")),
        name="moba_mixer",
    )(proj, proj, proj, q_g, k_g)


def _merge_kernel(ys_ref, yg_ref, ym_ref, gs_ref, gg_ref, gm_ref, ws_ref, wg_ref, wm_ref, o_ref):
    acc = _sigmoid(gs_ref[...].astype(F32)) * jnp.dot(ys_ref[...], ws_ref[...], preferred_element_type=F32)
    acc = acc + _sigmoid(gg_ref[...].astype(F32)) * jnp.dot(yg_ref[...], wg_ref[...], preferred_element_type=F32)
    acc = acc + _sigmoid(gm_ref[...].astype(F32)) * jnp.dot(ym_ref[...], wm_ref[...], preferred_element_type=F32)
    o_ref[...] = acc.astype(o_ref.dtype)


def merge_branches(y_ssm, y_gla, y_moba, proj, w_s, w_g, w_m, layer, *, tm, tn):
    m, kdim = y_ssm.shape
    n = w_s.shape[-1]
    yspec = pl.BlockSpec((tm, kdim), lambda i, j: (i, 0))
    gspec = lambda col: pl.BlockSpec((tm, tn), lambda i, j: (i, col // tn + j))
    wspec = pl.BlockSpec((None, kdim, tn), lambda i, j: (layer, 0, j))
    return pl.pallas_call(
        _merge_kernel,
        out_shape=jax.ShapeDtypeStruct((m, n), BF16),
        grid=(m // tm, n // tn),
        in_specs=[yspec, yspec, yspec, gspec(COL_GATE_SSM), gspec(COL_GATE_GLA), gspec(COL_GATE_MOBA),
                  wspec, wspec, wspec],
        out_specs=pl.BlockSpec((tm, tn), lambda i, j: (i, j)),
        compiler_params=_cparams(("parallel", "arbitrary")),
        name="merge",
    )(y_ssm, y_gla, y_moba, proj, proj, proj, w_s, w_g, w_m)


def _resid_matmul_kernel(a_ref, w_ref, x_ref, o_ref):
    o_ref[...] = x_ref[...] + jnp.dot(a_ref[...], w_ref[...], preferred_element_type=F32)


def resid_matmul(a, w_all, x, layer, *, tm, tn):
    m, kdim = a.shape
    n = w_all.shape[-1]
    return pl.pallas_call(
        _resid_matmul_kernel,
        out_shape=jax.ShapeDtypeStruct((m, n), F32),
        grid=(m // tm, n // tn),
        in_specs=[
            pl.BlockSpec((tm, kdim), lambda i, j: (i, 0)),
            pl.BlockSpec((None, kdim, tn), lambda i, j: (layer, 0, j)),
            pl.BlockSpec((tm, tn), lambda i, j: (i, j)),
        ],
        out_specs=pl.BlockSpec((tm, tn), lambda i, j: (i, j)),
        compiler_params=_cparams(("parallel", "arbitrary")),
        name="resid_matmul",
    )(a, w_all, x)


def _ffn_up_kernel(x_ref, g_ref, wg_ref, wu_ref, o_ref, hn_ref):
    @pl.when(pl.program_id(1) == 0)
    def _():
        x = x_ref[...]
        hn_ref[...] = (x * _rms_scale(x) * g_ref[...]).astype(BF16)

    h = hn_ref[...]
    gate = jnp.dot(h, wg_ref[...], preferred_element_type=F32)
    up = jnp.dot(h, wu_ref[...], preferred_element_type=F32)
    o_ref[...] = (_silu(gate) * up).astype(o_ref.dtype)


def ffn_up(x, g, w_gate, w_up, layer, *, tm, tn):
    m, d = x.shape
    n = w_gate.shape[-1]
    wspec = pl.BlockSpec((None, d, tn), lambda i, j: (layer, 0, j))
    return pl.pallas_call(
        _ffn_up_kernel,
        out_shape=jax.ShapeDtypeStruct((m, n), BF16),
        grid=(m // tm, n // tn),
        in_specs=[
            pl.BlockSpec((tm, d), lambda i, j: (i, 0)),
            pl.BlockSpec((None, 1, d), lambda i, j: (layer, 0, 0)),
            wspec, wspec,
        ],
        out_specs=pl.BlockSpec((tm, tn), lambda i, j: (i, j)),
        scratch_shapes=[pltpu.VMEM((tm, d), BF16)],
        compiler_params=_cparams(("parallel", "arbitrary")),
        name="ffn_up",
    )(x, g, w_gate, w_up)


def _gla_tri(tg):
    r = np.arange(tg)
    same_chunk = (r[:, None] // GLA_CHUNK) == (r[None, :] // GLA_CHUNK)
    return jnp.asarray(np.where(same_chunk & (r[None, :] <= r[:, None]), 1.0, 0.0), dtype=BF16)


def kernel(x, norm1_g, w_in, ssm_lambda_re, ssm_lambda_im, ssm_log_dt, ssm_b_re, ssm_b_im, ssm_c_re, ssm_c_im, ssm_d, ssm_w_glu, ssm_b_glu, gla_w_gate2, gla_b_gate2, gla_norm_g, moba_q_norm_g, moba_k_norm_g, w_branch_ssm, w_branch_gla, w_branch_moba, w_out, norm2_g, ffn_w_gate, ffn_w_up, ffn_w_down):
    batch, seq, d = x.shape
    depth = w_in.shape[0]
    m = batch * seq

    w_in_r = jnp.concatenate(
        [w_in[..., :ORIG_GLR], w_in[..., ORIG_GLR + GLA_GATE_RANK:],
         w_in[..., ORIG_GLR:ORIG_GLR + GLA_GATE_RANK],
         jnp.zeros((depth, d, LANES - GLA_GATE_RANK), w_in.dtype)], axis=-1).astype(BF16)
    wg2_pad = jnp.concatenate(
        [gla_w_gate2, jnp.zeros((depth, LANES - GLA_GATE_RANK, GLA_KEY_WIDTH), gla_w_gate2.dtype)],
        axis=1).astype(BF16)
    a_re, a_im, bb_re, bb_im = s5_prep(ssm_lambda_re, ssm_lambda_im, ssm_log_dt, ssm_b_re, ssm_b_im)
    s5_win, s5_wout, a_re_t, a_im_t = s5_layout(a_re, a_im, bb_re, bb_im, ssm_c_re, ssm_c_im)
    row3 = lambda t: t[:, None, :]
    w_glu = ssm_w_glu.astype(BF16)
    w_bs, w_bg, w_bm = (w_branch_ssm.astype(BF16), w_branch_gla.astype(BF16), w_branch_moba.astype(BF16))
    w_o = w_out.astype(BF16)
    w_fg, w_fu, w_fd = ffn_w_gate.astype(BF16), ffn_w_up.astype(BF16), ffn_w_down.astype(BF16)
    n1, n2 = row3(norm1_g), row3(norm2_g)
    ssm_d3, b_glu3 = row3(ssm_d), row3(ssm_b_glu)
    bg2, gng = row3(gla_b_gate2), row3(gla_norm_g)
    mqg, mkg = row3(moba_q_norm_g), row3(moba_k_norm_g)

    tg = 256
    tri = _gla_tri(tg)
    xf = x.reshape(m, d)
    for l in range(depth):
        proj = norm_matmul(xf, n1, w_in_r, l, tm=1024, tn=896)
        y_ssm = s5_mixer(proj, s5_win, s5_wout, a_re_t, a_im_t, ssm_d3, w_glu, b_glu3, l,
                         batch=batch, seq=seq, tc=256)
        y_gla = gla_mixer(proj, wg2_pad, bg2, gng, tri, l, batch=batch, seq=seq, tg=tg)
        y_moba = moba_mixer(proj, mqg, mkg, l, batch=batch, seq=seq)
        merged = merge_branches(y_ssm, y_gla, y_moba, proj, w_bs, w_bg, w_bm, l, tm=1024, tn=512)
        xf = resid_matmul(merged, w_o, xf, l, tm=1024, tn=512)
        act = ffn_up(xf, n2, w_fg, w_fu, l, tm=1024, tn=512)
        xf = resid_matmul(act, w_fd, xf, l, tm=512, tn=512)
    return xf.reshape(batch, seq, d)
```

```python
import functools

import jax
import jax.numpy as jnp
import numpy as np
from jax import lax
from jax.experimental import pallas as pl
from jax.experimental.pallas import tpu as pltpu

F32 = jnp.float32
BF16 = jnp.bfloat16

D_MODEL = 2048
DEPTH = 4
RMS_EPS = 1e-6
SSM_WIDTH = D_MODEL // 2
SSM_GROUP_SIZE = 16
SSM_GROUPS = SSM_WIDTH // SSM_GROUP_SIZE
SSM_STATE = 64
GLA_HEADS = 4
GLA_KEY_WIDTH = D_MODEL // 4
GLA_VALUE_WIDTH = D_MODEL // 2
GLA_DK = GLA_KEY_WIDTH // GLA_HEADS
GLA_DV = GLA_VALUE_WIDTH // GLA_HEADS
GLA_GATE_RANK = 16
GLA_GATE_NORM = 16.0
GLA_CHUNK = 64
MOBA_WIDTH = D_MODEL // 2
MOBA_HEAD_DIM = 128
MOBA_HEADS = MOBA_WIDTH // MOBA_HEAD_DIM
MOBA_BLOCK = 256
MOBA_TOPK = 3
FFN_HIDDEN = -(-8 * D_MODEL // (3 * 256)) * 256

LANES = 128
SUBLANES = 8
MXU_DIM = 256
VMEM_LIMIT = 56 * 1024 * 1024

COL_SSM = 0
COL_GLA_Q = COL_SSM + SSM_WIDTH
COL_GLA_K = COL_GLA_Q + GLA_KEY_WIDTH
COL_GLA_V = COL_GLA_K + GLA_KEY_WIDTH
COL_GLA_R = COL_GLA_V + GLA_VALUE_WIDTH
COL_MOBA_Q = COL_GLA_R + GLA_VALUE_WIDTH
COL_MOBA_K = COL_MOBA_Q + MOBA_WIDTH
COL_MOBA_V = COL_MOBA_K + MOBA_WIDTH
COL_GATE_SSM = COL_MOBA_V + MOBA_WIDTH
COL_GATE_GLA = COL_GATE_SSM + D_MODEL
COL_GATE_MOBA = COL_GATE_GLA + D_MODEL
PROJ_WIDTH = COL_GATE_MOBA + D_MODEL
ORIG_GLR = COL_MOBA_Q

S5_PAIRS = SSM_GROUPS // 2
S5_OCTETS = S5_PAIRS // SUBLANES
S5_OCT_CH = SSM_WIDTH // S5_OCTETS
S5_PITCH = S5_PAIRS + 4

NEG_BIG = -1e30


def _cparams(sem):
    return pltpu.CompilerParams(dimension_semantics=sem, vmem_limit_bytes=VMEM_LIMIT)


def _const_spec(shape, index_map):
    return pl.BlockSpec(shape, index_map, pipeline_mode=pl.Buffered(1))


def _sigmoid(x):
    return 1.0 / (1.0 + jnp.exp(-x))


def _silu(x):
    return x * _sigmoid(x)


def _gelu_tanh(x):
    c = np.float32(np.sqrt(2.0 / np.pi))
    return 0.5 * x * (1.0 + jnp.tanh(c * (x + np.float32(0.044715) * (x * x * x))))


def _rms_scale(x):
    return lax.rsqrt(jnp.mean(x * x, axis=-1, keepdims=True) + RMS_EPS)


def _in_proj_kernel(x_ref, g_ref, wa_ref, wb_ref, wglr_ref, o_ref, oglr_ref, xn_ref, wbf_ref, *,
                    n_aligned, row_chunk):
    j = pl.program_id(1)
    d, tn = wa_ref.shape
    shift = GLA_GATE_RANK

    @pl.when(j == 0)
    def _():
        x = x_ref[...]
        xn = (x * _rms_scale(x) * g_ref[...]).astype(BF16)
        xn_ref[...] = xn
        oglr_ref[...] = jnp.dot(xn, wglr_ref[...], preferred_element_type=F32).astype(oglr_ref.dtype)

    @pl.when(j < n_aligned)
    def _():
        wbf_ref[...] = wa_ref[...].astype(BF16)

    @pl.when(j >= n_aligned)
    def _():
        lane = lax.broadcasted_iota(jnp.int32, (row_chunk, LANES), 1)
        for r in range(0, d, row_chunk):
            rs = slice(r, r + row_chunk)
            ra = pltpu.roll(wa_ref[rs, :], tn - shift, axis=1)
            rb = pltpu.roll(wb_ref[rs, :], LANES - shift, axis=1)
            wbf_ref[rs, :tn - LANES] = ra[:, :tn - LANES].astype(BF16)
            wbf_ref[rs, tn - LANES:] = jnp.where(lane < LANES - shift, ra[:, tn - LANES:], rb).astype(BF16)

    o_ref[...] = jnp.dot(xn_ref[...], wbf_ref[...], preferred_element_type=F32).astype(o_ref.dtype)


def in_proj(x, g, w_in, w_glr, layer, *, tm, tn):
    m, d = x.shape
    assert ORIG_GLR % tn == 0 and PROJ_WIDTH % tn == 0
    n_aligned = ORIG_GLR // tn
    kern = functools.partial(_in_proj_kernel, n_aligned=n_aligned, row_chunk=256)
    return pl.pallas_call(
        kern,
        out_shape=(jax.ShapeDtypeStruct((m, PROJ_WIDTH), BF16), jax.ShapeDtypeStruct((m, LANES), BF16)),
        grid=(m // tm, PROJ_WIDTH // tn),
        in_specs=[
            pl.BlockSpec((tm, d), lambda i, j: (i, 0)),
            pl.BlockSpec((None, 1, d), lambda i, j: (layer, 0, 0)),
            pl.BlockSpec((None, d, tn), lambda i, j: (layer, 0, j)),
            pl.BlockSpec((None, d, LANES), lambda i, j: (layer, 0, (j + 1) * (tn // LANES))),
            _const_spec((None, d, LANES), lambda i, j: (layer, 0, 0)),
        ],
        out_specs=(pl.BlockSpec((tm, tn), lambda i, j: (i, j)),
                   pl.BlockSpec((tm, LANES), lambda i, j: (i, 0))),
        scratch_shapes=[pltpu.VMEM((tm, d), BF16), pltpu.VMEM((d, tn), BF16)],
        compiler_params=_cparams(("parallel", "arbitrary")),
        name="in_proj",
    )(x, g, w_in, w_in, w_glr)


def _s5_prep_kernel(lre_ref, lim_ref, ldt_ref, bre_ref, bim_ref, are_ref, aim_ref, bbre_ref, bbim_ref):
    lre = lre_ref[...]
    lim = lim_ref[...]
    dt = jnp.exp(ldt_ref[...])
    mag = jnp.exp(lre * dt)
    ang = lim * dt
    a_re = mag * jnp.cos(ang)
    a_im = mag * jnp.sin(ang)
    den = lre * lre + lim * lim
    n_re = a_re - 1.0
    n_im = a_im
    z_re = (n_re * lre + n_im * lim) / den
    z_im = (n_im * lre - n_re * lim) / den
    b_re = bre_ref[...]
    b_im = bim_ref[...]
    are_ref[...] = a_re
    aim_ref[...] = a_im
    bbre_ref[...] = z_re * b_re - z_im * b_im
    bbim_ref[...] = z_re * b_im + z_im * b_re


def s5_prep(lam_re, lam_im, log_dt, b_re, b_im):
    depth, g, p = lam_re.shape
    s = b_re.shape[-1]
    rows, cols = depth * g, p * s
    rep = lambda t: jnp.broadcast_to(t[..., None], (depth, g, p, s)).reshape(rows, cols)
    ldt = jnp.broadcast_to(log_dt[..., None, None], (depth, g, p, s)).reshape(rows, cols)
    sds = jax.ShapeDtypeStruct((rows, cols), F32)
    a_re, a_im, bb_re, bb_im = pl.pallas_call(
        _s5_prep_kernel,
        out_shape=(sds, sds, sds, sds),
        compiler_params=pltpu.CompilerParams(vmem_limit_bytes=VMEM_LIMIT),
        name="s5_prep",
    )(rep(lam_re), rep(lam_im), ldt, b_re.reshape(rows, cols), b_im.reshape(rows, cols))
    unrep = lambda t: t.reshape(depth, g, p, s)[..., 0]
    return unrep(a_re), unrep(a_im), bb_re.reshape(depth, g, p, s), bb_im.reshape(depth, g, p, s)


def s5_layout(a_re, a_im, bb_re, bb_im, c_re, c_im):
    depth = a_re.shape[0]
    gpo = SSM_GROUPS // S5_OCTETS
    eye = jnp.eye(gpo, dtype=F32)

    def in_blocks(bb):
        t = bb.reshape(depth, S5_OCTETS, gpo, SSM_STATE, SSM_GROUP_SIZE)
        w = jnp.einsum('logpc,gh->logchp', t, eye)
        return w.reshape(depth, S5_OCTETS, S5_OCT_CH, SUBLANES, LANES)

    w_in = jnp.stack([in_blocks(bb_re), in_blocks(bb_im)], axis=4)
    w_in = w_in.reshape(depth, S5_OCTETS, S5_OCT_CH, SUBLANES * 2 * LANES).astype(BF16)

    def out_blocks(c):
        t = c.reshape(depth, S5_OCTETS, gpo, SSM_GROUP_SIZE, SSM_STATE)
        w = jnp.einsum('logcp,gh->logphc', t, eye)
        return w.reshape(depth, S5_OCTETS, SUBLANES, LANES, S5_OCT_CH)

    w_out = jnp.stack([out_blocks(c_re), out_blocks(-c_im)], axis=3)
    w_out = w_out.reshape(depth, S5_OCTETS, SUBLANES * 2 * LANES, S5_OCT_CH).astype(BF16)
    tab = lambda a: a.reshape(depth, S5_PAIRS, LANES)
    return w_in, w_out, tab(a_re), tab(a_im)


def _s5_kernel(u_ref, win_ref, wout_ref, are_ref, aim_ref, d_ref, wglu_ref, bglu_ref, o_ref,
               bur_ref, bui_ref, sre_ref, sim_ref, y_ref, *, tc):
    @pl.when(pl.program_id(1) == 0)
    def _():
        sre_ref[...] = jnp.zeros_like(sre_ref)
        sim_ref[...] = jnp.zeros_like(sim_ref)

    u = u_ref[...]
    pair_w = 2 * LANES
    for o in range(S5_OCTETS):
        uo = u[:, o * S5_OCT_CH:(o + 1) * S5_OCT_CH]
        for jl in range(SUBLANES):
            res = jnp.dot(uo, win_ref[o, :, jl * pair_w:(jl + 1) * pair_w],
                          preferred_element_type=F32)
            j = o * SUBLANES + jl
            bur_ref[pl.ds(j, tc, stride=S5_PITCH), :] = res[:, :LANES]
            bui_ref[pl.ds(j, tc, stride=S5_PITCH), :] = res[:, LANES:]

    a_re = [are_ref[q * SUBLANES:(q + 1) * SUBLANES, :] for q in range(S5_OCTETS)]
    a_im = [aim_ref[q * SUBLANES:(q + 1) * SUBLANES, :] for q in range(S5_OCTETS)]

    def step(t, carry):
        xr, xi = carry
        nr, ni = [], []
        for q in range(S5_OCTETS):
            row = t * S5_PITCH + q * SUBLANES
            br = bur_ref[pl.ds(row, SUBLANES), :]
            bi = bui_ref[pl.ds(row, SUBLANES), :]
            r = a_re[q] * xr[q] - a_im[q] * xi[q] + br
            i = a_re[q] * xi[q] + a_im[q] * xr[q] + bi
            bur_ref[pl.ds(row, SUBLANES), :] = r
            bui_ref[pl.ds(row, SUBLANES), :] = i
            nr.append(r)
            ni.append(i)
        return tuple(nr), tuple(ni)

    init = (tuple(sre_ref[q * SUBLANES:(q + 1) * SUBLANES, :] for q in range(S5_OCTETS)),
            tuple(sim_ref[q * SUBLANES:(q + 1) * SUBLANES, :] for q in range(S5_OCTETS)))
    xr, xi = lax.fori_loop(0, tc, step, init, unroll=2)
    for q in range(S5_OCTETS):
        sre_ref[q * SUBLANES:(q + 1) * SUBLANES, :] = xr[q]
        sim_ref[q * SUBLANES:(q + 1) * SUBLANES, :] = xi[q]

    for o in range(S5_OCTETS):
        acc = None
        for jl in range(SUBLANES):
            j = o * SUBLANES + jl
            xr_j = bur_ref[pl.ds(j, tc, stride=S5_PITCH), :]
            xi_j = bui_ref[pl.ds(j, tc, stride=S5_PITCH), :]
            lhs = jnp.concatenate([xr_j, xi_j], axis=1).astype(BF16)
            part = jnp.dot(lhs, wout_ref[o, jl * pair_w:(jl + 1) * pair_w, :],
                           preferred_element_type=F32)
            acc = part if acc is None else acc + part
        y_ref[:, o * S5_OCT_CH:(o + 1) * S5_OCT_CH] = acc

    y = y_ref[...] + d_ref[...] * u.astype(F32)
    z = _gelu_tanh(y)
    gate = jnp.dot(z.astype(BF16), wglu_ref[...].astype(BF16), preferred_element_type=F32) + bglu_ref[...]
    o_ref[...] = (z * _sigmoid(gate)).astype(o_ref.dtype)


def s5_mixer(proj, w_in_blk, w_out_blk, a_re, a_im, d_skip, w_glu, b_glu, layer, *, batch, seq, tc):
    nt = seq // tc
    w = SSM_WIDTH
    kern = functools.partial(_s5_kernel, tc=tc)
    return pl.pallas_call(
        kern,
        out_shape=jax.ShapeDtypeStruct((batch * seq, w), BF16),
        grid=(batch, nt),
        in_specs=[
            pl.BlockSpec((tc, w), lambda b, c: (b * nt + c, COL_SSM // w)),
            _const_spec((None,) + w_in_blk.shape[1:], lambda b, c: (layer, 0, 0, 0)),
            _const_spec((None,) + w_out_blk.shape[1:], lambda b, c: (layer, 0, 0, 0)),
            _const_spec((None, S5_PAIRS, LANES), lambda b, c: (layer, 0, 0)),
            _const_spec((None, S5_PAIRS, LANES), lambda b, c: (layer, 0, 0)),
            _const_spec((None, 1, w), lambda b, c: (layer, 0, 0)),
            _const_spec((None, w, w), lambda b, c: (layer, 0, 0)),
            _const_spec((None, 1, w), lambda b, c: (layer, 0, 0)),
        ],
        out_specs=pl.BlockSpec((tc, w), lambda b, c: (b * nt + c, 0)),
        scratch_shapes=[
            pltpu.VMEM((tc * S5_PITCH, LANES), F32),
            pltpu.VMEM((tc * S5_PITCH, LANES), F32),
            pltpu.VMEM((S5_PAIRS, LANES), F32),
            pltpu.VMEM((S5_PAIRS, LANES), F32),
            pltpu.VMEM((tc, w), F32),
        ],
        compiler_params=_cparams(("parallel", "arbitrary")),
        name="s5_mixer",
    )(proj, w_in_blk, w_out_blk, a_re, a_im, d_skip, w_glu, b_glu)


def _split3(x):
    hi = x.astype(BF16)
    r1 = x - hi.astype(F32)
    mid = r1.astype(BF16)
    lo = (r1 - mid.astype(F32)).astype(BF16)
    return hi, mid, lo


def _gla_kernel(q_ref, k_ref, v_ref, r_ref, glr_ref, wg_ref, bg_ref, ng_ref, tri_ref, o_ref, st_ref, *, tg):
    @pl.when(pl.program_id(1) == 0)
    def _():
        st_ref[...] = jnp.zeros_like(st_ref)

    c = GLA_CHUNK
    glog = jnp.dot(glr_ref[...], wg_ref[...], preferred_element_type=F32) + bg_ref[...]
    g = (jnp.minimum(glog, 0.0) - jnp.log1p(jnp.exp(-jnp.abs(glog)))) / GLA_GATE_NORM
    tri = tri_ref[...]
    bcum = None
    for piece in _split3(g):
        part = jnp.dot(tri, piece, preferred_element_type=F32)
        bcum = part if bcum is None else bcum + part

    row = lax.broadcasted_iota(jnp.int32, (c, c), 0)
    col = lax.broadcasted_iota(jnp.int32, (c, c), 1)
    causal = col <= row
    scale = np.float32(GLA_DK ** -0.5)
    ng = ng_ref[...]
    for n in range(tg // c):
        rs = slice(n * c, (n + 1) * c)
        for h in range(GLA_HEADS):
            ks = slice(h * GLA_DK, (h + 1) * GLA_DK)
            vs = slice(h * GLA_DV, (h + 1) * GLA_DV)
            b = bcum[rs, ks]
            bl = b[c - 1:c, :]
            qh = q_ref[rs, ks].astype(F32) * scale
            kh = k_ref[rs, ks].astype(F32)
            vh = v_ref[rs, vs]
            q_dec = (qh * jnp.exp(b)).astype(BF16)
            k_inv = (kh * jnp.exp(-b)).astype(BF16)
            k_dec = (kh * jnp.exp(bl - b)).astype(BF16)
            attn = lax.dot_general(q_dec, k_inv, (((1,), (1,)), ((), ())), preferred_element_type=F32)
            attn = jnp.where(causal, attn, 0.0).astype(BF16)
            st = st_ref[h]
            o = (jnp.dot(attn, vh, preferred_element_type=F32)
                 + lax.dot_general(q_dec, st.astype(BF16), (((1,), (1,)), ((), ())),
                                   preferred_element_type=F32))
            kvt = lax.dot_general(vh, k_dec, (((0,), (0,)), ((), ())), preferred_element_type=F32)
            st_ref[h] = jnp.exp(bl) * st + kvt
            o = o * _rms_scale(o) * ng
            o_ref[rs, vs] = (o * _silu(r_ref[rs, vs].astype(F32))).astype(o_ref.dtype)


def gla_mixer(proj, glr, w_gate2_pad, b_gate2, norm_g, tri, layer, *, batch, seq, tg):
    nt = seq // tg
    kw, vw = GLA_KEY_WIDTH, GLA_VALUE_WIDTH
    kern = functools.partial(_gla_kernel, tg=tg)
    return pl.pallas_call(
        kern,
        out_shape=jax.ShapeDtypeStruct((batch * seq, vw), BF16),
        grid=(batch, nt),
        in_specs=[
            pl.BlockSpec((tg, kw), lambda b, c: (b * nt + c, COL_GLA_Q // kw)),
            pl.BlockSpec((tg, kw), lambda b, c: (b * nt + c, COL_GLA_K // kw)),
            pl.BlockSpec((tg, vw), lambda b, c: (b * nt + c, COL_GLA_V // vw)),
            pl.BlockSpec((tg, vw), lambda b, c: (b * nt + c, COL_GLA_R // vw)),
            pl.BlockSpec((tg, LANES), lambda b, c: (b * nt + c, 0)),
            _const_spec((None, LANES, kw), lambda b, c: (layer, 0, 0)),
            _const_spec((None, 1, kw), lambda b, c: (layer, 0, 0)),
            _const_spec((None, 1, GLA_DV), lambda b, c: (layer, 0, 0)),
            _const_spec((tg, tg), lambda b, c: (0, 0)),
        ],
        out_specs=pl.BlockSpec((tg, vw), lambda b, c: (b * nt + c, 0)),
        scratch_shapes=[pltpu.VMEM((GLA_HEADS, GLA_DV, GLA_DK), F32)],
        compiler_params=_cparams(("parallel", "arbitrary")),
        name="gla_mixer",
    )(proj, proj, proj, proj, glr, w_gate2_pad, b_gate2, norm_g, tri)


def _moba_kernel(q_ref, k_ref, v_ref, qg_ref, kg_ref, o_ref, kn_ref, vt_ref, km_ref, *, nb):
    bs = MOBA_BLOCK
    kg = kg_ref[...]
    for jb in range(nb):
        kb = k_ref[jb * bs:(jb + 1) * bs, :].astype(F32)
        kn = kb * _rms_scale(kb) * kg
        kn_ref[jb * bs:(jb + 1) * bs, :] = kn.astype(BF16)
        km_ref[jb:jb + 1, :] = jnp.mean(kn, axis=0, keepdims=True)
    vt_ref[...] = v_ref[...].astype(F32).T.astype(BF16)

    qg = qg_ref[...]
    km = km_ref[...]
    blk = lax.broadcasted_iota(jnp.int32, (nb, bs), 0)
    krow = lax.broadcasted_iota(jnp.int32, (bs, bs), 0)
    qcol = lax.broadcasted_iota(jnp.int32, (bs, bs), 1)
    for i in range(nb):
        q = q_ref[i * bs:(i + 1) * bs, :].astype(F32)
        qn = (q * _rms_scale(q) * qg) * np.float32(MOBA_HEAD_DIM ** -0.5)
        qb = qn.astype(BF16)
        bias_t = None
        if i > MOBA_TOPK:
            sbt = lax.dot_general(km, qn, (((1,), (1,)), ((), ())),
                                  precision=lax.Precision.HIGHEST, preferred_element_type=F32)
            cnt = jnp.zeros((nb, bs), F32)
            for jp in range(i):
                sp = sbt[jp:jp + 1, :]
                beats = jnp.where(sp > sbt, 1.0, jnp.where(sp == sbt, jnp.where(blk > jp, 1.0, 0.0), 0.0))
                cnt = cnt + beats
            bias_t = jnp.where(cnt < MOBA_TOPK, 0.0, NEG_BIG)
        nk = (i + 1) * bs
        st = lax.dot_general(kn_ref[0:nk, :], qb, (((1,), (1,)), ((), ())),
                             preferred_element_type=F32)
        pieces = []
        for j in range(i + 1):
            sj = st[j * bs:(j + 1) * bs, :]
            if j == i:
                sj = jnp.where(krow <= qcol, sj, NEG_BIG)
            elif bias_t is not None:
                sj = sj + bias_t[j:j + 1, :]
            pieces.append(sj)
        mx = pieces[0]
        for pj in pieces[1:]:
            mx = jnp.maximum(mx, pj)
        m = jnp.max(mx, axis=0, keepdims=True)
        ps = [jnp.exp(pj - m) for pj in pieces]
        lsum = ps[0]
        for pj in ps[1:]:
            lsum = lsum + pj
        l = jnp.sum(lsum, axis=0, keepdims=True)
        pt = jnp.concatenate([pj.astype(BF16) for pj in ps], axis=0)
        ot = jnp.dot(vt_ref[:, 0:nk], pt, preferred_element_type=F32)
        o_ref[i * bs:(i + 1) * bs, :] = (ot / l).T.astype(o_ref.dtype)


def moba_mixer(proj, q_g, k_g, layer, *, batch, seq):
    bs, hd, nh = MOBA_BLOCK, MOBA_HEAD_DIM, MOBA_HEADS
    nb = seq // bs
    kern = functools.partial(_moba_kernel, nb=nb)
    blk = lambda col: pl.BlockSpec((seq, hd), lambda b, h: (b, col // hd + h))
    return pl.pallas_call(
        kern,
        out_shape=jax.ShapeDtypeStruct((batch * seq, MOBA_WIDTH), BF16),
        grid=(batch, nh),
        in_specs=[
            blk(COL_MOBA_Q), blk(COL_MOBA_K), blk(COL_MOBA_V),
            _const_spec((None, 1, hd), lambda b, h: (layer, 0, 0)),
            _const_spec((None, 1, hd), lambda b, h: (layer, 0, 0)),
        ],
        out_specs=pl.BlockSpec((seq, hd), lambda b, h: (b, h)),
        scratch_shapes=[pltpu.VMEM((seq, hd), BF16), pltpu.VMEM((hd, seq), BF16),
                        pltpu.VMEM((nb, hd), F32)],
        compiler_params=_cparams(("parallel", "parallel")),
        name="moba_mixer",
    )(proj, proj, proj, q_g, k_g)


def _merge_kernel(ys_ref, yg_ref, ym_ref, gs_ref, gg_ref, gm_ref, ws_ref, wg_ref, wm_ref, o_ref):
    def branch(g_ref, y_ref, w_ref):
        return _sigmoid(g_ref[...].astype(F32)) * jnp.dot(y_ref[...], w_ref[...].astype(BF16),
                                                          preferred_element_type=F32)

    acc = branch(gs_ref, ys_ref, ws_ref) + branch(gg_ref, yg_ref, wg_ref) + branch(gm_ref, ym_ref, wm_ref)
    o_ref[...] = acc.astype(o_ref.dtype)


def merge_branches(y_ssm, y_gla, y_moba, proj, w_s, w_g, w_m, layer, *, tm, tn):
    m, kdim = y_ssm.shape
    n = w_s.shape[-1]
    yspec = pl.BlockSpec((tm, kdim), lambda i, j: (i, 0))
    gspec = lambda col: pl.BlockSpec((tm, tn), lambda i, j: (i, col // tn + j))
    wspec = pl.BlockSpec((None, kdim, tn), lambda i, j: (layer, 0, j))
    return pl.pallas_call(
        _merge_kernel,
        out_shape=jax.ShapeDtypeStruct((m, n), BF16),
        grid=(m // tm, n // tn),
        in_specs=[yspec, yspec, yspec, gspec(COL_GATE_SSM), gspec(COL_GATE_GLA), gspec(COL_GATE_MOBA),
                  wspec, wspec, wspec],
        out_specs=pl.BlockSpec((tm, tn), lambda i, j: (i, j)),
        compiler_params=_cparams(("parallel", "arbitrary")),
        name="merge",
    )(y_ssm, y_gla, y_moba, proj, proj, proj, w_s, w_g, w_m)


def _resid_matmul_kernel(a_ref, w_ref, x_ref, o_ref):
    o_ref[...] = x_ref[...] + jnp.dot(a_ref[...], w_ref[...].astype(BF16), preferred_element_type=F32)


def resid_matmul(a, w_all, x, layer, *, tm, tn):
    m, kdim = a.shape
    n = w_all.shape[-1]
    return pl.pallas_call(
        _resid_matmul_kernel,
        out_shape=jax.ShapeDtypeStruct((m, n), F32),
        grid=(m // tm, n // tn),
        in_specs=[
            pl.BlockSpec((tm, kdim), lambda i, j: (i, 0)),
            pl.BlockSpec((None, kdim, tn), lambda i, j: (layer, 0, j)),
            pl.BlockSpec((tm, tn), lambda i, j: (i, j)),
        ],
        out_specs=pl.BlockSpec((tm, tn), lambda i, j: (i, j)),
        compiler_params=_cparams(("parallel", "arbitrary")),
        name="resid_matmul",
    )(a, w_all, x)


def _ffn_up_kernel(x_ref, g_ref, wg_ref, wu_ref, o_ref, hn_ref):
    @pl.when(pl.program_id(1) == 0)
    def _():
        x = x_ref[...]
        hn_ref[...] = (x * _rms_scale(x) * g_ref[...]).astype(BF16)

    h = hn_ref[...]
    gate = jnp.dot(h, wg_ref[...].astype(BF16), preferred_element_type=F32)
    up = jnp.dot(h, wu_ref[...].astype(BF16), preferred_element_type=F32)
    o_ref[...] = (_silu(gate) * up).astype(o_ref.dtype)


def ffn_up(x, g, w_gate, w_up, layer, *, tm, tn):
    m, d = x.shape
    n = w_gate.shape[-1]
    wspec = pl.BlockSpec((None, d, tn), lambda i, j: (layer, 0, j))
    return pl.pallas_call(
        _ffn_up_kernel,
        out_shape=jax.ShapeDtypeStruct((m, n), BF16),
        grid=(m // tm, n // tn),
        in_specs=[
            pl.BlockSpec((tm, d), lambda i, j: (i, 0)),
            pl.BlockSpec((None, 1, d), lambda i, j: (layer, 0, 0)),
            wspec, wspec,
        ],
        out_specs=pl.BlockSpec((tm, tn), lambda i, j: (i, j)),
        scratch_shapes=[pltpu.VMEM((tm, d), BF16)],
        compiler_params=_cparams(("parallel", "arbitrary")),
        name="ffn_up",
    )(x, g, w_gate, w_up)


def _gla_tri(tg):
    r = np.arange(tg)
    same_chunk = (r[:, None] // GLA_CHUNK) == (r[None, :] // GLA_CHUNK)
    return jnp.asarray(np.where(same_chunk & (r[None, :] <= r[:, None]), 1.0, 0.0), dtype=BF16)


def kernel(x, norm1_g, w_in, ssm_lambda_re, ssm_lambda_im, ssm_log_dt, ssm_b_re, ssm_b_im, ssm_c_re, ssm_c_im, ssm_d, ssm_w_glu, ssm_b_glu, gla_w_gate2, gla_b_gate2, gla_norm_g, moba_q_norm_g, moba_k_norm_g, w_branch_ssm, w_branch_gla, w_branch_moba, w_out, norm2_g, ffn_w_gate, ffn_w_up, ffn_w_down):
    batch, seq, d = x.shape
    depth = w_in.shape[0]
    m = batch * seq

    pad_glr = lambda t, axis: jnp.pad(
        t, [(0, LANES - GLA_GATE_RANK) if a == axis else (0, 0) for a in range(3)]).astype(BF16)
    w_glr = pad_glr(w_in[..., ORIG_GLR:ORIG_GLR + GLA_GATE_RANK], 2)
    wg2_pad = pad_glr(gla_w_gate2, 1)
    a_re, a_im, bb_re, bb_im = s5_prep(ssm_lambda_re, ssm_lambda_im, ssm_log_dt, ssm_b_re, ssm_b_im)
    s5_win, s5_wout, a_re_t, a_im_t = s5_layout(a_re, a_im, bb_re, bb_im, ssm_c_re, ssm_c_im)
    row3 = lambda t: t[:, None, :]
    n1, n2 = row3(norm1_g), row3(norm2_g)
    ssm_d3, b_glu3 = row3(ssm_d), row3(ssm_b_glu)
    bg2, gng = row3(gla_b_gate2), row3(gla_norm_g)
    mqg, mkg = row3(moba_q_norm_g), row3(moba_k_norm_g)

    tg = 256
    tri = _gla_tri(tg)
    xf = x.reshape(m, d)
    for l in range(depth):
        proj, glr = in_proj(xf, n1, w_in, w_glr, l, tm=1024, tn=1024)
        y_ssm = s5_mixer(proj, s5_win, s5_wout, a_re_t, a_im_t, ssm_d3, ssm_w_glu, b_glu3, l,
                         batch=batch, seq=seq, tc=256)
        y_gla = gla_mixer(proj, glr, wg2_pad, bg2, gng, tri, l, batch=batch, seq=seq, tg=tg)
        y_moba = moba_mixer(proj, mqg, mkg, l, batch=batch, seq=seq)
        merged = merge_branches(y_ssm, y_gla, y_moba, proj, w_branch_ssm, w_branch_gla, w_branch_moba, l,
                                tm=1024, tn=512)
        xf = resid_matmul(merged, w_out, xf, l, tm=1024, tn=512)
        act = ffn_up(xf, n2, ffn_w_gate, ffn_w_up, l, tm=1024, tn=512)
        xf = resid_matmul(act, ffn_w_down, xf, l, tm=1024, tn=256)
    return xf.reshape(batch, seq, d)
```

```python
import functools

import jax
import jax.numpy as jnp
import numpy as np
from jax import lax
from jax.experimental import pallas as pl
from jax.experimental.pallas import tpu as pltpu

F32 = jnp.float32
BF16 = jnp.bfloat16

D_MODEL = 2048
DEPTH = 4
RMS_EPS = 1e-6
SSM_WIDTH = D_MODEL // 2
SSM_GROUP_SIZE = 16
SSM_GROUPS = SSM_WIDTH // SSM_GROUP_SIZE
SSM_STATE = 64
GLA_HEADS = 4
GLA_KEY_WIDTH = D_MODEL // 4
GLA_VALUE_WIDTH = D_MODEL // 2
GLA_DK = GLA_KEY_WIDTH // GLA_HEADS
GLA_DV = GLA_VALUE_WIDTH // GLA_HEADS
GLA_GATE_RANK = 16
GLA_GATE_NORM = 16.0
GLA_CHUNK = 64
MOBA_WIDTH = D_MODEL // 2
MOBA_HEAD_DIM = 128
MOBA_HEADS = MOBA_WIDTH // MOBA_HEAD_DIM
MOBA_BLOCK = 256
MOBA_TOPK = 3
FFN_HIDDEN = -(-8 * D_MODEL // (3 * 256)) * 256

LANES = 128
SUBLANES = 8
MXU_DIM = 256
VMEM_LIMIT = 56 * 1024 * 1024

COL_SSM = 0
COL_GLA_Q = COL_SSM + SSM_WIDTH
COL_GLA_K = COL_GLA_Q + GLA_KEY_WIDTH
COL_GLA_V = COL_GLA_K + GLA_KEY_WIDTH
COL_GLA_R = COL_GLA_V + GLA_VALUE_WIDTH
COL_MOBA_Q = COL_GLA_R + GLA_VALUE_WIDTH
COL_MOBA_K = COL_MOBA_Q + MOBA_WIDTH
COL_MOBA_V = COL_MOBA_K + MOBA_WIDTH
COL_GATE_SSM = COL_MOBA_V + MOBA_WIDTH
COL_GATE_GLA = COL_GATE_SSM + D_MODEL
COL_GATE_MOBA = COL_GATE_GLA + D_MODEL
PROJ_WIDTH = COL_GATE_MOBA + D_MODEL
ORIG_GLR = COL_MOBA_Q

S5_PAIRS = SSM_GROUPS // 2
S5_OCTETS = S5_PAIRS // SUBLANES
S5_OCT_CH = SSM_WIDTH // S5_OCTETS
S5_PITCH = S5_PAIRS + 4

NEG_BIG = -1e30


def _cparams(sem):
    return pltpu.CompilerParams(dimension_semantics=sem, vmem_limit_bytes=VMEM_LIMIT)


def _const_spec(shape, index_map):
    return pl.BlockSpec(shape, index_map, pipeline_mode=pl.Buffered(1))


def _sigmoid(x):
    return 1.0 / (1.0 + jnp.exp(-x))


def _silu(x):
    return x * _sigmoid(x)


def _gelu_tanh(x):
    c = np.float32(np.sqrt(2.0 / np.pi))
    return 0.5 * x * (1.0 + jnp.tanh(c * (x + np.float32(0.044715) * (x * x * x))))


def _rms_scale(x):
    return lax.rsqrt(jnp.mean(x * x, axis=-1, keepdims=True) + RMS_EPS)


def _rmsnorm_kernel(x_ref, g_ref, o_ref):
    x = x_ref[...]
    o_ref[...] = (x * _rms_scale(x) * g_ref[...]).astype(o_ref.dtype)


def rmsnorm(x, g, layer, *, tm):
    m, d = x.shape
    return pl.pallas_call(
        _rmsnorm_kernel,
        out_shape=jax.ShapeDtypeStruct((m, d), BF16),
        grid=(m // tm,),
        in_specs=[pl.BlockSpec((tm, d), lambda i: (i, 0)),
                  _const_spec((None, 1, d), lambda i: (layer, 0, 0))],
        out_specs=pl.BlockSpec((tm, d), lambda i: (i, 0)),
        compiler_params=_cparams(("parallel",)),
        name="rmsnorm",
    )(x, g)


_NT_DIMS = (((1,), (1,)), ((), ()))


def _in_proj_kernel(xn_ref, wa_ref, wb_ref, o_ref, wbf_ref, *, n_aligned):
    j = pl.program_id(0)
    first = pl.program_id(1) == 0
    tn = wa_ref.shape[0]
    sh = GLA_GATE_RANK

    @pl.when(jnp.logical_and(first, j < n_aligned))
    def _():
        wbf_ref[...] = wa_ref[...].astype(BF16)

    @pl.when(jnp.logical_and(first, j >= n_aligned))
    def _():
        wbf_ref[0:tn - sh, :] = wa_ref[sh:tn, :].astype(BF16)
        wbf_ref[tn - sh:tn, :] = wb_ref[...].astype(BF16)

    o_ref[...] = lax.dot_general(xn_ref[...], wbf_ref[...], _NT_DIMS,
                                 preferred_element_type=F32).astype(o_ref.dtype)


def in_proj(xn, w_in_t, layer, *, tm, tn):
    m, d = xn.shape
    sh = GLA_GATE_RANK
    assert ORIG_GLR % tn == 0 and PROJ_WIDTH % tn == 0 and tn % sh == 0
    kern = functools.partial(_in_proj_kernel, n_aligned=ORIG_GLR // tn)
    return pl.pallas_call(
        kern,
        out_shape=jax.ShapeDtypeStruct((m, PROJ_WIDTH), BF16),
        grid=(PROJ_WIDTH // tn, m // tm),
        in_specs=[
            pl.BlockSpec((tm, d), lambda j, i: (i, 0)),
            pl.BlockSpec((None, tn, d), lambda j, i: (layer, j, 0)),
            pl.BlockSpec((None, sh, d), lambda j, i: (layer, (j + 1) * (tn // sh), 0)),
        ],
        out_specs=pl.BlockSpec((tm, tn), lambda j, i: (i, j)),
        scratch_shapes=[pltpu.VMEM((tn, d), BF16)],
        compiler_params=_cparams(("arbitrary", "arbitrary")),
        name="in_proj",
    )(xn, w_in_t, w_in_t)


def _s5_prep_kernel(lre_ref, lim_ref, ldt_ref, bre_ref, bim_ref, are_ref, aim_ref, bbre_ref, bbim_ref):
    lre = lre_ref[...]
    lim = lim_ref[...]
    dt = jnp.exp(ldt_ref[...])
    mag = jnp.exp(lre * dt)
    ang = lim * dt
    a_re = mag * jnp.cos(ang)
    a_im = mag * jnp.sin(ang)
    den = lre * lre + lim * lim
    n_re = a_re - 1.0
    n_im = a_im
    z_re = (n_re * lre + n_im * lim) / den
    z_im = (n_im * lre - n_re * lim) / den
    b_re = bre_ref[...]
    b_im = bim_ref[...]
    are_ref[...] = a_re
    aim_ref[...] = a_im
    bbre_ref[...] = z_re * b_re - z_im * b_im
    bbim_ref[...] = z_re * b_im + z_im * b_re


def s5_prep(lam_re, lam_im, log_dt, b_re, b_im):
    depth, g, p = lam_re.shape
    s = b_re.shape[-1]
    rows, cols = depth * g, p * s
    rep = lambda t: jnp.broadcast_to(t[..., None], (depth, g, p, s)).reshape(rows, cols)
    ldt = jnp.broadcast_to(log_dt[..., None, None], (depth, g, p, s)).reshape(rows, cols)
    sds = jax.ShapeDtypeStruct((rows, cols), F32)
    a_re, a_im, bb_re, bb_im = pl.pallas_call(
        _s5_prep_kernel,
        out_shape=(sds, sds, sds, sds),
        compiler_params=pltpu.CompilerParams(vmem_limit_bytes=VMEM_LIMIT),
        name="s5_prep",
    )(rep(lam_re), rep(lam_im), ldt, b_re.reshape(rows, cols), b_im.reshape(rows, cols))
    unrep = lambda t: t.reshape(depth, g, p, s)[..., 0]
    return unrep(a_re), unrep(a_im), bb_re.reshape(depth, g, p, s), bb_im.reshape(depth, g, p, s)


def s5_layout(a_re, a_im, bb_re, bb_im, c_re, c_im):
    depth = a_re.shape[0]
    gpo = SSM_GROUPS // S5_OCTETS
    eye = jnp.eye(gpo, dtype=F32)

    def in_blocks(bb):
        t = bb.reshape(depth, S5_OCTETS, gpo, SSM_STATE, SSM_GROUP_SIZE)
        w = jnp.einsum('logpc,gh->logchp', t, eye)
        return w.reshape(depth, S5_OCTETS, S5_OCT_CH, SUBLANES, LANES)

    w_in = jnp.stack([in_blocks(bb_re), in_blocks(bb_im)], axis=4)
    w_in = w_in.reshape(depth, S5_OCTETS, S5_OCT_CH, SUBLANES * 2 * LANES).astype(BF16)

    def out_blocks(c):
        t = c.reshape(depth, S5_OCTETS, gpo, SSM_GROUP_SIZE, SSM_STATE)
        w = jnp.einsum('logcp,gh->logphc', t, eye)
        return w.reshape(depth, S5_OCTETS, SUBLANES, LANES, S5_OCT_CH)

    w_out = jnp.stack([out_blocks(c_re), out_blocks(-c_im)], axis=3)
    w_out = w_out.reshape(depth, S5_OCTETS, SUBLANES * 2 * LANES, S5_OCT_CH).astype(BF16)
    tab = lambda a: a.reshape(depth, S5_PAIRS, LANES)
    return w_in, w_out, tab(a_re), tab(a_im)


def _s5_kernel(u_ref, win_ref, wout_ref, are_ref, aim_ref, d_ref, wglu_ref, bglu_ref, o_ref,
               bur_ref, bui_ref, sre_ref, sim_ref, y_ref, *, tc):
    @pl.when(pl.program_id(1) == 0)
    def _():
        sre_ref[...] = jnp.zeros_like(sre_ref)
        sim_ref[...] = jnp.zeros_like(sim_ref)

    u = u_ref[...]
    pair_w = 2 * LANES
    for o in range(S5_OCTETS):
        uo = u[:, o * S5_OCT_CH:(o + 1) * S5_OCT_CH]
        for jl in range(SUBLANES):
            res = jnp.dot(uo, win_ref[o, :, jl * pair_w:(jl + 1) * pair_w],
                          preferred_element_type=F32)
            j = o * SUBLANES + jl
            bur_ref[pl.ds(j, tc, stride=S5_PITCH), :] = res[:, :LANES]
            bui_ref[pl.ds(j, tc, stride=S5_PITCH), :] = res[:, LANES:]

    a_re = [are_ref[q * SUBLANES:(q + 1) * SUBLANES, :] for q in range(S5_OCTETS)]
    a_im = [aim_ref[q * SUBLANES:(q + 1) * SUBLANES, :] for q in range(S5_OCTETS)]

    def step(t, carry):
        xr, xi = carry
        nr, ni = [], []
        for q in range(S5_OCTETS):
            row = t * S5_PITCH + q * SUBLANES
            br = bur_ref[pl.ds(row, SUBLANES), :]
            bi = bui_ref[pl.ds(row, SUBLANES), :]
            r = a_re[q] * xr[q] - a_im[q] * xi[q] + br
            i = a_re[q] * xi[q] + a_im[q] * xr[q] + bi
            bur_ref[pl.ds(row, SUBLANES), :] = r
            bui_ref[pl.ds(row, SUBLANES), :] = i
            nr.append(r)
            ni.append(i)
        return tuple(nr), tuple(ni)

    init = (tuple(sre_ref[q * SUBLANES:(q + 1) * SUBLANES, :] for q in range(S5_OCTETS)),
            tuple(sim_ref[q * SUBLANES:(q + 1) * SUBLANES, :] for q in range(S5_OCTETS)))
    xr, xi = lax.fori_loop(0, tc, step, init, unroll=2)
    for q in range(S5_OCTETS):
        sre_ref[q * SUBLANES:(q + 1) * SUBLANES, :] = xr[q]
        sim_ref[q * SUBLANES:(q + 1) * SUBLANES, :] = xi[q]

    for o in range(S5_OCTETS):
        acc = None
        for jl in range(SUBLANES):
            j = o * SUBLANES + jl
            xr_j = bur_ref[pl.ds(j, tc, stride=S5_PITCH), :]
            xi_j = bui_ref[pl.ds(j, tc, stride=S5_PITCH), :]
            lhs = jnp.concatenate([xr_j, xi_j], axis=1).astype(BF16)
            part = jnp.dot(lhs, wout_ref[o, jl * pair_w:(jl + 1) * pair_w, :],
                           preferred_element_type=F32)
            acc = part if acc is None else acc + part
        y_ref[:, o * S5_OCT_CH:(o + 1) * S5_OCT_CH] = acc

    y = y_ref[...] + d_ref[...] * u.astype(F32)
    z = _gelu_tanh(y)
    gate = jnp.dot(z.astype(BF16), wglu_ref[...].astype(BF16), preferred_element_type=F32) + bglu_ref[...]
    o_ref[...] = (z * _sigmoid(gate)).astype(o_ref.dtype)


def s5_mixer(proj, w_in_blk, w_out_blk, a_re, a_im, d_skip, w_glu, b_glu, layer, *, batch, seq, tc):
    nt = seq // tc
    w = SSM_WIDTH
    kern = functools.partial(_s5_kernel, tc=tc)
    return pl.pallas_call(
        kern,
        out_shape=jax.ShapeDtypeStruct((batch * seq, w), BF16),
        grid=(batch, nt),
        in_specs=[
            pl.BlockSpec((tc, w), lambda b, c: (b * nt + c, COL_SSM // w)),
            _const_spec((None,) + w_in_blk.shape[1:], lambda b, c: (layer, 0, 0, 0)),
            _const_spec((None,) + w_out_blk.shape[1:], lambda b, c: (layer, 0, 0, 0)),
            _const_spec((None, S5_PAIRS, LANES), lambda b, c: (layer, 0, 0)),
            _const_spec((None, S5_PAIRS, LANES), lambda b, c: (layer, 0, 0)),
            _const_spec((None, 1, w), lambda b, c: (layer, 0, 0)),
            _const_spec((None, w, w), lambda b, c: (layer, 0, 0)),
            _const_spec((None, 1, w), lambda b, c: (layer, 0, 0)),
        ],
        out_specs=pl.BlockSpec((tc, w), lambda b, c: (b * nt + c, 0)),
        scratch_shapes=[
            pltpu.VMEM((tc * S5_PITCH, LANES), F32),
            pltpu.VMEM((tc * S5_PITCH, LANES), F32),
            pltpu.VMEM((S5_PAIRS, LANES), F32),
            pltpu.VMEM((S5_PAIRS, LANES), F32),
            pltpu.VMEM((tc, w), F32),
        ],
        compiler_params=_cparams(("parallel", "arbitrary")),
        name="s5_mixer",
    )(proj, w_in_blk, w_out_blk, a_re, a_im, d_skip, w_glu, b_glu)


def _split3(x):
    hi = x.astype(BF16)
    r1 = x - hi.astype(F32)
    mid = r1.astype(BF16)
    lo = (r1 - mid.astype(F32)).astype(BF16)
    return hi, mid, lo


def _gla_kernel(q_ref, k_ref, v_ref, r_ref, xn_ref, wglr_ref, wg_ref, bg_ref, ng_ref, tri_ref, o_ref, st_ref,
                *, tg):
    @pl.when(pl.program_id(1) == 0)
    def _():
        st_ref[...] = jnp.zeros_like(st_ref)

    c = GLA_CHUNK
    wl = wglr_ref[...].astype(BF16)
    wl = jnp.concatenate([wl, jnp.zeros((LANES - GLA_GATE_RANK, wl.shape[1]), BF16)], axis=0)
    glr = lax.dot_general(xn_ref[...], wl, _NT_DIMS, preferred_element_type=F32).astype(BF16)
    glog = jnp.dot(glr, wg_ref[...], preferred_element_type=F32) + bg_ref[...]
    g = (jnp.minimum(glog, 0.0) - jnp.log1p(jnp.exp(-jnp.abs(glog)))) / GLA_GATE_NORM
    tri = tri_ref[...]
    bcum = None
    for piece in _split3(g):
        part = jnp.dot(tri, piece, preferred_element_type=F32)
        bcum = part if bcum is None else bcum + part

    row = lax.broadcasted_iota(jnp.int32, (c, c), 0)
    col = lax.broadcasted_iota(jnp.int32, (c, c), 1)
    causal = col <= row
    scale = np.float32(GLA_DK ** -0.5)
    ng = ng_ref[...]
    for n in range(tg // c):
        rs = slice(n * c, (n + 1) * c)
        for h in range(GLA_HEADS):
            ks = slice(h * GLA_DK, (h + 1) * GLA_DK)
            vs = slice(h * GLA_DV, (h + 1) * GLA_DV)
            b = bcum[rs, ks]
            bl = b[c - 1:c, :]
            qh = q_ref[rs, ks].astype(F32) * scale
            kh = k_ref[rs, ks].astype(F32)
            vh = v_ref[rs, vs]
            q_dec = (qh * jnp.exp(b)).astype(BF16)
            k_inv = (kh * jnp.exp(-b)).astype(BF16)
            k_dec = (kh * jnp.exp(bl - b)).astype(BF16)
            attn = lax.dot_general(q_dec, k_inv, (((1,), (1,)), ((), ())), preferred_element_type=F32)
            attn = jnp.where(causal, attn, 0.0).astype(BF16)
            st = st_ref[h]
            o = (jnp.dot(attn, vh, preferred_element_type=F32)
                 + lax.dot_general(q_dec, st.astype(BF16), (((1,), (1,)), ((), ())),
                                   preferred_element_type=F32))
            kvt = lax.dot_general(vh, k_dec, (((0,), (0,)), ((), ())), preferred_element_type=F32)
            st_ref[h] = jnp.exp(bl) * st + kvt
            o = o * _rms_scale(o) * ng
            o_ref[rs, vs] = (o * _silu(r_ref[rs, vs].astype(F32))).astype(o_ref.dtype)


def gla_mixer(proj, xn, w_in_t, w_gate2_pad, b_gate2, norm_g, tri, layer, *, batch, seq, tg):
    nt = seq // tg
    kw, vw = GLA_KEY_WIDTH, GLA_VALUE_WIDTH
    d = xn.shape[1]
    sh = GLA_GATE_RANK
    kern = functools.partial(_gla_kernel, tg=tg)
    return pl.pallas_call(
        kern,
        out_shape=jax.ShapeDtypeStruct((batch * seq, vw), BF16),
        grid=(batch, nt),
        in_specs=[
            pl.BlockSpec((tg, kw), lambda b, c: (b * nt + c, COL_GLA_Q // kw)),
            pl.BlockSpec((tg, kw), lambda b, c: (b * nt + c, COL_GLA_K // kw)),
            pl.BlockSpec((tg, vw), lambda b, c: (b * nt + c, COL_GLA_V // vw)),
            pl.BlockSpec((tg, vw), lambda b, c: (b * nt + c, COL_GLA_R // vw)),
            pl.BlockSpec((tg, d), lambda b, c: (b * nt + c, 0)),
            _const_spec((None, sh, d), lambda b, c: (layer, ORIG_GLR // sh, 0)),
            _const_spec((None, LANES, kw), lambda b, c: (layer, 0, 0)),
            _const_spec((None, 1, kw), lambda b, c: (layer, 0, 0)),
            _const_spec((None, 1, GLA_DV), lambda b, c: (layer, 0, 0)),
            _const_spec((tg, tg), lambda b, c: (0, 0)),
        ],
        out_specs=pl.BlockSpec((tg, vw), lambda b, c: (b * nt + c, 0)),
        scratch_shapes=[pltpu.VMEM((GLA_HEADS, GLA_DV, GLA_DK), F32)],
        compiler_params=_cparams(("parallel", "arbitrary")),
        name="gla_mixer",
    )(proj, proj, proj, proj, xn, w_in_t, w_gate2_pad, b_gate2, norm_g, tri)


def _moba_kernel(q_ref, k_ref, v_ref, qg_ref, kg_ref, o_ref, kn_ref, vt_ref, km_ref, *, nb):
    bs = MOBA_BLOCK
    kg = kg_ref[...]
    for jb in range(nb):
        kb = k_ref[jb * bs:(jb + 1) * bs, :].astype(F32)
        kn = kb * _rms_scale(kb) * kg
        kn_ref[jb * bs:(jb + 1) * bs, :] = kn.astype(BF16)
        km_ref[jb:jb + 1, :] = jnp.mean(kn, axis=0, keepdims=True)
    vt_ref[...] = v_ref[...].astype(F32).T.astype(BF16)

    qg = qg_ref[...]
    km = km_ref[...]
    blk = lax.broadcasted_iota(jnp.int32, (nb, bs), 0)
    krow = lax.broadcasted_iota(jnp.int32, (bs, bs), 0)
    qcol = lax.broadcasted_iota(jnp.int32, (bs, bs), 1)
    for i in range(nb):
        q = q_ref[i * bs:(i + 1) * bs, :].astype(F32)
        qn = (q * _rms_scale(q) * qg) * np.float32(MOBA_HEAD_DIM ** -0.5)
        qb = qn.astype(BF16)
        bias_t = None
        if i > MOBA_TOPK:
            sbt = lax.dot_general(km, qn, (((1,), (1,)), ((), ())),
                                  precision=lax.Precision.HIGHEST, preferred_element_type=F32)
            cnt = jnp.zeros((nb, bs), F32)
            for jp in range(i):
                sp = sbt[jp:jp + 1, :]
                beats = jnp.where(sp > sbt, 1.0, jnp.where(sp == sbt, jnp.where(blk > jp, 1.0, 0.0), 0.0))
                cnt = cnt + beats
            bias_t = jnp.where(cnt < MOBA_TOPK, 0.0, NEG_BIG)
        nk = (i + 1) * bs
        st = lax.dot_general(kn_ref[0:nk, :], qb, (((1,), (1,)), ((), ())),
                             preferred_element_type=F32)
        pieces = []
        for j in range(i + 1):
            sj = st[j * bs:(j + 1) * bs, :]
            if j == i:
                sj = jnp.where(krow <= qcol, sj, NEG_BIG)
            elif bias_t is not None:
                sj = sj + bias_t[j:j + 1, :]
            pieces.append(sj)
        mx = pieces[0]
        for pj in pieces[1:]:
            mx = jnp.maximum(mx, pj)
        m = jnp.max(mx, axis=0, keepdims=True)
        ps = [jnp.exp(pj - m) for pj in pieces]
        lsum = ps[0]
        for pj in ps[1:]:
            lsum = lsum + pj
        l = jnp.sum(lsum, axis=0, keepdims=True)
        pt = jnp.concatenate([pj.astype(BF16) for pj in ps], axis=0)
        ot = jnp.dot(vt_ref[:, 0:nk], pt, preferred_element_type=F32)
        o_ref[i * bs:(i + 1) * bs, :] = (ot / l).T.astype(o_ref.dtype)


def moba_mixer(proj, q_g, k_g, layer, *, batch, seq):
    bs, hd, nh = MOBA_BLOCK, MOBA_HEAD_DIM, MOBA_HEADS
    nb = seq // bs
    kern = functools.partial(_moba_kernel, nb=nb)
    blk = lambda col: pl.BlockSpec((seq, hd), lambda b, h: (b, col // hd + h))
    return pl.pallas_call(
        kern,
        out_shape=jax.ShapeDtypeStruct((batch * seq, MOBA_WIDTH), BF16),
        grid=(batch, nh),
        in_specs=[
            blk(COL_MOBA_Q), blk(COL_MOBA_K), blk(COL_MOBA_V),
            _const_spec((None, 1, hd), lambda b, h: (layer, 0, 0)),
            _const_spec((None, 1, hd), lambda b, h: (layer, 0, 0)),
        ],
        out_specs=pl.BlockSpec((seq, hd), lambda b, h: (b, h)),
        scratch_shapes=[pltpu.VMEM((seq, hd), BF16), pltpu.VMEM((hd, seq), BF16),
                        pltpu.VMEM((nb, hd), F32)],
        compiler_params=_cparams(("parallel", "parallel")),
        name="moba_mixer",
    )(proj, proj, proj, q_g, k_g)


_WEIGHT_OUTER = ("arbitrary", "arbitrary")


def _merge_kernel(ys_ref, yg_ref, ym_ref, gs_ref, gg_ref, gm_ref, ws_ref, wg_ref, wm_ref, o_ref,
                  wsb_ref, wgb_ref, wmb_ref):
    @pl.when(pl.program_id(1) == 0)
    def _():
        wsb_ref[...] = ws_ref[...].astype(BF16)
        wgb_ref[...] = wg_ref[...].astype(BF16)
        wmb_ref[...] = wm_ref[...].astype(BF16)

    def branch(g_ref, y_ref, w_ref):
        return _sigmoid(g_ref[...].astype(F32)) * jnp.dot(y_ref[...], w_ref[...], preferred_element_type=F32)

    acc = branch(gs_ref, ys_ref, wsb_ref) + branch(gg_ref, yg_ref, wgb_ref) + branch(gm_ref, ym_ref, wmb_ref)
    o_ref[...] = acc.astype(o_ref.dtype)


def merge_branches(y_ssm, y_gla, y_moba, proj, w_s, w_g, w_m, layer, *, tm, tn):
    m, kdim = y_ssm.shape
    n = w_s.shape[-1]
    yspec = pl.BlockSpec((tm, kdim), lambda j, i: (i, 0))
    gspec = lambda col: pl.BlockSpec((tm, tn), lambda j, i: (i, col // tn + j))
    wspec = pl.BlockSpec((None, kdim, tn), lambda j, i: (layer, 0, j))
    return pl.pallas_call(
        _merge_kernel,
        out_shape=jax.ShapeDtypeStruct((m, n), BF16),
        grid=(n // tn, m // tm),
        in_specs=[yspec, yspec, yspec, gspec(COL_GATE_SSM), gspec(COL_GATE_GLA), gspec(COL_GATE_MOBA),
                  wspec, wspec, wspec],
        out_specs=pl.BlockSpec((tm, tn), lambda j, i: (i, j)),
        scratch_shapes=[pltpu.VMEM((kdim, tn), BF16)] * 3,
        compiler_params=_cparams(_WEIGHT_OUTER),
        name="merge",
    )(y_ssm, y_gla, y_moba, proj, proj, proj, w_s, w_g, w_m)


def _out_proj_kernel(a_ref, w_ref, x_ref, g_ref, o_ref, h_ref, wbf_ref):
    @pl.when(pl.program_id(0) == 0)
    def _():
        wbf_ref[...] = w_ref[...].astype(BF16)

    xn = x_ref[...] + jnp.dot(a_ref[...], wbf_ref[...], preferred_element_type=F32)
    o_ref[...] = xn
    h_ref[...] = (xn * _rms_scale(xn) * g_ref[...]).astype(h_ref.dtype)


def out_proj_norm(a, w_all, x, g, layer, *, tm):
    m, kdim = a.shape
    n = w_all.shape[-1]
    row = lambda width: pl.BlockSpec((tm, width), lambda i: (i, 0))
    return pl.pallas_call(
        _out_proj_kernel,
        out_shape=(jax.ShapeDtypeStruct((m, n), F32), jax.ShapeDtypeStruct((m, n), BF16)),
        grid=(m // tm,),
        in_specs=[row(kdim), _const_spec((None, kdim, n), lambda i: (layer, 0, 0)), row(n),
                  _const_spec((None, 1, n), lambda i: (layer, 0, 0))],
        out_specs=(row(n), row(n)),
        scratch_shapes=[pltpu.VMEM((kdim, n), BF16)],
        compiler_params=_cparams(("arbitrary",)),
        name="out_proj",
    )(a, w_all, x, g)


def _resid_matmul_kernel(a_ref, w_ref, x_ref, o_ref, wbf_ref):
    @pl.when(pl.program_id(1) == 0)
    def _():
        wbf_ref[...] = w_ref[...].astype(BF16)

    o_ref[...] = x_ref[...] + jnp.dot(a_ref[...], wbf_ref[...], preferred_element_type=F32)


def resid_matmul(a, w_all, x, layer, *, tm, tn):
    m, kdim = a.shape
    n = w_all.shape[-1]
    return pl.pallas_call(
        _resid_matmul_kernel,
        out_shape=jax.ShapeDtypeStruct((m, n), F32),
        grid=(n // tn, m // tm),
        in_specs=[
            pl.BlockSpec((tm, kdim), lambda j, i: (i, 0)),
            _const_spec((None, kdim, tn), lambda j, i: (layer, 0, j)),
            pl.BlockSpec((tm, tn), lambda j, i: (i, j)),
        ],
        out_specs=pl.BlockSpec((tm, tn), lambda j, i: (i, j)),
        scratch_shapes=[pltpu.VMEM((kdim, tn), BF16)],
        compiler_params=_cparams(_WEIGHT_OUTER),
        name="resid_matmul",
    )(a, w_all, x)


def _ffn_up_kernel(h_ref, wg_ref, wu_ref, o_ref, wgb_ref, wub_ref):
    @pl.when(pl.program_id(1) == 0)
    def _():
        wgb_ref[...] = wg_ref[...].astype(BF16)
        wub_ref[...] = wu_ref[...].astype(BF16)

    h = h_ref[...]
    gate = jnp.dot(h, wgb_ref[...], preferred_element_type=F32)
    up = jnp.dot(h, wub_ref[...], preferred_element_type=F32)
    o_ref[...] = (_silu(gate) * up).astype(o_ref.dtype)


def ffn_up(h, w_gate, w_up, layer, *, tm, tn):
    m, d = h.shape
    n = w_gate.shape[-1]
    wspec = pl.BlockSpec((None, d, tn), lambda j, i: (layer, 0, j))
    return pl.pallas_call(
        _ffn_up_kernel,
        out_shape=jax.ShapeDtypeStruct((m, n), BF16),
        grid=(n // tn, m // tm),
        in_specs=[pl.BlockSpec((tm, d), lambda j, i: (i, 0)), wspec, wspec],
        out_specs=pl.BlockSpec((tm, tn), lambda j, i: (i, j)),
        scratch_shapes=[pltpu.VMEM((d, tn), BF16)] * 2,
        compiler_params=_cparams(_WEIGHT_OUTER),
        name="ffn_up",
    )(h, w_gate, w_up)


def _gla_tri(tg):
    r = np.arange(tg)
    same_chunk = (r[:, None] // GLA_CHUNK) == (r[None, :] // GLA_CHUNK)
    return jnp.asarray(np.where(same_chunk & (r[None, :] <= r[:, None]), 1.0, 0.0), dtype=BF16)


def kernel(x, norm1_g, w_in, ssm_lambda_re, ssm_lambda_im, ssm_log_dt, ssm_b_re, ssm_b_im, ssm_c_re, ssm_c_im, ssm_d, ssm_w_glu, ssm_b_glu, gla_w_gate2, gla_b_gate2, gla_norm_g, moba_q_norm_g, moba_k_norm_g, w_branch_ssm, w_branch_gla, w_branch_moba, w_out, norm2_g, ffn_w_gate, ffn_w_up, ffn_w_down):
    batch, seq, d = x.shape
    depth = w_in.shape[0]
    m = batch * seq

    w_in_t = jnp.swapaxes(w_in, 1, 2)
    wg2_pad = jnp.pad(gla_w_gate2, ((0, 0), (0, LANES - GLA_GATE_RANK), (0, 0))).astype(BF16)
    a_re, a_im, bb_re, bb_im = s5_prep(ssm_lambda_re, ssm_lambda_im, ssm_log_dt, ssm_b_re, ssm_b_im)
    s5_win, s5_wout, a_re_t, a_im_t = s5_layout(a_re, a_im, bb_re, bb_im, ssm_c_re, ssm_c_im)
    row3 = lambda t: t[:, None, :]
    n1, n2 = row3(norm1_g), row3(norm2_g)
    ssm_d3, b_glu3 = row3(ssm_d), row3(ssm_b_glu)
    bg2, gng = row3(gla_b_gate2), row3(gla_norm_g)
    mqg, mkg = row3(moba_q_norm_g), row3(moba_k_norm_g)

    tg = 256
    tri = _gla_tri(tg)
    xf = x.reshape(m, d)
    for l in range(depth):
        xn = rmsnorm(xf, n1, l, tm=512)
        proj = in_proj(xn, w_in_t, l, tm=1024, tn=1024)
        y_ssm = s5_mixer(proj, s5_win, s5_wout, a_re_t, a_im_t, ssm_d3, ssm_w_glu, b_glu3, l,
                         batch=batch, seq=seq, tc=256)
        y_gla = gla_mixer(proj, xn, w_in_t, wg2_pad, bg2, gng, tri, l, batch=batch, seq=seq, tg=tg)
        y_moba = moba_mixer(proj, mqg, mkg, l, batch=batch, seq=seq)
        merged = merge_branches(y_ssm, y_gla, y_moba, proj, w_branch_ssm, w_branch_gla, w_branch_moba, l,
                                tm=512, tn=1024)
        xf, hn = out_proj_norm(merged, w_out, xf, n2, l, tm=512)
        act = ffn_up(hn, ffn_w_gate, ffn_w_up, l, tm=1024, tn=512)
        xf = resid_matmul(act, ffn_w_down, xf, l, tm=512, tn=1024)
    return xf.reshape(batch, seq, d)
```

```python
import functools

import jax
import jax.numpy as jnp
import numpy as np
from jax import lax
from jax.experimental import pallas as pl
from jax.experimental.pallas import tpu as pltpu

F32 = jnp.float32
BF16 = jnp.bfloat16

D_MODEL = 2048
DEPTH = 4
RMS_EPS = 1e-6
SSM_WIDTH = D_MODEL // 2
SSM_GROUP_SIZE = 16
SSM_GROUPS = SSM_WIDTH // SSM_GROUP_SIZE
SSM_STATE = 64
GLA_HEADS = 4
GLA_KEY_WIDTH = D_MODEL // 4
GLA_VALUE_WIDTH = D_MODEL // 2
GLA_DK = GLA_KEY_WIDTH // GLA_HEADS
GLA_DV = GLA_VALUE_WIDTH // GLA_HEADS
GLA_GATE_RANK = 16
GLA_GATE_NORM = 16.0
GLA_CHUNK = 64
MOBA_WIDTH = D_MODEL // 2
MOBA_HEAD_DIM = 128
MOBA_HEADS = MOBA_WIDTH // MOBA_HEAD_DIM
MOBA_BLOCK = 256
MOBA_TOPK = 3
FFN_HIDDEN = -(-8 * D_MODEL // (3 * 256)) * 256

LANES = 128
SUBLANES = 8
MXU_DIM = 256
VMEM_LIMIT = 56 * 1024 * 1024

COL_SSM = 0
COL_GLA_Q = COL_SSM + SSM_WIDTH
COL_GLA_K = COL_GLA_Q + GLA_KEY_WIDTH
COL_GLA_V = COL_GLA_K + GLA_KEY_WIDTH
COL_GLA_R = COL_GLA_V + GLA_VALUE_WIDTH
COL_MOBA_Q = COL_GLA_R + GLA_VALUE_WIDTH
COL_MOBA_K = COL_MOBA_Q + MOBA_WIDTH
COL_MOBA_V = COL_MOBA_K + MOBA_WIDTH
COL_GATE_SSM = COL_MOBA_V + MOBA_WIDTH
COL_GATE_GLA = COL_GATE_SSM + D_MODEL
COL_GATE_MOBA = COL_GATE_GLA + D_MODEL
PROJ_WIDTH = COL_GATE_MOBA + D_MODEL
ORIG_GLR = COL_MOBA_Q

S5_PAIRS = SSM_GROUPS // 2
S5_OCTETS = S5_PAIRS // SUBLANES
S5_OCT_CH = SSM_WIDTH // S5_OCTETS
S5_PITCH = S5_PAIRS + 4

NEG_BIG = -1e30
LOG2_E = float(np.log2(np.e))
BF16_ROWS = 2 * SUBLANES


def _cparams(sem):
    return pltpu.CompilerParams(dimension_semantics=sem, vmem_limit_bytes=VMEM_LIMIT)


def _const_spec(shape, index_map):
    return pl.BlockSpec(shape, index_map, pipeline_mode=pl.Buffered(1))


def _sigmoid(x):
    return 1.0 / (1.0 + jnp.exp(-x))


def _silu(x):
    return x * _sigmoid(x)


def _gelu_tanh(x):
    c = np.float32(np.sqrt(2.0 / np.pi))
    return 0.5 * x * (1.0 + jnp.tanh(c * (x + np.float32(0.044715) * (x * x * x))))


def _rms_scale(x):
    return lax.rsqrt(jnp.mean(x * x, axis=-1, keepdims=True) + RMS_EPS)


def _rmsnorm_kernel(x_ref, g_ref, o_ref):
    x = x_ref[...]
    o_ref[...] = (x * _rms_scale(x) * g_ref[...]).astype(o_ref.dtype)


def rmsnorm(x, g, layer, *, tm):
    m, d = x.shape
    return pl.pallas_call(
        _rmsnorm_kernel,
        out_shape=jax.ShapeDtypeStruct((m, d), BF16),
        grid=(m // tm,),
        in_specs=[pl.BlockSpec((tm, d), lambda i: (i, 0)),
                  _const_spec((None, 1, d), lambda i: (layer, 0, 0))],
        out_specs=pl.BlockSpec((tm, d), lambda i: (i, 0)),
        compiler_params=_cparams(("parallel",)),
        name="rmsnorm",
    )(x, g)


_NT_DIMS = (((1,), (1,)), ((), ()))


def _in_proj_kernel(xn_ref, wa_ref, wb_ref, o_ref, wbf_ref, *, n_aligned):
    j = pl.program_id(0)
    first = pl.program_id(1) == 0
    tn = wa_ref.shape[0]
    sh = GLA_GATE_RANK

    @pl.when(jnp.logical_and(first, j < n_aligned))
    def _():
        wbf_ref[...] = wa_ref[...].astype(BF16)

    @pl.when(jnp.logical_and(first, j >= n_aligned))
    def _():
        wbf_ref[0:tn - sh, :] = wa_ref[sh:tn, :].astype(BF16)
        wbf_ref[tn - sh:tn, :] = wb_ref[...].astype(BF16)

    o_ref[...] = lax.dot_general(xn_ref[...], wbf_ref[...], _NT_DIMS,
                                 preferred_element_type=F32).astype(o_ref.dtype)


def in_proj(xn, w_in_t, layer, *, tm, tn):
    m, d = xn.shape
    sh = GLA_GATE_RANK
    assert ORIG_GLR % tn == 0 and PROJ_WIDTH % tn == 0 and tn % sh == 0
    kern = functools.partial(_in_proj_kernel, n_aligned=ORIG_GLR // tn)
    return pl.pallas_call(
        kern,
        out_shape=jax.ShapeDtypeStruct((m, PROJ_WIDTH), BF16),
        grid=(PROJ_WIDTH // tn, m // tm),
        in_specs=[
            pl.BlockSpec((tm, d), lambda j, i: (i, 0)),
            pl.BlockSpec((None, tn, d), lambda j, i: (layer, j, 0)),
            pl.BlockSpec((None, sh, d), lambda j, i: (layer, (j + 1) * (tn // sh), 0)),
        ],
        out_specs=pl.BlockSpec((tm, tn), lambda j, i: (i, j)),
        scratch_shapes=[pltpu.VMEM((tn, d), BF16)],
        compiler_params=_cparams(("arbitrary", "arbitrary")),
        name="in_proj",
    )(xn, w_in_t, w_in_t)


def _s5_prep_kernel(lre_ref, lim_ref, ldt_ref, bre_ref, bim_ref, are_ref, aim_ref, bbre_ref, bbim_ref):
    lre = lre_ref[...]
    lim = lim_ref[...]
    dt = jnp.exp(ldt_ref[...])
    mag = jnp.exp(lre * dt)
    ang = lim * dt
    a_re = mag * jnp.cos(ang)
    a_im = mag * jnp.sin(ang)
    den = lre * lre + lim * lim
    n_re = a_re - 1.0
    n_im = a_im
    z_re = (n_re * lre + n_im * lim) / den
    z_im = (n_im * lre - n_re * lim) / den
    b_re = bre_ref[...]
    b_im = bim_ref[...]
    are_ref[...] = a_re
    aim_ref[...] = a_im
    bbre_ref[...] = z_re * b_re - z_im * b_im
    bbim_ref[...] = z_re * b_im + z_im * b_re


def s5_prep(lam_re, lam_im, log_dt, b_re, b_im):
    depth, g, p = lam_re.shape
    s = b_re.shape[-1]
    rows, cols = depth * g, p * s
    rep = lambda t: jnp.broadcast_to(t[..., None], (depth, g, p, s)).reshape(rows, cols)
    ldt = jnp.broadcast_to(log_dt[..., None, None], (depth, g, p, s)).reshape(rows, cols)
    sds = jax.ShapeDtypeStruct((rows, cols), F32)
    a_re, a_im, bb_re, bb_im = pl.pallas_call(
        _s5_prep_kernel,
        out_shape=(sds, sds, sds, sds),
        compiler_params=pltpu.CompilerParams(vmem_limit_bytes=VMEM_LIMIT),
        name="s5_prep",
    )(rep(lam_re), rep(lam_im), ldt, b_re.reshape(rows, cols), b_im.reshape(rows, cols))
    unrep = lambda t: t.reshape(depth, g, p, s)[..., 0]
    return unrep(a_re), unrep(a_im), bb_re.reshape(depth, g, p, s), bb_im.reshape(depth, g, p, s)


def s5_layout(a_re, a_im, bb_re, bb_im, c_re, c_im):
    depth = a_re.shape[0]
    gpo = SSM_GROUPS // S5_OCTETS
    eye = jnp.eye(gpo, dtype=F32)

    def in_blocks(bb):
        t = bb.reshape(depth, S5_OCTETS, gpo, SSM_STATE, SSM_GROUP_SIZE)
        w = jnp.einsum('logpc,gh->logchp', t, eye)
        return w.reshape(depth, S5_OCTETS, S5_OCT_CH, SUBLANES, LANES)

    w_in = jnp.stack([in_blocks(bb_re), in_blocks(bb_im)], axis=4)
    w_in = w_in.reshape(depth, S5_OCTETS, S5_OCT_CH, SUBLANES * 2 * LANES).astype(BF16)

    def out_blocks(c):
        t = c.reshape(depth, S5_OCTETS, gpo, SSM_GROUP_SIZE, SSM_STATE)
        w = jnp.einsum('logcp,gh->logphc', t, eye)
        return w.reshape(depth, S5_OCTETS, SUBLANES, LANES, S5_OCT_CH)

    w_out = jnp.stack([out_blocks(c_re), out_blocks(-c_im)], axis=3)
    w_out = w_out.reshape(depth, S5_OCTETS, SUBLANES * 2 * LANES, S5_OCT_CH).astype(BF16)
    tab = lambda a: a.reshape(depth, S5_PAIRS, LANES)
    return w_in, w_out, tab(a_re), tab(a_im)


def _s5_kernel(u_ref, win_ref, wout_ref, are_ref, aim_ref, d_ref, wglu_ref, bglu_ref, o_ref,
               bur_ref, bui_ref, sre_ref, sim_ref, y_ref, *, tc):
    @pl.when(pl.program_id(1) == 0)
    def _():
        sre_ref[...] = jnp.zeros_like(sre_ref)
        sim_ref[...] = jnp.zeros_like(sim_ref)

    half = tc // 2
    pair_w = 2 * LANES
    groups = S5_PAIRS // 2
    per = half // groups

    def in_map(j, h):
        o, jl = divmod(j, SUBLANES)
        uo = u_ref[h * half:(h + 1) * half, o * S5_OCT_CH:(o + 1) * S5_OCT_CH]
        res = jnp.dot(uo, win_ref[o, :, jl * pair_w:(jl + 1) * pair_w],
                      preferred_element_type=F32)
        base = h * half * S5_PITCH + j
        bur_ref[pl.ds(base, half, stride=S5_PITCH), :] = res[:, :LANES]
        bui_ref[pl.ds(base, half, stride=S5_PITCH), :] = res[:, LANES:]

    def out_map(j, h):
        o, jl = divmod(j, SUBLANES)
        base = h * half * S5_PITCH + j
        xr_j = bur_ref[pl.ds(base, half, stride=S5_PITCH), :]
        xi_j = bui_ref[pl.ds(base, half, stride=S5_PITCH), :]
        lhs = jnp.concatenate([xr_j, xi_j], axis=1).astype(BF16)
        return jnp.dot(lhs, wout_ref[o, jl * pair_w:(jl + 1) * pair_w, :], preferred_element_type=F32)

    a_re = [are_ref[q * SUBLANES:(q + 1) * SUBLANES, :] for q in range(S5_OCTETS)]
    a_im = [aim_ref[q * SUBLANES:(q + 1) * SUBLANES, :] for q in range(S5_OCTETS)]

    def scan(h, t0, carry):
        xr, xi = carry
        for t in range(t0, t0 + per):
            nr, ni = [], []
            for q in range(S5_OCTETS):
                row = (h * half + t) * S5_PITCH + q * SUBLANES
                br = bur_ref[row:row + SUBLANES, :]
                bi = bui_ref[row:row + SUBLANES, :]
                r = a_re[q] * xr[q] - a_im[q] * xi[q] + br
                i = a_re[q] * xi[q] + a_im[q] * xr[q] + bi
                bur_ref[row:row + SUBLANES, :] = r
                bui_ref[row:row + SUBLANES, :] = i
                nr.append(r)
                ni.append(i)
            xr, xi = nr, ni
        return xr, xi

    for j in range(S5_PAIRS):
        in_map(j, 0)
    carry = ([sre_ref[q * SUBLANES:(q + 1) * SUBLANES, :] for q in range(S5_OCTETS)],
             [sim_ref[q * SUBLANES:(q + 1) * SUBLANES, :] for q in range(S5_OCTETS)])
    for g in range(groups):
        carry = scan(0, g * per, carry)
        in_map(2 * g, 1)
        in_map(2 * g + 1, 1)
    acc = None
    for g in range(groups):
        carry = scan(1, g * per, carry)
        for j in (2 * g, 2 * g + 1):
            part = out_map(j, 0)
            acc = part if j % SUBLANES == 0 else acc + part
            if j % SUBLANES == SUBLANES - 1:
                o = j // SUBLANES
                y_ref[0:half, o * S5_OCT_CH:(o + 1) * S5_OCT_CH] = acc
    for q in range(S5_OCTETS):
        sre_ref[q * SUBLANES:(q + 1) * SUBLANES, :] = carry[0][q]
        sim_ref[q * SUBLANES:(q + 1) * SUBLANES, :] = carry[1][q]
    for j in range(S5_PAIRS):
        part = out_map(j, 1)
        acc = part if j % SUBLANES == 0 else acc + part
        if j % SUBLANES == SUBLANES - 1:
            o = j // SUBLANES
            y_ref[half:tc, o * S5_OCT_CH:(o + 1) * S5_OCT_CH] = acc

    y = y_ref[...] + d_ref[...] * u_ref[...].astype(F32)
    z = _gelu_tanh(y)
    gate = jnp.dot(z.astype(BF16), wglu_ref[...].astype(BF16), preferred_element_type=F32) + bglu_ref[...]
    o_ref[...] = (z * _sigmoid(gate)).astype(o_ref.dtype)


def s5_mixer(proj, w_in_blk, w_out_blk, a_re, a_im, d_skip, w_glu, b_glu, layer, *, batch, seq, tc):
    nt = seq // tc
    w = SSM_WIDTH
    kern = functools.partial(_s5_kernel, tc=tc)
    return pl.pallas_call(
        kern,
        out_shape=jax.ShapeDtypeStruct((batch * seq, w), BF16),
        grid=(batch, nt),
        in_specs=[
            pl.BlockSpec((tc, w), lambda b, c: (b * nt + c, COL_SSM // w)),
            _const_spec((None,) + w_in_blk.shape[1:], lambda b, c: (layer, 0, 0, 0)),
            _const_spec((None,) + w_out_blk.shape[1:], lambda b, c: (layer, 0, 0, 0)),
            _const_spec((None, S5_PAIRS, LANES), lambda b, c: (layer, 0, 0)),
            _const_spec((None, S5_PAIRS, LANES), lambda b, c: (layer, 0, 0)),
            _const_spec((None, 1, w), lambda b, c: (layer, 0, 0)),
            _const_spec((None, w, w), lambda b, c: (layer, 0, 0)),
            _const_spec((None, 1, w), lambda b, c: (layer, 0, 0)),
        ],
        out_specs=pl.BlockSpec((tc, w), lambda b, c: (b * nt + c, 0)),
        scratch_shapes=[
            pltpu.VMEM((tc * S5_PITCH, LANES), F32),
            pltpu.VMEM((tc * S5_PITCH, LANES), F32),
            pltpu.VMEM((S5_PAIRS, LANES), F32),
            pltpu.VMEM((S5_PAIRS, LANES), F32),
            pltpu.VMEM((tc, w), F32),
        ],
        compiler_params=_cparams(("parallel", "arbitrary")),
        name="s5_mixer",
    )(proj, w_in_blk, w_out_blk, a_re, a_im, d_skip, w_glu, b_glu)


def _split3(x):
    hi = x.astype(BF16)
    r1 = x - hi.astype(F32)
    mid = r1.astype(BF16)
    lo = (r1 - mid.astype(F32)).astype(BF16)
    return hi, mid, lo


def _gla_kernel(q_ref, k_ref, v_ref, r_ref, xn_ref, wglr_ref, wg_ref, bg_ref, ng_ref, tri_ref, o_ref, st_ref,
                *, tg):
    @pl.when(pl.program_id(1) == 0)
    def _():
        st_ref[...] = jnp.zeros_like(st_ref)

    c = GLA_CHUNK
    wl = wglr_ref[...].astype(BF16)
    wl = jnp.concatenate([wl, jnp.zeros((LANES - GLA_GATE_RANK, wl.shape[1]), BF16)], axis=0)
    glr = lax.dot_general(xn_ref[...], wl, _NT_DIMS, preferred_element_type=F32).astype(BF16)
    glog = jnp.dot(glr, wg_ref[...], preferred_element_type=F32) + bg_ref[...]
    g = (jnp.minimum(glog, 0.0) - jnp.log1p(jnp.exp(-jnp.abs(glog)))) / GLA_GATE_NORM
    tri = tri_ref[...]
    bcum = None
    for piece in _split3(g):
        part = jnp.dot(tri, piece, preferred_element_type=F32)
        bcum = part if bcum is None else bcum + part

    row = lax.broadcasted_iota(jnp.int32, (c, c), 0)
    col = lax.broadcasted_iota(jnp.int32, (c, c), 1)
    causal = col <= row
    scale = np.float32(GLA_DK ** -0.5)
    ng = ng_ref[...]
    for n in range(tg // c):
        rs = slice(n * c, (n + 1) * c)
        for h in range(GLA_HEADS):
            ks = slice(h * GLA_DK, (h + 1) * GLA_DK)
            vs = slice(h * GLA_DV, (h + 1) * GLA_DV)
            b = bcum[rs, ks]
            bl = b[c - 1:c, :]
            qh = q_ref[rs, ks].astype(F32) * scale
            kh = k_ref[rs, ks].astype(F32)
            vh = v_ref[rs, vs]
            q_dec = (qh * jnp.exp(b)).astype(BF16)
            k_inv = (kh * jnp.exp(-b)).astype(BF16)
            k_dec = (kh * jnp.exp(bl - b)).astype(BF16)
            attn = lax.dot_general(q_dec, k_inv, (((1,), (1,)), ((), ())), preferred_element_type=F32)
            attn = jnp.where(causal, attn, 0.0).astype(BF16)
            st = st_ref[h]
            o = (jnp.dot(attn, vh, preferred_element_type=F32)
                 + lax.dot_general(q_dec, st.astype(BF16), (((1,), (1,)), ((), ())),
                                   preferred_element_type=F32))
            kvt = lax.dot_general(vh, k_dec, (((0,), (0,)), ((), ())), preferred_element_type=F32)
            st_ref[h] = jnp.exp(bl) * st + kvt
            o = o * _rms_scale(o) * ng
            o_ref[rs, vs] = (o * _silu(r_ref[rs, vs].astype(F32))).astype(o_ref.dtype)


def gla_mixer(proj, xn, w_in_t, w_gate2_pad, b_gate2, norm_g, tri, layer, *, batch, seq, tg):
    nt = seq // tg
    kw, vw = GLA_KEY_WIDTH, GLA_VALUE_WIDTH
    d = xn.shape[1]
    sh = GLA_GATE_RANK
    kern = functools.partial(_gla_kernel, tg=tg)
    return pl.pallas_call(
        kern,
        out_shape=jax.ShapeDtypeStruct((batch * seq, vw), BF16),
        grid=(batch, nt),
        in_specs=[
            pl.BlockSpec((tg, kw), lambda b, c: (b * nt + c, COL_GLA_Q // kw)),
            pl.BlockSpec((tg, kw), lambda b, c: (b * nt + c, COL_GLA_K // kw)),
            pl.BlockSpec((tg, vw), lambda b, c: (b * nt + c, COL_GLA_V // vw)),
            pl.BlockSpec((tg, vw), lambda b, c: (b * nt + c, COL_GLA_R // vw)),
            pl.BlockSpec((tg, d), lambda b, c: (b * nt + c, 0)),
            _const_spec((None, sh, d), lambda b, c: (layer, ORIG_GLR // sh, 0)),
            _const_spec((None, LANES, kw), lambda b, c: (layer, 0, 0)),
            _const_spec((None, 1, kw), lambda b, c: (layer, 0, 0)),
            _const_spec((None, 1, GLA_DV), lambda b, c: (layer, 0, 0)),
            _const_spec((tg, tg), lambda b, c: (0, 0)),
        ],
        out_specs=pl.BlockSpec((tg, vw), lambda b, c: (b * nt + c, 0)),
        scratch_shapes=[pltpu.VMEM((GLA_HEADS, GLA_DV, GLA_DK), F32)],
        compiler_params=_cparams(("parallel", "arbitrary")),
        name="gla_mixer",
    )(proj, proj, proj, proj, xn, w_in_t, w_gate2_pad, b_gate2, norm_g, tri)


def _moba_kernel(q_ref, k_ref, v_ref, qg_ref, kg_ref, o_ref, kn_ref, vt_ref, km_ref, st_ref, pt_ref, *, nb):
    bs = MOBA_BLOCK
    kg = kg_ref[...]
    for jb in range(nb):
        kb = k_ref[jb * bs:(jb + 1) * bs, :].astype(F32)
        kn = kb * _rms_scale(kb) * kg
        kn_ref[jb * bs:(jb + 1) * bs, :] = kn.astype(BF16)
        km_ref[jb:jb + 1, :] = jnp.mean(kn, axis=0, keepdims=True)
    hd = MOBA_HEAD_DIM
    vt_ref[0:hd, :] = v_ref[...].astype(F32).T.astype(BF16)
    ones_row = lax.broadcasted_iota(jnp.int32, (vt_ref.shape[0] - hd, vt_ref.shape[1]), 0) == 0
    vt_ref[hd:, :] = jnp.where(ones_row, 1.0, 0.0).astype(BF16)

    qg = qg_ref[...]
    km = km_ref[...]
    blk = lax.broadcasted_iota(jnp.int32, (nb, bs), 0)
    krow = lax.broadcasted_iota(jnp.int32, (bs, bs), 0)
    qcol = lax.broadcasted_iota(jnp.int32, (bs, bs), 1)
    for i in range(nb):
        q = q_ref[i * bs:(i + 1) * bs, :].astype(F32)
        qn = (q * _rms_scale(q) * qg) * np.float32(MOBA_HEAD_DIM ** -0.5)
        qb = (qn * LOG2_E).astype(BF16)
        bias_t = None
        if i > MOBA_TOPK:
            sbt = lax.dot_general(km, qn, (((1,), (1,)), ((), ())),
                                  precision=lax.Precision.HIGHEST, preferred_element_type=F32)
            cnt = jnp.zeros((nb, bs), F32)
            for jp in range(i):
                sp = sbt[jp:jp + 1, :]
                beats = jnp.where(sp > sbt, 1.0, jnp.where(sp == sbt, jnp.where(blk > jp, 1.0, 0.0), 0.0))
                cnt = cnt + beats
            bias_t = jnp.where(cnt < MOBA_TOPK, 0.0, NEG_BIG)
        nk = (i + 1) * bs
        s_buf = st_ref.at[i % 2]
        p_buf = pt_ref.at[i % 2]
        mx8 = None
        for j in range(i + 1):
            ks = slice(j * bs, (j + 1) * bs)
            sj = lax.dot_general(kn_ref[ks, :], qb, _NT_DIMS, preferred_element_type=F32)
            if j == i:
                sj = jnp.where(krow <= qcol, sj, NEG_BIG)
            elif bias_t is not None:
                sj = sj + bias_t[j:j + 1, :]
            s_buf[ks, :] = sj
            pm = jnp.max(sj.reshape(bs // SUBLANES, SUBLANES, bs), axis=0)
            mx8 = pm if mx8 is None else jnp.maximum(mx8, pm)
        m = jnp.max(mx8, axis=0, keepdims=True)
        for j in range(i + 1):
            ks = slice(j * bs, (j + 1) * bs)
            p_buf[ks, :] = jnp.exp2(s_buf[ks, :] - m).astype(BF16)
        ot = jnp.dot(vt_ref[:, 0:nk], p_buf[0:nk, :], preferred_element_type=F32)
        o_ref[i * bs:(i + 1) * bs, :] = (ot[0:hd, :] / ot[hd:hd + 1, :]).T.astype(o_ref.dtype)


def moba_mixer(proj, q_g, k_g, layer, *, batch, seq):
    bs, hd, nh = MOBA_BLOCK, MOBA_HEAD_DIM, MOBA_HEADS
    nb = seq // bs
    kern = functools.partial(_moba_kernel, nb=nb)
    blk = lambda col: pl.BlockSpec((seq, hd), lambda b, h: (b, col // hd + h))
    return pl.pallas_call(
        kern,
        out_shape=jax.ShapeDtypeStruct((batch * seq, MOBA_WIDTH), BF16),
        grid=(batch, nh),
        in_specs=[
            blk(COL_MOBA_Q), blk(COL_MOBA_K), blk(COL_MOBA_V),
            _const_spec((None, 1, hd), lambda b, h: (layer, 0, 0)),
            _const_spec((None, 1, hd), lambda b, h: (layer, 0, 0)),
        ],
        out_specs=pl.BlockSpec((seq, hd), lambda b, h: (b, h)),
        scratch_shapes=[pltpu.VMEM((seq, hd), BF16), pltpu.VMEM((hd + BF16_ROWS, seq), BF16),
                        pltpu.VMEM((nb, hd), F32),
                        pltpu.VMEM((2, seq, bs), F32), pltpu.VMEM((2, seq, bs), BF16)],
        compiler_params=_cparams(("parallel", "parallel")),
        name="moba_mixer",
    )(proj, proj, proj, q_g, k_g)


_WEIGHT_OUTER = ("arbitrary", "arbitrary")


def _merge_kernel(ys_ref, yg_ref, ym_ref, gs_ref, gg_ref, gm_ref, ws_ref, wg_ref, wm_ref, o_ref,
                  wsb_ref, wgb_ref, wmb_ref):
    @pl.when(pl.program_id(1) == 0)
    def _():
        wsb_ref[...] = ws_ref[...].astype(BF16)
        wgb_ref[...] = wg_ref[...].astype(BF16)
        wmb_ref[...] = wm_ref[...].astype(BF16)

    def branch(g_ref, y_ref, w_ref):
        return _sigmoid(g_ref[...].astype(F32)) * jnp.dot(y_ref[...], w_ref[...], preferred_element_type=F32)

    acc = branch(gs_ref, ys_ref, wsb_ref) + branch(gg_ref, yg_ref, wgb_ref) + branch(gm_ref, ym_ref, wmb_ref)
    o_ref[...] = acc.astype(o_ref.dtype)


def merge_branches(y_ssm, y_gla, y_moba, proj, w_s, w_g, w_m, layer, *, tm, tn):
    m, kdim = y_ssm.shape
    n = w_s.shape[-1]
    yspec = pl.BlockSpec((tm, kdim), lambda j, i: (i, 0))
    gspec = lambda col: pl.BlockSpec((tm, tn), lambda j, i: (i, col // tn + j))
    wspec = pl.BlockSpec((None, kdim, tn), lambda j, i: (layer, 0, j))
    return pl.pallas_call(
        _merge_kernel,
        out_shape=jax.ShapeDtypeStruct((m, n), BF16),
        grid=(n // tn, m // tm),
        in_specs=[yspec, yspec, yspec, gspec(COL_GATE_SSM), gspec(COL_GATE_GLA), gspec(COL_GATE_MOBA),
                  wspec, wspec, wspec],
        out_specs=pl.BlockSpec((tm, tn), lambda j, i: (i, j)),
        scratch_shapes=[pltpu.VMEM((kdim, tn), BF16)] * 3,
        compiler_params=_cparams(_WEIGHT_OUTER),
        name="merge",
    )(y_ssm, y_gla, y_moba, proj, proj, proj, w_s, w_g, w_m)


def _out_proj_kernel(a_ref, w_ref, x_ref, g_ref, o_ref, h_ref, wbf_ref):
    @pl.when(pl.program_id(0) == 0)
    def _():
        wbf_ref[...] = w_ref[...].astype(BF16)

    xn = x_ref[...] + jnp.dot(a_ref[...], wbf_ref[...], preferred_element_type=F32)
    o_ref[...] = xn
    h_ref[...] = (xn * _rms_scale(xn) * g_ref[...]).astype(h_ref.dtype)


def out_proj_norm(a, w_all, x, g, layer, *, tm):
    m, kdim = a.shape
    n = w_all.shape[-1]
    row = lambda width: pl.BlockSpec((tm, width), lambda i: (i, 0))
    return pl.pallas_call(
        _out_proj_kernel,
        out_shape=(jax.ShapeDtypeStruct((m, n), F32), jax.ShapeDtypeStruct((m, n), BF16)),
        grid=(m // tm,),
        in_specs=[row(kdim), _const_spec((None, kdim, n), lambda i: (layer, 0, 0)), row(n),
                  _const_spec((None, 1, n), lambda i: (layer, 0, 0))],
        out_specs=(row(n), row(n)),
        scratch_shapes=[pltpu.VMEM((kdim, n), BF16)],
        compiler_params=_cparams(("arbitrary",)),
        name="out_proj",
    )(a, w_all, x, g)


def _resid_matmul_kernel(a_ref, w_ref, x_ref, o_ref, wbf_ref):
    @pl.when(pl.program_id(1) == 0)
    def _():
        wbf_ref[...] = w_ref[...].astype(BF16)

    o_ref[...] = x_ref[...] + jnp.dot(a_ref[...], wbf_ref[...], preferred_element_type=F32)


def resid_matmul(a, w_all, x, layer, *, tm, tn):
    m, kdim = a.shape
    n = w_all.shape[-1]
    return pl.pallas_call(
        _resid_matmul_kernel,
        out_shape=jax.ShapeDtypeStruct((m, n), F32),
        grid=(n // tn, m // tm),
        in_specs=[
            pl.BlockSpec((tm, kdim), lambda j, i: (i, 0)),
            _const_spec((None, kdim, tn), lambda j, i: (layer, 0, j)),
            pl.BlockSpec((tm, tn), lambda j, i: (i, j)),
        ],
        out_specs=pl.BlockSpec((tm, tn), lambda j, i: (i, j)),
        scratch_shapes=[pltpu.VMEM((kdim, tn), BF16)],
        compiler_params=_cparams(_WEIGHT_OUTER),
        name="resid_matmul",
    )(a, w_all, x)


def _ffn_up_kernel(h_ref, wg_ref, wu_ref, o_ref, wgb_ref, wub_ref):
    @pl.when(pl.program_id(1) == 0)
    def _():
        wgb_ref[...] = wg_ref[...].astype(BF16)
        wub_ref[...] = wu_ref[...].astype(BF16)

    h = h_ref[...]
    gate = jnp.dot(h, wgb_ref[...], preferred_element_type=F32)
    up = jnp.dot(h, wub_ref[...], preferred_element_type=F32)
    o_ref[...] = (_silu(gate) * up).astype(o_ref.dtype)


def ffn_up(h, w_gate, w_up, layer, *, tm, tn):
    m, d = h.shape
    n = w_gate.shape[-1]
    wspec = pl.BlockSpec((None, d, tn), lambda j, i: (layer, 0, j))
    return pl.pallas_call(
        _ffn_up_kernel,
        out_shape=jax.ShapeDtypeStruct((m, n), BF16),
        grid=(n // tn, m // tm),
        in_specs=[pl.BlockSpec((tm, d), lambda j, i: (i, 0)), wspec, wspec],
        out_specs=pl.BlockSpec((tm, tn), lambda j, i: (i, j)),
        scratch_shapes=[pltpu.VMEM((d, tn), BF16)] * 2,
        compiler_params=_cparams(_WEIGHT_OUTER),
        name="ffn_up",
    )(h, w_gate, w_up)


def _gla_tri(tg):
    r = np.arange(tg)
    same_chunk = (r[:, None] // GLA_CHUNK) == (r[None, :] // GLA_CHUNK)
    return jnp.asarray(np.where(same_chunk & (r[None, :] <= r[:, None]), 1.0, 0.0), dtype=BF16)


def kernel(x, norm1_g, w_in, ssm_lambda_re, ssm_lambda_im, ssm_log_dt, ssm_b_re, ssm_b_im, ssm_c_re, ssm_c_im, ssm_d, ssm_w_glu, ssm_b_glu, gla_w_gate2, gla_b_gate2, gla_norm_g, moba_q_norm_g, moba_k_norm_g, w_branch_ssm, w_branch_gla, w_branch_moba, w_out, norm2_g, ffn_w_gate, ffn_w_up, ffn_w_down):
    batch, seq, d = x.shape
    depth = w_in.shape[0]
    m = batch * seq

    w_in_t = jnp.swapaxes(w_in, 1, 2)
    wg2_pad = jnp.pad(gla_w_gate2, ((0, 0), (0, LANES - GLA_GATE_RANK), (0, 0))).astype(BF16)
    a_re, a_im, bb_re, bb_im = s5_prep(ssm_lambda_re, ssm_lambda_im, ssm_log_dt, ssm_b_re, ssm_b_im)
    s5_win, s5_wout, a_re_t, a_im_t = s5_layout(a_re, a_im, bb_re, bb_im, ssm_c_re, ssm_c_im)
    row3 = lambda t: t[:, None, :]
    n1, n2 = row3(norm1_g), row3(norm2_g)
    ssm_d3, b_glu3 = row3(ssm_d), row3(ssm_b_glu)
    bg2, gng = row3(gla_b_gate2), row3(gla_norm_g)
    mqg, mkg = row3(moba_q_norm_g), row3(moba_k_norm_g)

    tg = 256
    tri = _gla_tri(tg)
    xf = x.reshape(m, d)
    for l in range(depth):
        xn = rmsnorm(xf, n1, l, tm=512)
        proj = in_proj(xn, w_in_t, l, tm=2048, tn=1024)
        y_ssm = s5_mixer(proj, s5_win, s5_wout, a_re_t, a_im_t, ssm_d3, ssm_w_glu, b_glu3, l,
                         batch=batch, seq=seq, tc=512)
        y_gla = gla_mixer(proj, xn, w_in_t, wg2_pad, bg2, gng, tri, l, batch=batch, seq=seq, tg=tg)
        y_moba = moba_mixer(proj, mqg, mkg, l, batch=batch, seq=seq)
        merged = merge_branches(y_ssm, y_gla, y_moba, proj, w_branch_ssm, w_branch_gla, w_branch_moba, l,
                                tm=512, tn=1024)
        xf, hn = out_proj_norm(merged, w_out, xf, n2, l, tm=512)
        act = ffn_up(hn, ffn_w_gate, ffn_w_up, l, tm=1024, tn=512)
        xf = resid_matmul(act, ffn_w_down, xf, l, tm=512, tn=1024)
    return xf.reshape(batch, seq, d)
```

```python
import functools

import jax
import jax.numpy as jnp
import numpy as np
from jax import lax
from jax.experimental import pallas as pl
from jax.experimental.pallas import tpu as pltpu

F32 = jnp.float32
BF16 = jnp.bfloat16

D_MODEL = 2048
DEPTH = 4
RMS_EPS = 1e-6
SSM_WIDTH = D_MODEL // 2
SSM_GROUP_SIZE = 16
SSM_GROUPS = SSM_WIDTH // SSM_GROUP_SIZE
SSM_STATE = 64
GLA_HEADS = 4
GLA_KEY_WIDTH = D_MODEL // 4
GLA_VALUE_WIDTH = D_MODEL // 2
GLA_DK = GLA_KEY_WIDTH // GLA_HEADS
GLA_DV = GLA_VALUE_WIDTH // GLA_HEADS
GLA_GATE_RANK = 16
GLA_GATE_NORM = 16.0
GLA_CHUNK = 64
MOBA_WIDTH = D_MODEL // 2
MOBA_HEAD_DIM = 128
MOBA_HEADS = MOBA_WIDTH // MOBA_HEAD_DIM
MOBA_BLOCK = 256
MOBA_TOPK = 3
FFN_HIDDEN = -(-8 * D_MODEL // (3 * 256)) * 256

LANES = 128
SUBLANES = 8
MXU_DIM = 256
VMEM_LIMIT = 56 * 1024 * 1024

COL_SSM = 0
COL_GLA_Q = COL_SSM + SSM_WIDTH
COL_GLA_K = COL_GLA_Q + GLA_KEY_WIDTH
COL_GLA_V = COL_GLA_K + GLA_KEY_WIDTH
COL_GLA_R = COL_GLA_V + GLA_VALUE_WIDTH
COL_MOBA_Q = COL_GLA_R + GLA_VALUE_WIDTH
COL_MOBA_K = COL_MOBA_Q + MOBA_WIDTH
COL_MOBA_V = COL_MOBA_K + MOBA_WIDTH
COL_GATE_SSM = COL_MOBA_V + MOBA_WIDTH
COL_GATE_GLA = COL_GATE_SSM + D_MODEL
COL_GATE_MOBA = COL_GATE_GLA + D_MODEL
PROJ_WIDTH = COL_GATE_MOBA + D_MODEL
ORIG_GLR = COL_MOBA_Q

S5_PAIRS = SSM_GROUPS // 2
S5_OCTETS = S5_PAIRS // SUBLANES
S5_OCT_CH = SSM_WIDTH // S5_OCTETS
S5_PITCH = S5_PAIRS + 4

NEG_BIG = -1e30
LOG2_E = float(np.log2(np.e))
BF16_ROWS = 2 * SUBLANES


def _cparams(sem):
    return pltpu.CompilerParams(dimension_semantics=sem, vmem_limit_bytes=VMEM_LIMIT)


def _const_spec(shape, index_map):
    return pl.BlockSpec(shape, index_map, pipeline_mode=pl.Buffered(1))


def _sigmoid(x):
    return 1.0 / (1.0 + jnp.exp(-x))


def _silu(x):
    return x * _sigmoid(x)


def _gelu_tanh(x):
    c = np.float32(np.sqrt(2.0 / np.pi))
    return 0.5 * x * (1.0 + jnp.tanh(c * (x + np.float32(0.044715) * (x * x * x))))


def _rms_scale(x):
    return lax.rsqrt(jnp.mean(x * x, axis=-1, keepdims=True) + RMS_EPS)


def _rmsnorm_kernel(x_ref, g_ref, o_ref):
    x = x_ref[...]
    o_ref[...] = (x * _rms_scale(x) * g_ref[...]).astype(o_ref.dtype)


def rmsnorm(x, g, layer, *, tm):
    m, d = x.shape
    return pl.pallas_call(
        _rmsnorm_kernel,
        out_shape=jax.ShapeDtypeStruct((m, d), BF16),
        grid=(m // tm,),
        in_specs=[pl.BlockSpec((tm, d), lambda i: (i, 0)),
                  _const_spec((None, 1, d), lambda i: (layer, 0, 0))],
        out_specs=pl.BlockSpec((tm, d), lambda i: (i, 0)),
        compiler_params=_cparams(("parallel",)),
        name="rmsnorm",
    )(x, g)


_NT_DIMS = (((1,), (1,)), ((), ()))


def _in_proj_kernel(xn_ref, wa_ref, wb_ref, o_ref, wbf_ref, *, n_aligned):
    j = pl.program_id(0)
    first = pl.program_id(1) == 0
    tn = wa_ref.shape[0]
    sh = GLA_GATE_RANK

    @pl.when(jnp.logical_and(first, j < n_aligned))
    def _():
        wbf_ref[...] = wa_ref[...].astype(BF16)

    @pl.when(jnp.logical_and(first, j >= n_aligned))
    def _():
        wbf_ref[0:tn - sh, :] = wa_ref[sh:tn, :].astype(BF16)
        wbf_ref[tn - sh:tn, :] = wb_ref[...].astype(BF16)

    o_ref[...] = lax.dot_general(xn_ref[...], wbf_ref[...], _NT_DIMS,
                                 preferred_element_type=F32).astype(o_ref.dtype)


def in_proj(xn, w_in_t, layer, *, tm, tn):
    m, d = xn.shape
    sh = GLA_GATE_RANK
    assert ORIG_GLR % tn == 0 and PROJ_WIDTH % tn == 0 and tn % sh == 0
    kern = functools.partial(_in_proj_kernel, n_aligned=ORIG_GLR // tn)
    return pl.pallas_call(
        kern,
        out_shape=jax.ShapeDtypeStruct((m, PROJ_WIDTH), BF16),
        grid=(PROJ_WIDTH // tn, m // tm),
        in_specs=[
            pl.BlockSpec((tm, d), lambda j, i: (i, 0)),
            pl.BlockSpec((None, tn, d), lambda j, i: (layer, j, 0)),
            pl.BlockSpec((None, sh, d), lambda j, i: (layer, (j + 1) * (tn // sh), 0)),
        ],
        out_specs=pl.BlockSpec((tm, tn), lambda j, i: (i, j)),
        scratch_shapes=[pltpu.VMEM((tn, d), BF16)],
        compiler_params=_cparams(("arbitrary", "arbitrary")),
        name="in_proj",
    )(xn, w_in_t, w_in_t)


def _s5_weights_kernel(lre_ref, lim_ref, ldt_ref, bre_ref, bim_ref, cre_ref, cim_ref, rep_ref, rept_ref,
                       win_ref, wout_ref, are_ref, aim_ref):
    lre = lre_ref[...]
    lim = lim_ref[...]
    dt = jnp.exp(ldt_ref[...])
    mag = jnp.exp(lre * dt)
    ang = lim * dt
    a_re = mag * jnp.cos(ang)
    a_im = mag * jnp.sin(ang)
    den = lre * lre + lim * lim
    n_re = a_re - 1.0
    n_im = a_im
    z_re = (n_re * lre + n_im * lim) / den
    z_im = (n_im * lre - n_re * lim) / den
    b_re = bre_ref[...]
    b_im = bim_ref[...]
    are_ref[...] = a_re
    aim_ref[...] = a_im
    bb_re = (z_re * b_re - z_im * b_im).astype(BF16)
    bb_im = (z_re * b_im + z_im * b_re).astype(BF16)

    def own_and_re(s, ch):
        pair = jnp.right_shift(s, 8)
        half = jnp.bitwise_and(jnp.right_shift(s, 6), 1)
        own = jnp.right_shift(ch, 4) == 2 * pair + half
        return own, jnp.bitwise_and(jnp.right_shift(s, 7), 1) == 0

    shape = win_ref.shape
    own, is_re = own_and_re(lax.broadcasted_iota(jnp.int32, shape, 1), lax.broadcasted_iota(jnp.int32, shape, 0))
    t_re = jnp.dot(bb_re, rep_ref[...], preferred_element_type=F32)
    t_im = jnp.dot(bb_im, rep_ref[...], preferred_element_type=F32)
    win_ref[...] = jnp.where(own, jnp.where(is_re, t_re, t_im), 0.0).astype(win_ref.dtype)

    shape = wout_ref.shape
    own, is_re = own_and_re(lax.broadcasted_iota(jnp.int32, shape, 0), lax.broadcasted_iota(jnp.int32, shape, 1))
    t_re = lax.dot_general(rept_ref[...], cre_ref[...].astype(BF16), _NT_DIMS, preferred_element_type=F32)
    t_im = lax.dot_general(rept_ref[...], cim_ref[...].astype(BF16), _NT_DIMS, preferred_element_type=F32)
    wout_ref[...] = jnp.where(own, jnp.where(is_re, t_re, -t_im), 0.0).astype(wout_ref.dtype)


def s5_weights(lam_re, lam_im, log_dt, b_re, b_im, c_re, c_im):
    depth, g, p = lam_re.shape
    s = b_re.shape[-1]
    gpo = g // S5_OCTETS
    blk_rows = gpo * s
    rows = depth * g * s
    rep_rows = lambda t: jnp.broadcast_to(t[:, :, None, :], (depth, g, s, p)).reshape(rows, p)
    ldt = jnp.broadcast_to(log_dt[:, :, None, None], (depth, g, s, p)).reshape(rows, p)
    b_gcp = lambda t: jnp.swapaxes(t, 2, 3).reshape(rows, p)
    span = SUBLANES * 2 * LANES
    rep = jnp.asarray((np.arange(span)[None, :] % p) == np.arange(p)[:, None], dtype=BF16)
    blk = pl.BlockSpec((blk_rows, p), lambda l, o: (l * S5_OCTETS + o, 0))
    w_in, w_out, a_re, a_im = pl.pallas_call(
        _s5_weights_kernel,
        out_shape=(jax.ShapeDtypeStruct((depth, S5_OCTETS, blk_rows, span), BF16),
                   jax.ShapeDtypeStruct((depth, S5_OCTETS, span, blk_rows), BF16),
                   jax.ShapeDtypeStruct((rows, p), F32), jax.ShapeDtypeStruct((rows, p), F32)),
        grid=(depth, S5_OCTETS),
        in_specs=[blk] * 7 + [_const_spec((p, span), lambda l, o: (0, 0)),
                              _const_spec((span, p), lambda l, o: (0, 0))],
        out_specs=(pl.BlockSpec((None, None, blk_rows, span), lambda l, o: (l, o, 0, 0)),
                   pl.BlockSpec((None, None, span, blk_rows), lambda l, o: (l, o, 0, 0)),
                   blk, blk),
        compiler_params=_cparams(("parallel", "parallel")),
        name="s5_weights",
    )(rep_rows(lam_re), rep_rows(lam_im), ldt, b_gcp(b_re), b_gcp(b_im),
      c_re.reshape(rows, p), c_im.reshape(rows, p), rep, rep.T)
    tab = lambda a: a.reshape(depth, g, s, p)[:, :, 0, :].reshape(depth, S5_PAIRS, LANES)
    return w_in, w_out, tab(a_re), tab(a_im)


def _s5_kernel(u_ref, win_ref, wout_ref, are_ref, aim_ref, d_ref, wglu_ref, bglu_ref, o_ref,
               bur_ref, bui_ref, sre_ref, sim_ref, y_ref, *, tc):
    @pl.when(pl.program_id(1) == 0)
    def _():
        sre_ref[...] = jnp.zeros_like(sre_ref)
        sim_ref[...] = jnp.zeros_like(sim_ref)

    half = tc // 2
    pair_w = 2 * LANES
    groups = S5_PAIRS // 2
    per = half // groups

    def in_map(j, h):
        o, jl = divmod(j, SUBLANES)
        uo = u_ref[h * half:(h + 1) * half, o * S5_OCT_CH:(o + 1) * S5_OCT_CH]
        res = jnp.dot(uo, win_ref[o, :, jl * pair_w:(jl + 1) * pair_w],
                      preferred_element_type=F32)
        base = h * half * S5_PITCH + j
        bur_ref[pl.ds(base, half, stride=S5_PITCH), :] = res[:, :LANES]
        bui_ref[pl.ds(base, half, stride=S5_PITCH), :] = res[:, LANES:]

    def out_map(j, h):
        o, jl = divmod(j, SUBLANES)
        base = h * half * S5_PITCH + j
        xr_j = bur_ref[pl.ds(base, half, stride=S5_PITCH), :]
        xi_j = bui_ref[pl.ds(base, half, stride=S5_PITCH), :]
        lhs = jnp.concatenate([xr_j, xi_j], axis=1).astype(BF16)
        return jnp.dot(lhs, wout_ref[o, jl * pair_w:(jl + 1) * pair_w, :], preferred_element_type=F32)

    a_re = [are_ref[q * SUBLANES:(q + 1) * SUBLANES, :] for q in range(S5_OCTETS)]
    a_im = [aim_ref[q * SUBLANES:(q + 1) * SUBLANES, :] for q in range(S5_OCTETS)]

    def scan(h, t0, carry):
        xr, xi = carry
        for t in range(t0, t0 + per):
            nr, ni = [], []
            for q in range(S5_OCTETS):
                row = (h * half + t) * S5_PITCH + q * SUBLANES
                br = bur_ref[row:row + SUBLANES, :]
                bi = bui_ref[row:row + SUBLANES, :]
                r = a_re[q] * xr[q] - a_im[q] * xi[q] + br
                i = a_re[q] * xi[q] + a_im[q] * xr[q] + bi
                bur_ref[row:row + SUBLANES, :] = r
                bui_ref[row:row + SUBLANES, :] = i
                nr.append(r)
                ni.append(i)
            xr, xi = nr, ni
        return xr, xi

    for j in range(S5_PAIRS):
        in_map(j, 0)
    carry = ([sre_ref[q * SUBLANES:(q + 1) * SUBLANES, :] for q in range(S5_OCTETS)],
             [sim_ref[q * SUBLANES:(q + 1) * SUBLANES, :] for q in range(S5_OCTETS)])
    for g in range(groups):
        carry = scan(0, g * per, carry)
        in_map(2 * g, 1)
        in_map(2 * g + 1, 1)
    acc = None
    for g in range(groups):
        carry = scan(1, g * per, carry)
        for j in (2 * g, 2 * g + 1):
            part = out_map(j, 0)
            acc = part if j % SUBLANES == 0 else acc + part
            if j % SUBLANES == SUBLANES - 1:
                o = j // SUBLANES
                y_ref[0:half, o * S5_OCT_CH:(o + 1) * S5_OCT_CH] = acc
    for q in range(S5_OCTETS):
        sre_ref[q * SUBLANES:(q + 1) * SUBLANES, :] = carry[0][q]
        sim_ref[q * SUBLANES:(q + 1) * SUBLANES, :] = carry[1][q]
    for j in range(S5_PAIRS):
        part = out_map(j, 1)
        acc = part if j % SUBLANES == 0 else acc + part
        if j % SUBLANES == SUBLANES - 1:
            o = j // SUBLANES
            y_ref[half:tc, o * S5_OCT_CH:(o + 1) * S5_OCT_CH] = acc

    y = y_ref[...] + d_ref[...] * u_ref[...].astype(F32)
    z = _gelu_tanh(y)
    gate = jnp.dot(z.astype(BF16), wglu_ref[...].astype(BF16), preferred_element_type=F32) + bglu_ref[...]
    o_ref[...] = (z * _sigmoid(gate)).astype(o_ref.dtype)


def s5_mixer(proj, w_in_blk, w_out_blk, a_re, a_im, d_skip, w_glu, b_glu, layer, *, batch, seq, tc):
    nt = seq // tc
    w = SSM_WIDTH
    kern = functools.partial(_s5_kernel, tc=tc)
    return pl.pallas_call(
        kern,
        out_shape=jax.ShapeDtypeStruct((batch * seq, w), BF16),
        grid=(batch, nt),
        in_specs=[
            pl.BlockSpec((tc, w), lambda b, c: (b * nt + c, COL_SSM // w)),
            _const_spec((None,) + w_in_blk.shape[1:], lambda b, c: (layer, 0, 0, 0)),
            _const_spec((None,) + w_out_blk.shape[1:], lambda b, c: (layer, 0, 0, 0)),
            _const_spec((None, S5_PAIRS, LANES), lambda b, c: (layer, 0, 0)),
            _const_spec((None, S5_PAIRS, LANES), lambda b, c: (layer, 0, 0)),
            _const_spec((None, 1, w), lambda b, c: (layer, 0, 0)),
            _const_spec((None, w, w), lambda b, c: (layer, 0, 0)),
            _const_spec((None, 1, w), lambda b, c: (layer, 0, 0)),
        ],
        out_specs=pl.BlockSpec((tc, w), lambda b, c: (b * nt + c, 0)),
        scratch_shapes=[
            pltpu.VMEM((tc * S5_PITCH, LANES), F32),
            pltpu.VMEM((tc * S5_PITCH, LANES), F32),
            pltpu.VMEM((S5_PAIRS, LANES), F32),
            pltpu.VMEM((S5_PAIRS, LANES), F32),
            pltpu.VMEM((tc, w), F32),
        ],
        compiler_params=_cparams(("parallel", "arbitrary")),
        name="s5_mixer",
    )(proj, w_in_blk, w_out_blk, a_re, a_im, d_skip, w_glu, b_glu)


def _split3(x):
    hi = x.astype(BF16)
    r1 = x - hi.astype(F32)
    mid = r1.astype(BF16)
    lo = (r1 - mid.astype(F32)).astype(BF16)
    return hi, mid, lo


def _gla_kernel(q_ref, k_ref, v_ref, r_ref, xn_ref, wglr_ref, wg_ref, bg_ref, ng_ref, tri_ref, o_ref, st_ref,
                *, tg):
    @pl.when(pl.program_id(1) == 0)
    def _():
        st_ref[...] = jnp.zeros_like(st_ref)

    c = GLA_CHUNK
    wl = wglr_ref[...].astype(BF16)
    wl = jnp.concatenate([wl, jnp.zeros((LANES - GLA_GATE_RANK, wl.shape[1]), BF16)], axis=0)
    glr = lax.dot_general(xn_ref[...], wl, _NT_DIMS, preferred_element_type=F32).astype(BF16)
    glog = jnp.dot(glr, wg_ref[...], preferred_element_type=F32) + bg_ref[...]
    g = (jnp.minimum(glog, 0.0) - jnp.log1p(jnp.exp(-jnp.abs(glog)))) / GLA_GATE_NORM
    tri = tri_ref[...]
    bcum = None
    for piece in _split3(g):
        part = jnp.dot(tri, piece, preferred_element_type=F32)
        bcum = part if bcum is None else bcum + part

    row = lax.broadcasted_iota(jnp.int32, (c, c), 0)
    col = lax.broadcasted_iota(jnp.int32, (c, c), 1)
    causal = col <= row
    scale = np.float32(GLA_DK ** -0.5)
    ng = ng_ref[...]
    for n in range(tg // c):
        rs = slice(n * c, (n + 1) * c)
        for h in range(GLA_HEADS):
            ks = slice(h * GLA_DK, (h + 1) * GLA_DK)
            vs = slice(h * GLA_DV, (h + 1) * GLA_DV)
            b = bcum[rs, ks]
            bl = b[c - 1:c, :]
            qh = q_ref[rs, ks].astype(F32) * scale
            kh = k_ref[rs, ks].astype(F32)
            vh = v_ref[rs, vs]
            q_dec = (qh * jnp.exp(b)).astype(BF16)
            k_inv = (kh * jnp.exp(-b)).astype(BF16)
            k_dec = (kh * jnp.exp(bl - b)).astype(BF16)
            attn = lax.dot_general(q_dec, k_inv, (((1,), (1,)), ((), ())), preferred_element_type=F32)
            attn = jnp.where(causal, attn, 0.0).astype(BF16)
            st = st_ref[h]
            o = (jnp.dot(attn, vh, preferred_element_type=F32)
                 + lax.dot_general(q_dec, st.astype(BF16), (((1,), (1,)), ((), ())),
                                   preferred_element_type=F32))
            kvt = lax.dot_general(vh, k_dec, (((0,), (0,)), ((), ())), preferred_element_type=F32)
            st_ref[h] = jnp.exp(bl) * st + kvt
            o = o * _rms_scale(o) * ng
            o_ref[rs, vs] = (o * _silu(r_ref[rs, vs].astype(F32))).astype(o_ref.dtype)


def gla_mixer(proj, xn, w_in_t, w_gate2_pad, b_gate2, norm_g, tri, layer, *, batch, seq, tg):
    nt = seq // tg
    kw, vw = GLA_KEY_WIDTH, GLA_VALUE_WIDTH
    d = xn.shape[1]
    sh = GLA_GATE_RANK
    kern = functools.partial(_gla_kernel, tg=tg)
    return pl.pallas_call(
        kern,
        out_shape=jax.ShapeDtypeStruct((batch * seq, vw), BF16),
        grid=(batch, nt),
        in_specs=[
            pl.BlockSpec((tg, kw), lambda b, c: (b * nt + c, COL_GLA_Q // kw)),
            pl.BlockSpec((tg, kw), lambda b, c: (b * nt + c, COL_GLA_K // kw)),
            pl.BlockSpec((tg, vw), lambda b, c: (b * nt + c, COL_GLA_V // vw)),
            pl.BlockSpec((tg, vw), lambda b, c: (b * nt + c, COL_GLA_R // vw)),
            pl.BlockSpec((tg, d), lambda b, c: (b * nt + c, 0)),
            _const_spec((None, sh, d), lambda b, c: (layer, ORIG_GLR // sh, 0)),
            _const_spec((None, LANES, kw), lambda b, c: (layer, 0, 0)),
            _const_spec((None, 1, kw), lambda b, c: (layer, 0, 0)),
            _const_spec((None, 1, GLA_DV), lambda b, c: (layer, 0, 0)),
            _const_spec((tg, tg), lambda b, c: (0, 0)),
        ],
        out_specs=pl.BlockSpec((tg, vw), lambda b, c: (b * nt + c, 0)),
        scratch_shapes=[pltpu.VMEM((GLA_HEADS, GLA_DV, GLA_DK), F32)],
        compiler_params=_cparams(("parallel", "arbitrary")),
        name="gla_mixer",
    )(proj, proj, proj, proj, xn, w_in_t, w_gate2_pad, b_gate2, norm_g, tri)


def _moba_kernel(q_ref, k_ref, v_ref, qg_ref, kg_ref, o_ref, kn_ref, vt_ref, km_ref, st_ref, pt_ref, *, nb):
    bs = MOBA_BLOCK
    kg = kg_ref[...]
    for jb in range(nb):
        kb = k_ref[jb * bs:(jb + 1) * bs, :].astype(F32)
        kn = kb * _rms_scale(kb) * kg
        kn_ref[jb * bs:(jb + 1) * bs, :] = kn.astype(BF16)
        km_ref[jb:jb + 1, :] = jnp.mean(kn, axis=0, keepdims=True)
    hd = MOBA_HEAD_DIM
    vt_ref[0:hd, :] = v_ref[...].astype(F32).T.astype(BF16)
    ones_row = lax.broadcasted_iota(jnp.int32, (vt_ref.shape[0] - hd, vt_ref.shape[1]), 0) == 0
    vt_ref[hd:, :] = jnp.where(ones_row, 1.0, 0.0).astype(BF16)

    qg = qg_ref[...]
    km = km_ref[...]
    blk = lax.broadcasted_iota(jnp.int32, (nb, bs), 0)
    krow = lax.broadcasted_iota(jnp.int32, (bs, bs), 0)
    qcol = lax.broadcasted_iota(jnp.int32, (bs, bs), 1)
    for i in range(nb):
        q = q_ref[i * bs:(i + 1) * bs, :].astype(F32)
        qn = (q * _rms_scale(q) * qg) * np.float32(MOBA_HEAD_DIM ** -0.5)
        qb = (qn * LOG2_E).astype(BF16)
        bias_t = None
        if i > MOBA_TOPK:
            sbt = lax.dot_general(km, qn, (((1,), (1,)), ((), ())),
                                  precision=lax.Precision.HIGHEST, preferred_element_type=F32)
            cnt = jnp.zeros((nb, bs), F32)
            for jp in range(i):
                sp = sbt[jp:jp + 1, :]
                beats = jnp.where(sp > sbt, 1.0, jnp.where(sp == sbt, jnp.where(blk > jp, 1.0, 0.0), 0.0))
                cnt = cnt + beats
            bias_t = jnp.where(cnt < MOBA_TOPK, 0.0, NEG_BIG)
        nk = (i + 1) * bs
        s_buf = st_ref.at[i % 2]
        p_buf = pt_ref.at[i % 2]
        mx8 = None
        for j in range(i + 1):
            ks = slice(j * bs, (j + 1) * bs)
            sj = lax.dot_general(kn_ref[ks, :], qb, _NT_DIMS, preferred_element_type=F32)
            if j == i:
                sj = jnp.where(krow <= qcol, sj, NEG_BIG)
            elif bias_t is not None:
                sj = sj + bias_t[j:j + 1, :]
            s_buf[ks, :] = sj
            pm = jnp.max(sj.reshape(bs // SUBLANES, SUBLANES, bs), axis=0)
            mx8 = pm if mx8 is None else jnp.maximum(mx8, pm)
        m = jnp.max(mx8, axis=0, keepdims=True)
        for j in range(i + 1):
            ks = slice(j * bs, (j + 1) * bs)
            p_buf[ks, :] = jnp.exp2(s_buf[ks, :] - m).astype(BF16)
        ot = jnp.dot(vt_ref[:, 0:nk], p_buf[0:nk, :], preferred_element_type=F32)
        o_ref[i * bs:(i + 1) * bs, :] = (ot[0:hd, :] / ot[hd:hd + 1, :]).T.astype(o_ref.dtype)


def moba_mixer(proj, q_g, k_g, layer, *, batch, seq):
    bs, hd, nh = MOBA_BLOCK, MOBA_HEAD_DIM, MOBA_HEADS
    nb = seq // bs
    kern = functools.partial(_moba_kernel, nb=nb)
    blk = lambda col: pl.BlockSpec((seq, hd), lambda b, h: (b, col // hd + h))
    return pl.pallas_call(
        kern,
        out_shape=jax.ShapeDtypeStruct((batch * seq, MOBA_WIDTH), BF16),
        grid=(batch, nh),
        in_specs=[
            blk(COL_MOBA_Q), blk(COL_MOBA_K), blk(COL_MOBA_V),
            _const_spec((None, 1, hd), lambda b, h: (layer, 0, 0)),
            _const_spec((None, 1, hd), lambda b, h: (layer, 0, 0)),
        ],
        out_specs=pl.BlockSpec((seq, hd), lambda b, h: (b, h)),
        scratch_shapes=[pltpu.VMEM((seq, hd), BF16), pltpu.VMEM((hd + BF16_ROWS, seq), BF16),
                        pltpu.VMEM((nb, hd), F32),
                        pltpu.VMEM((2, seq, bs), F32), pltpu.VMEM((2, seq, bs), BF16)],
        compiler_params=_cparams(("parallel", "parallel")),
        name="moba_mixer",
    )(proj, proj, proj, q_g, k_g)


_WEIGHT_OUTER = ("arbitrary", "arbitrary")


def _merge_kernel(ys_ref, yg_ref, ym_ref, gs_ref, gg_ref, gm_ref, ws_ref, wg_ref, wm_ref, o_ref,
                  wsb_ref, wgb_ref, wmb_ref):
    @pl.when(pl.program_id(1) == 0)
    def _():
        wsb_ref[...] = ws_ref[...].astype(BF16)
        wgb_ref[...] = wg_ref[...].astype(BF16)
        wmb_ref[...] = wm_ref[...].astype(BF16)

    def branch(g_ref, y_ref, w_ref):
        return _sigmoid(g_ref[...].astype(F32)) * jnp.dot(y_ref[...], w_ref[...], preferred_element_type=F32)

    acc = branch(gs_ref, ys_ref, wsb_ref) + branch(gg_ref, yg_ref, wgb_ref) + branch(gm_ref, ym_ref, wmb_ref)
    o_ref[...] = acc.astype(o_ref.dtype)


def merge_branches(y_ssm, y_gla, y_moba, proj, w_s, w_g, w_m, layer, *, tm, tn):
    m, kdim = y_ssm.shape
    n = w_s.shape[-1]
    yspec = pl.BlockSpec((tm, kdim), lambda j, i: (i, 0))
    gspec = lambda col: pl.BlockSpec((tm, tn), lambda j, i: (i, col // tn + j))
    wspec = pl.BlockSpec((None, kdim, tn), lambda j, i: (layer, 0, j))
    return pl.pallas_call(
        _merge_kernel,
        out_shape=jax.ShapeDtypeStruct((m, n), BF16),
        grid=(n // tn, m // tm),
        in_specs=[yspec, yspec, yspec, gspec(COL_GATE_SSM), gspec(COL_GATE_GLA), gspec(COL_GATE_MOBA),
                  wspec, wspec, wspec],
        out_specs=pl.BlockSpec((tm, tn), lambda j, i: (i, j)),
        scratch_shapes=[pltpu.VMEM((kdim, tn), BF16)] * 3,
        compiler_params=_cparams(_WEIGHT_OUTER),
        name="merge",
    )(y_ssm, y_gla, y_moba, proj, proj, proj, w_s, w_g, w_m)


def _out_proj_kernel(a_ref, w_ref, x_ref, g_ref, o_ref, h_ref, wbf_ref):
    @pl.when(pl.program_id(0) == 0)
    def _():
        wbf_ref[...] = w_ref[...].astype(BF16)

    xn = x_ref[...] + jnp.dot(a_ref[...], wbf_ref[...], preferred_element_type=F32)
    o_ref[...] = xn
    h_ref[...] = (xn * _rms_scale(xn) * g_ref[...]).astype(h_ref.dtype)


def out_proj_norm(a, w_all, x, g, layer, *, tm):
    m, kdim = a.shape
    n = w_all.shape[-1]
    row = lambda width: pl.BlockSpec((tm, width), lambda i: (i, 0))
    return pl.pallas_call(
        _out_proj_kernel,
        out_shape=(jax.ShapeDtypeStruct((m, n), F32), jax.ShapeDtypeStruct((m, n), BF16)),
        grid=(m // tm,),
        in_specs=[row(kdim), _const_spec((None, kdim, n), lambda i: (layer, 0, 0)), row(n),
                  _const_spec((None, 1, n), lambda i: (layer, 0, 0))],
        out_specs=(row(n), row(n)),
        scratch_shapes=[pltpu.VMEM((kdim, n), BF16)],
        compiler_params=_cparams(("arbitrary",)),
        name="out_proj",
    )(a, w_all, x, g)


def _resid_matmul_kernel(a_ref, w_ref, x_ref, o_ref, wbf_ref):
    @pl.when(pl.program_id(1) == 0)
    def _():
        wbf_ref[...] = w_ref[...].astype(BF16)

    o_ref[...] = x_ref[...] + jnp.dot(a_ref[...], wbf_ref[...], preferred_element_type=F32)


def resid_matmul(a, w_all, x, layer, *, tm, tn):
    m, kdim = a.shape
    n = w_all.shape[-1]
    return pl.pallas_call(
        _resid_matmul_kernel,
        out_shape=jax.ShapeDtypeStruct((m, n), F32),
        grid=(n // tn, m // tm),
        in_specs=[
            pl.BlockSpec((tm, kdim), lambda j, i: (i, 0)),
            _const_spec((None, kdim, tn), lambda j, i: (layer, 0, j)),
            pl.BlockSpec((tm, tn), lambda j, i: (i, j)),
        ],
        out_specs=pl.BlockSpec((tm, tn), lambda j, i: (i, j)),
        scratch_shapes=[pltpu.VMEM((kdim, tn), BF16)],
        compiler_params=_cparams(_WEIGHT_OUTER),
        name="resid_matmul",
    )(a, w_all, x)


def _ffn_up_kernel(h_ref, wg_ref, wu_ref, o_ref, wgb_ref, wub_ref):
    @pl.when(pl.program_id(1) == 0)
    def _():
        wgb_ref[...] = wg_ref[...].astype(BF16)
        wub_ref[...] = wu_ref[...].astype(BF16)

    h = h_ref[...]
    gate = jnp.dot(h, wgb_ref[...], preferred_element_type=F32)
    up = jnp.dot(h, wub_ref[...], preferred_element_type=F32)
    o_ref[...] = (_silu(gate) * up).astype(o_ref.dtype)


def ffn_up(h, w_gate, w_up, layer, *, tm, tn):
    m, d = h.shape
    n = w_gate.shape[-1]
    wspec = pl.BlockSpec((None, d, tn), lambda j, i: (layer, 0, j))
    return pl.pallas_call(
        _ffn_up_kernel,
        out_shape=jax.ShapeDtypeStruct((m, n), BF16),
        grid=(n // tn, m // tm),
        in_specs=[pl.BlockSpec((tm, d), lambda j, i: (i, 0)), wspec, wspec],
        out_specs=pl.BlockSpec((tm, tn), lambda j, i: (i, j)),
        scratch_shapes=[pltpu.VMEM((d, tn), BF16)] * 2,
        compiler_params=_cparams(_WEIGHT_OUTER),
        name="ffn_up",
    )(h, w_gate, w_up)


def _gla_tri(tg):
    r = np.arange(tg)
    same_chunk = (r[:, None] // GLA_CHUNK) == (r[None, :] // GLA_CHUNK)
    return jnp.asarray(np.where(same_chunk & (r[None, :] <= r[:, None]), 1.0, 0.0), dtype=BF16)


def kernel(x, norm1_g, w_in, ssm_lambda_re, ssm_lambda_im, ssm_log_dt, ssm_b_re, ssm_b_im, ssm_c_re, ssm_c_im, ssm_d, ssm_w_glu, ssm_b_glu, gla_w_gate2, gla_b_gate2, gla_norm_g, moba_q_norm_g, moba_k_norm_g, w_branch_ssm, w_branch_gla, w_branch_moba, w_out, norm2_g, ffn_w_gate, ffn_w_up, ffn_w_down):
    batch, seq, d = x.shape
    depth = w_in.shape[0]
    m = batch * seq

    w_in_t = jnp.swapaxes(w_in, 1, 2)
    wg2_pad = jnp.pad(gla_w_gate2, ((0, 0), (0, LANES - GLA_GATE_RANK), (0, 0))).astype(BF16)
    s5_win, s5_wout, a_re_t, a_im_t = s5_weights(ssm_lambda_re, ssm_lambda_im, ssm_log_dt, ssm_b_re, ssm_b_im,
                                                 ssm_c_re, ssm_c_im)
    row3 = lambda t: t[:, None, :]
    n1, n2 = row3(norm1_g), row3(norm2_g)
    ssm_d3, b_glu3 = row3(ssm_d), row3(ssm_b_glu)
    bg2, gng = row3(gla_b_gate2), row3(gla_norm_g)
    mqg, mkg = row3(moba_q_norm_g), row3(moba_k_norm_g)

    tg = 256
    tri = _gla_tri(tg)
    xf = x.reshape(m, d)
    for l in range(depth):
        xn = rmsnorm(xf, n1, l, tm=512)
        proj = in_proj(xn, w_in_t, l, tm=2048, tn=1024)
        y_ssm = s5_mixer(proj, s5_win, s5_wout, a_re_t, a_im_t, ssm_d3, ssm_w_glu, b_glu3, l,
                         batch=batch, seq=seq, tc=512)
        y_gla = gla_mixer(proj, xn, w_in_t, wg2_pad, bg2, gng, tri, l, batch=batch, seq=seq, tg=tg)
        y_moba = moba_mixer(proj, mqg, mkg, l, batch=batch, seq=seq)
        merged = merge_branches(y_ssm, y_gla, y_moba, proj, w_branch_ssm, w_branch_gla, w_branch_moba, l,
                                tm=512, tn=1024)
        xf, hn = out_proj_norm(merged, w_out, xf, n2, l, tm=512)
        act = ffn_up(hn, ffn_w_gate, ffn_w_up, l, tm=2048, tn=512)
        xf = resid_matmul(act, ffn_w_down, xf, l, tm=512, tn=1024)
    return xf.reshape(batch, seq, d)
```

```python
import functools

import jax
import jax.numpy as jnp
import numpy as np
from jax import lax
from jax.experimental import pallas as pl
from jax.experimental.pallas import tpu as pltpu

F32 = jnp.float32
BF16 = jnp.bfloat16

D_MODEL = 2048
DEPTH = 4
RMS_EPS = 1e-6
SSM_WIDTH = D_MODEL // 2
SSM_GROUP_SIZE = 16
SSM_GROUPS = SSM_WIDTH // SSM_GROUP_SIZE
SSM_STATE = 64
GLA_HEADS = 4
GLA_KEY_WIDTH = D_MODEL // 4
GLA_VALUE_WIDTH = D_MODEL // 2
GLA_DK = GLA_KEY_WIDTH // GLA_HEADS
GLA_DV = GLA_VALUE_WIDTH // GLA_HEADS
GLA_GATE_RANK = 16
GLA_GATE_NORM = 16.0
GLA_CHUNK = 64
MOBA_WIDTH = D_MODEL // 2
MOBA_HEAD_DIM = 128
MOBA_HEADS = MOBA_WIDTH // MOBA_HEAD_DIM
MOBA_BLOCK = 256
MOBA_TOPK = 3
FFN_HIDDEN = -(-8 * D_MODEL // (3 * 256)) * 256

LANES = 128
SUBLANES = 8
MXU_DIM = 256
VMEM_LIMIT = 56 * 1024 * 1024

COL_SSM = 0
COL_GLA_Q = COL_SSM + SSM_WIDTH
COL_GLA_K = COL_GLA_Q + GLA_KEY_WIDTH
COL_GLA_V = COL_GLA_K + GLA_KEY_WIDTH
COL_GLA_R = COL_GLA_V + GLA_VALUE_WIDTH
COL_MOBA_Q = COL_GLA_R + GLA_VALUE_WIDTH
COL_MOBA_K = COL_MOBA_Q + MOBA_WIDTH
COL_MOBA_V = COL_MOBA_K + MOBA_WIDTH
COL_GATE_SSM = COL_MOBA_V + MOBA_WIDTH
COL_GATE_GLA = COL_GATE_SSM + D_MODEL
COL_GATE_MOBA = COL_GATE_GLA + D_MODEL
PROJ_WIDTH = COL_GATE_MOBA + D_MODEL
ORIG_GLR = COL_MOBA_Q

S5_PAIRS = SSM_GROUPS // 2
S5_OCTETS = S5_PAIRS // SUBLANES
S5_OCT_CH = SSM_WIDTH // S5_OCTETS
S5_TILE_PAIRS = LANES // (2 * SSM_GROUP_SIZE)
S5_PITCH = S5_PAIRS + 4

NEG_BIG = -1e30
LOG2_E = float(np.log2(np.e))
BF16_ROWS = 2 * SUBLANES


def _cparams(sem):
    return pltpu.CompilerParams(dimension_semantics=sem, vmem_limit_bytes=VMEM_LIMIT)


def _const_spec(shape, index_map):
    return pl.BlockSpec(shape, index_map, pipeline_mode=pl.Buffered(1))


def _sigmoid(x):
    return 1.0 / (1.0 + jnp.exp(-x))


def _silu(x):
    return x * _sigmoid(x)


def _gelu_tanh(x):
    c = np.float32(np.sqrt(2.0 / np.pi))
    return 0.5 * x * (1.0 + jnp.tanh(c * (x + np.float32(0.044715) * (x * x * x))))


def _rms_scale(x):
    return lax.rsqrt(jnp.mean(x * x, axis=-1, keepdims=True) + RMS_EPS)


def _rmsnorm_kernel(x_ref, g_ref, o_ref):
    x = x_ref[...]
    o_ref[...] = (x * _rms_scale(x) * g_ref[...]).astype(o_ref.dtype)


def rmsnorm(x, g, layer, *, tm):
    m, d = x.shape
    return pl.pallas_call(
        _rmsnorm_kernel,
        out_shape=jax.ShapeDtypeStruct((m, d), BF16),
        grid=(m // tm,),
        in_specs=[pl.BlockSpec((tm, d), lambda i: (i, 0)),
                  _const_spec((None, 1, d), lambda i: (layer, 0, 0))],
        out_specs=pl.BlockSpec((tm, d), lambda i: (i, 0)),
        compiler_params=_cparams(("parallel",)),
        name="rmsnorm",
    )(x, g)


_NT_DIMS = (((1,), (1,)), ((), ()))


def _in_proj_kernel(xn_ref, wa_ref, wb_ref, o_ref, wbf_ref, *, n_aligned):
    j = pl.program_id(0)
    first = pl.program_id(1) == 0
    tn = wa_ref.shape[0]
    sh = GLA_GATE_RANK

    @pl.when(jnp.logical_and(first, j < n_aligned))
    def _():
        wbf_ref[...] = wa_ref[...].astype(BF16)

    @pl.when(jnp.logical_and(first, j >= n_aligned))
    def _():
        wbf_ref[0:tn - sh, :] = wa_ref[sh:tn, :].astype(BF16)
        wbf_ref[tn - sh:tn, :] = wb_ref[...].astype(BF16)

    o_ref[...] = lax.dot_general(xn_ref[...], wbf_ref[...], _NT_DIMS,
                                 preferred_element_type=F32).astype(o_ref.dtype)


def in_proj(xn, w_in_t, layer, *, tm, tn):
    m, d = xn.shape
    sh = GLA_GATE_RANK
    assert ORIG_GLR % tn == 0 and PROJ_WIDTH % tn == 0 and tn % sh == 0
    kern = functools.partial(_in_proj_kernel, n_aligned=ORIG_GLR // tn)
    return pl.pallas_call(
        kern,
        out_shape=jax.ShapeDtypeStruct((m, PROJ_WIDTH), BF16),
        grid=(PROJ_WIDTH // tn, m // tm),
        in_specs=[
            pl.BlockSpec((tm, d), lambda j, i: (i, 0)),
            pl.BlockSpec((None, tn, d), lambda j, i: (layer, j, 0)),
            pl.BlockSpec((None, sh, d), lambda j, i: (layer, (j + 1) * (tn // sh), 0)),
        ],
        out_specs=pl.BlockSpec((tm, tn), lambda j, i: (i, j)),
        scratch_shapes=[pltpu.VMEM((tn, d), BF16)],
        compiler_params=_cparams(("arbitrary", "arbitrary")),
        name="in_proj",
    )(xn, w_in_t, w_in_t)


def _s5_weights_kernel(lre_ref, lim_ref, ldt_ref, bre_ref, bim_ref, cre_ref, cim_ref, rep_ref, rept_ref,
                       win_ref, wout_ref, are_ref, aim_ref):
    lre = lre_ref[...]
    lim = lim_ref[...]
    dt = jnp.exp(ldt_ref[...])
    mag = jnp.exp(lre * dt)
    ang = lim * dt
    a_re = mag * jnp.cos(ang)
    a_im = mag * jnp.sin(ang)
    den = lre * lre + lim * lim
    n_re = a_re - 1.0
    n_im = a_im
    z_re = (n_re * lre + n_im * lim) / den
    z_im = (n_im * lre - n_re * lim) / den
    b_re = bre_ref[...]
    b_im = bim_ref[...]
    are_ref[...] = a_re
    aim_ref[...] = a_im
    bb_re = (z_re * b_re - z_im * b_im).astype(BF16)
    bb_im = (z_re * b_im + z_im * b_re).astype(BF16)

    def own_and_re(s, ch):
        pair = jnp.right_shift(s, 8)
        half = jnp.bitwise_and(jnp.right_shift(s, 6), 1)
        own = jnp.right_shift(ch, 4) == 2 * pair + half
        return own, jnp.bitwise_and(jnp.right_shift(s, 7), 1) == 0

    shape = win_ref.shape
    own, is_re = own_and_re(lax.broadcasted_iota(jnp.int32, shape, 1), lax.broadcasted_iota(jnp.int32, shape, 0))
    t_re = jnp.dot(bb_re, rep_ref[...], preferred_element_type=F32)
    t_im = jnp.dot(bb_im, rep_ref[...], preferred_element_type=F32)
    win_ref[...] = jnp.where(own, jnp.where(is_re, t_re, t_im), 0.0).astype(win_ref.dtype)

    shape = wout_ref.shape
    own, is_re = own_and_re(lax.broadcasted_iota(jnp.int32, shape, 0), lax.broadcasted_iota(jnp.int32, shape, 1))
    t_re = lax.dot_general(rept_ref[...], cre_ref[...].astype(BF16), _NT_DIMS, preferred_element_type=F32)
    t_im = lax.dot_general(rept_ref[...], cim_ref[...].astype(BF16), _NT_DIMS, preferred_element_type=F32)
    wout_ref[...] = jnp.where(own, jnp.where(is_re, t_re, -t_im), 0.0).astype(wout_ref.dtype)


def s5_weights(lam_re, lam_im, log_dt, b_re, b_im, c_re, c_im):
    depth, g, p = lam_re.shape
    s = b_re.shape[-1]
    gpo = g // S5_OCTETS
    blk_rows = gpo * s
    rows = depth * g * s
    rep_rows = lambda t: jnp.broadcast_to(t[:, :, None, :], (depth, g, s, p)).reshape(rows, p)
    ldt = jnp.broadcast_to(log_dt[:, :, None, None], (depth, g, s, p)).reshape(rows, p)
    b_gcp = lambda t: jnp.swapaxes(t, 2, 3).reshape(rows, p)
    span = SUBLANES * 2 * LANES
    rep = jnp.asarray((np.arange(span)[None, :] % p) == np.arange(p)[:, None], dtype=BF16)
    blk = pl.BlockSpec((blk_rows, p), lambda l, o: (l * S5_OCTETS + o, 0))
    w_in, w_out, a_re, a_im = pl.pallas_call(
        _s5_weights_kernel,
        out_shape=(jax.ShapeDtypeStruct((depth, S5_OCTETS, blk_rows, span), BF16),
                   jax.ShapeDtypeStruct((depth, S5_OCTETS, span, blk_rows), BF16),
                   jax.ShapeDtypeStruct((rows, p), F32), jax.ShapeDtypeStruct((rows, p), F32)),
        grid=(depth, S5_OCTETS),
        in_specs=[blk] * 7 + [_const_spec((p, span), lambda l, o: (0, 0)),
                              _const_spec((span, p), lambda l, o: (0, 0))],
        out_specs=(pl.BlockSpec((None, None, blk_rows, span), lambda l, o: (l, o, 0, 0)),
                   pl.BlockSpec((None, None, span, blk_rows), lambda l, o: (l, o, 0, 0)),
                   blk, blk),
        compiler_params=_cparams(("parallel", "parallel")),
        name="s5_weights",
    )(rep_rows(lam_re), rep_rows(lam_im), ldt, b_gcp(b_re), b_gcp(b_im),
      c_re.reshape(rows, p), c_im.reshape(rows, p), rep, rep.T)
    tab = lambda a: a.reshape(depth, g, s, p)[:, :, 0, :].reshape(depth, S5_PAIRS, LANES)
    return w_in, w_out, tab(a_re), tab(a_im)


def _s5_kernel(u_ref, win_ref, wout_ref, are_ref, aim_ref, d_ref, wglu_ref, bglu_ref, o_ref,
               bur_ref, bui_ref, sre_ref, sim_ref, y_ref, *, tc):
    @pl.when(pl.program_id(1) == 0)
    def _():
        sre_ref[...] = jnp.zeros_like(sre_ref)
        sim_ref[...] = jnp.zeros_like(sim_ref)

    half = tc // 2
    pair_w = 2 * LANES
    groups = S5_PAIRS // 2
    per = half // groups

    def in_map(j, h):
        o, jl = divmod(j, SUBLANES)
        t = jl // S5_TILE_PAIRS
        ch0 = o * S5_OCT_CH + t * LANES
        uo = u_ref[h * half:(h + 1) * half, ch0:ch0 + LANES]
        res = jnp.dot(uo, win_ref[o, t * LANES:(t + 1) * LANES, jl * pair_w:(jl + 1) * pair_w],
                      preferred_element_type=F32)
        base = h * half * S5_PITCH + j
        bur_ref[pl.ds(base, half, stride=S5_PITCH), :] = res[:, :LANES]
        bui_ref[pl.ds(base, half, stride=S5_PITCH), :] = res[:, LANES:]

    def out_map(j, h):
        o, jl = divmod(j, SUBLANES)
        t = jl // S5_TILE_PAIRS
        base = h * half * S5_PITCH + j
        xr_j = bur_ref[pl.ds(base, half, stride=S5_PITCH), :]
        xi_j = bui_ref[pl.ds(base, half, stride=S5_PITCH), :]
        lhs = jnp.concatenate([xr_j, xi_j], axis=1).astype(BF16)
        return jnp.dot(lhs, wout_ref[o, jl * pair_w:(jl + 1) * pair_w, t * LANES:(t + 1) * LANES],
                       preferred_element_type=F32)

    def accumulate(acc, j, part, rows):
        acc = part if j % S5_TILE_PAIRS == 0 else acc + part
        if j % S5_TILE_PAIRS == S5_TILE_PAIRS - 1:
            ch0 = (j // S5_TILE_PAIRS) * LANES
            y_ref[rows, ch0:ch0 + LANES] = acc
        return acc

    a_re = [are_ref[q * SUBLANES:(q + 1) * SUBLANES, :] for q in range(S5_OCTETS)]
    a_im = [aim_ref[q * SUBLANES:(q + 1) * SUBLANES, :] for q in range(S5_OCTETS)]

    def scan(h, t0, carry):
        xr, xi = carry
        for t in range(t0, t0 + per):
            nr, ni = [], []
            for q in range(S5_OCTETS):
                row = (h * half + t) * S5_PITCH + q * SUBLANES
                br = bur_ref[row:row + SUBLANES, :]
                bi = bui_ref[row:row + SUBLANES, :]
                r = a_re[q] * xr[q] - a_im[q] * xi[q] + br
                i = a_re[q] * xi[q] + a_im[q] * xr[q] + bi
                bur_ref[row:row + SUBLANES, :] = r
                bui_ref[row:row + SUBLANES, :] = i
                nr.append(r)
                ni.append(i)
            xr, xi = nr, ni
        return xr, xi

    for j in range(S5_PAIRS):
        in_map(j, 0)
    carry = ([sre_ref[q * SUBLANES:(q + 1) * SUBLANES, :] for q in range(S5_OCTETS)],
             [sim_ref[q * SUBLANES:(q + 1) * SUBLANES, :] for q in range(S5_OCTETS)])
    for g in range(groups):
        carry = scan(0, g * per, carry)
        in_map(2 * g, 1)
        in_map(2 * g + 1, 1)
    acc = None
    for g in range(groups):
        carry = scan(1, g * per, carry)
        for j in (2 * g, 2 * g + 1):
            acc = accumulate(acc, j, out_map(j, 0), slice(0, half))
    for q in range(S5_OCTETS):
        sre_ref[q * SUBLANES:(q + 1) * SUBLANES, :] = carry[0][q]
        sim_ref[q * SUBLANES:(q + 1) * SUBLANES, :] = carry[1][q]
    for j in range(S5_PAIRS):
        acc = accumulate(acc, j, out_map(j, 1), slice(half, tc))

    y = y_ref[...] + d_ref[...] * u_ref[...].astype(F32)
    z = _gelu_tanh(y)
    gate = jnp.dot(z.astype(BF16), wglu_ref[...].astype(BF16), preferred_element_type=F32) + bglu_ref[...]
    o_ref[...] = (z * _sigmoid(gate)).astype(o_ref.dtype)


def s5_mixer(proj, w_in_blk, w_out_blk, a_re, a_im, d_skip, w_glu, b_glu, layer, *, batch, seq, tc):
    nt = seq // tc
    w = SSM_WIDTH
    kern = functools.partial(_s5_kernel, tc=tc)
    return pl.pallas_call(
        kern,
        out_shape=jax.ShapeDtypeStruct((batch * seq, w), BF16),
        grid=(batch, nt),
        in_specs=[
            pl.BlockSpec((tc, w), lambda b, c: (b * nt + c, COL_SSM // w)),
            _const_spec((None,) + w_in_blk.shape[1:], lambda b, c: (layer, 0, 0, 0)),
            _const_spec((None,) + w_out_blk.shape[1:], lambda b, c: (layer, 0, 0, 0)),
            _const_spec((None, S5_PAIRS, LANES), lambda b, c: (layer, 0, 0)),
            _const_spec((None, S5_PAIRS, LANES), lambda b, c: (layer, 0, 0)),
            _const_spec((None, 1, w), lambda b, c: (layer, 0, 0)),
            _const_spec((None, w, w), lambda b, c: (layer, 0, 0)),
            _const_spec((None, 1, w), lambda b, c: (layer, 0, 0)),
        ],
        out_specs=pl.BlockSpec((tc, w), lambda b, c: (b * nt + c, 0)),
        scratch_shapes=[
            pltpu.VMEM((tc * S5_PITCH, LANES), F32),
            pltpu.VMEM((tc * S5_PITCH, LANES), F32),
            pltpu.VMEM((S5_PAIRS, LANES), F32),
            pltpu.VMEM((S5_PAIRS, LANES), F32),
            pltpu.VMEM((tc, w), F32),
        ],
        compiler_params=_cparams(("parallel", "arbitrary")),
        name="s5_mixer",
    )(proj, w_in_blk, w_out_blk, a_re, a_im, d_skip, w_glu, b_glu)


def _split3(x):
    hi = x.astype(BF16)
    r1 = x - hi.astype(F32)
    mid = r1.astype(BF16)
    lo = (r1 - mid.astype(F32)).astype(BF16)
    return hi, mid, lo


def _gla_kernel(q_ref, k_ref, v_ref, r_ref, xn_ref, wglr_ref, wg_ref, bg_ref, ng_ref, tri_ref, o_ref, st_ref,
                bcum_ref, *, tg):
    c = GLA_CHUNK

    @pl.when(pl.program_id(1) == 0)
    def _():
        st_ref[...] = jnp.zeros_like(st_ref)
        wl = wglr_ref[...].astype(BF16)
        wl = jnp.concatenate([wl, jnp.zeros((LANES - GLA_GATE_RANK, wl.shape[1]), BF16)], axis=0)
        tri = tri_ref[...]
        for t0 in range(0, xn_ref.shape[0], tg):
            ts = slice(t0, t0 + tg)
            glr = lax.dot_general(xn_ref[ts, :], wl, _NT_DIMS, preferred_element_type=F32).astype(BF16)
            glog = jnp.dot(glr, wg_ref[...], preferred_element_type=F32) + bg_ref[...]
            g = (jnp.minimum(glog, 0.0) - jnp.log1p(jnp.exp(-jnp.abs(glog)))) / GLA_GATE_NORM
            acc = None
            for piece in _split3(g):
                part = jnp.dot(tri, piece, preferred_element_type=F32)
                acc = part if acc is None else acc + part
            bcum_ref[ts, :] = acc

    bcum = bcum_ref[pl.ds(pl.multiple_of(pl.program_id(1) * tg, tg), tg), :]

    row = lax.broadcasted_iota(jnp.int32, (c, c), 0)
    col = lax.broadcasted_iota(jnp.int32, (c, c), 1)
    causal = col <= row
    scale = np.float32(GLA_DK ** -0.5)
    ng = ng_ref[...]
    for n in range(tg // c):
        rs = slice(n * c, (n + 1) * c)
        for h in range(GLA_HEADS):
            ks = slice(h * GLA_DK, (h + 1) * GLA_DK)
            vs = slice(h * GLA_DV, (h + 1) * GLA_DV)
            b = bcum[rs, ks]
            bl = b[c - 1:c, :]
            qh = q_ref[rs, ks].astype(F32) * scale
            kh = k_ref[rs, ks].astype(F32)
            vh = v_ref[rs, vs]
            q_dec = (qh * jnp.exp(b)).astype(BF16)
            k_inv = (kh * jnp.exp(-b)).astype(BF16)
            k_dec = (kh * jnp.exp(bl - b)).astype(BF16)
            attn = lax.dot_general(q_dec, k_inv, (((1,), (1,)), ((), ())), preferred_element_type=F32)
            attn = jnp.where(causal, attn, 0.0).astype(BF16)
            st = st_ref[h]
            o = (jnp.dot(attn, vh, preferred_element_type=F32)
                 + lax.dot_general(q_dec, st.astype(BF16), (((1,), (1,)), ((), ())),
                                   preferred_element_type=F32))
            kvt = lax.dot_general(vh, k_dec, (((0,), (0,)), ((), ())), preferred_element_type=F32)
            st_ref[h] = jnp.exp(bl) * st + kvt
            o = o * _rms_scale(o) * ng
            o_ref[rs, vs] = (o * _silu(r_ref[rs, vs].astype(F32))).astype(o_ref.dtype)


def gla_mixer(proj, xn, w_in_t, w_gate2_pad, b_gate2, norm_g, tri, layer, *, batch, seq, tg):
    nt = seq // tg
    kw, vw = GLA_KEY_WIDTH, GLA_VALUE_WIDTH
    d = xn.shape[1]
    sh = GLA_GATE_RANK
    kern = functools.partial(_gla_kernel, tg=tg)
    return pl.pallas_call(
        kern,
        out_shape=jax.ShapeDtypeStruct((batch * seq, vw), BF16),
        grid=(batch, nt),
        in_specs=[
            pl.BlockSpec((tg, kw), lambda b, c: (b * nt + c, COL_GLA_Q // kw)),
            pl.BlockSpec((tg, kw), lambda b, c: (b * nt + c, COL_GLA_K // kw)),
            pl.BlockSpec((tg, vw), lambda b, c: (b * nt + c, COL_GLA_V // vw)),
            pl.BlockSpec((tg, vw), lambda b, c: (b * nt + c, COL_GLA_R // vw)),
            pl.BlockSpec((seq, d), lambda b, c: (b, 0)),
            _const_spec((None, sh, d), lambda b, c: (layer, ORIG_GLR // sh, 0)),
            _const_spec((None, LANES, kw), lambda b, c: (layer, 0, 0)),
            _const_spec((None, 1, kw), lambda b, c: (layer, 0, 0)),
            _const_spec((None, 1, GLA_DV), lambda b, c: (layer, 0, 0)),
            _const_spec((tg, tg), lambda b, c: (0, 0)),
        ],
        out_specs=pl.BlockSpec((tg, vw), lambda b, c: (b * nt + c, 0)),
        scratch_shapes=[pltpu.VMEM((GLA_HEADS, GLA_DV, GLA_DK), F32), pltpu.VMEM((seq, kw), F32)],
        compiler_params=_cparams(("parallel", "arbitrary")),
        name="gla_mixer",
    )(proj, proj, proj, proj, xn, w_in_t, w_gate2_pad, b_gate2, norm_g, tri)


def _moba_kernel(q_ref, k_ref, v_ref, qg_ref, kg_ref, o_ref, kn_ref, vt_ref, km_ref, st_ref, pt_ref, *, nb):
    bs = MOBA_BLOCK
    kg = kg_ref[...]
    for jb in range(nb):
        kb = k_ref[jb * bs:(jb + 1) * bs, :].astype(F32)
        kn = kb * _rms_scale(kb) * kg
        kn_ref[jb * bs:(jb + 1) * bs, :] = kn.astype(BF16)
        km_ref[jb:jb + 1, :] = jnp.mean(kn, axis=0, keepdims=True)
    hd = MOBA_HEAD_DIM
    vt_ref[0:hd, :] = v_ref[...].astype(F32).T.astype(BF16)
    ones_row = lax.broadcasted_iota(jnp.int32, (vt_ref.shape[0] - hd, vt_ref.shape[1]), 0) == 0
    vt_ref[hd:, :] = jnp.where(ones_row, 1.0, 0.0).astype(BF16)

    qg = qg_ref[...]
    km = km_ref[...]
    blk = lax.broadcasted_iota(jnp.int32, (nb, bs), 0)
    krow = lax.broadcasted_iota(jnp.int32, (bs, bs), 0)
    qcol = lax.broadcasted_iota(jnp.int32, (bs, bs), 1)
    for i in range(nb):
        q = q_ref[i * bs:(i + 1) * bs, :].astype(F32)
        qn = (q * _rms_scale(q) * qg) * np.float32(MOBA_HEAD_DIM ** -0.5)
        qb = (qn * LOG2_E).astype(BF16)
        bias_t = None
        if i > MOBA_TOPK:
            sbt = lax.dot_general(km, qn, (((1,), (1,)), ((), ())),
                                  precision=lax.Precision.HIGHEST, preferred_element_type=F32)
            cnt = jnp.zeros((nb, bs), F32)
            for jp in range(i):
                sp = sbt[jp:jp + 1, :]
                beats = jnp.where(sp > sbt, 1.0, jnp.where(sp == sbt, jnp.where(blk > jp, 1.0, 0.0), 0.0))
                cnt = cnt + beats
            bias_t = jnp.where(cnt < MOBA_TOPK, 0.0, NEG_BIG)
        nk = (i + 1) * bs
        s_buf = st_ref.at[i % 2]
        p_buf = pt_ref.at[i % 2]
        mx8 = None
        for j in range(i + 1):
            ks = slice(j * bs, (j + 1) * bs)
            sj = lax.dot_general(kn_ref[ks, :], qb, _NT_DIMS, preferred_element_type=F32)
            if j == i:
                sj = jnp.where(krow <= qcol, sj, NEG_BIG)
            elif bias_t is not None:
                sj = sj + bias_t[j:j + 1, :]
            s_buf[ks, :] = sj
            pm = jnp.max(sj.reshape(bs // SUBLANES, SUBLANES, bs), axis=0)
            mx8 = pm if mx8 is None else jnp.maximum(mx8, pm)
        m = jnp.max(mx8, axis=0, keepdims=True)
        for j in range(i + 1):
            ks = slice(j * bs, (j + 1) * bs)
            p_buf[ks, :] = jnp.exp2(s_buf[ks, :] - m).astype(BF16)
        ot = jnp.dot(vt_ref[:, 0:nk], p_buf[0:nk, :], preferred_element_type=F32)
        o_ref[i * bs:(i + 1) * bs, :] = (ot[0:hd, :] / ot[hd:hd + 1, :]).T.astype(o_ref.dtype)


def moba_mixer(proj, q_g, k_g, layer, *, batch, seq):
    bs, hd, nh = MOBA_BLOCK, MOBA_HEAD_DIM, MOBA_HEADS
    nb = seq // bs
    kern = functools.partial(_moba_kernel, nb=nb)
    blk = lambda col: pl.BlockSpec((seq, hd), lambda b, h: (b, col // hd + h))
    return pl.pallas_call(
        kern,
        out_shape=jax.ShapeDtypeStruct((batch * seq, MOBA_WIDTH), BF16),
        grid=(batch, nh),
        in_specs=[
            blk(COL_MOBA_Q), blk(COL_MOBA_K), blk(COL_MOBA_V),
            _const_spec((None, 1, hd), lambda b, h: (layer, 0, 0)),
            _const_spec((None, 1, hd), lambda b, h: (layer, 0, 0)),
        ],
        out_specs=pl.BlockSpec((seq, hd), lambda b, h: (b, h)),
        scratch_shapes=[pltpu.VMEM((seq, hd), BF16), pltpu.VMEM((hd + BF16_ROWS, seq), BF16),
                        pltpu.VMEM((nb, hd), F32),
                        pltpu.VMEM((2, seq, bs), F32), pltpu.VMEM((2, seq, bs), BF16)],
        compiler_params=_cparams(("parallel", "parallel")),
        name="moba_mixer",
    )(proj, proj, proj, q_g, k_g)


_WEIGHT_OUTER = ("arbitrary", "arbitrary")


def _merge_kernel(ys_ref, yg_ref, ym_ref, gs_ref, gg_ref, gm_ref, ws_ref, wg_ref, wm_ref, o_ref,
                  wsb_ref, wgb_ref, wmb_ref):
    @pl.when(pl.program_id(1) == 0)
    def _():
        wsb_ref[...] = ws_ref[...].astype(BF16)
        wgb_ref[...] = wg_ref[...].astype(BF16)
        wmb_ref[...] = wm_ref[...].astype(BF16)

    def branch(g_ref, y_ref, w_ref):
        return _sigmoid(g_ref[...].astype(F32)) * jnp.dot(y_ref[...], w_ref[...], preferred_element_type=F32)

    acc = branch(gs_ref, ys_ref, wsb_ref) + branch(gg_ref, yg_ref, wgb_ref) + branch(gm_ref, ym_ref, wmb_ref)
    o_ref[...] = acc.astype(o_ref.dtype)


def merge_branches(y_ssm, y_gla, y_moba, proj, w_s, w_g, w_m, layer, *, tm, tn):
    m, kdim = y_ssm.shape
    n = w_s.shape[-1]
    yspec = pl.BlockSpec((tm, kdim), lambda j, i: (i, 0))
    gspec = lambda col: pl.BlockSpec((tm, tn), lambda j, i: (i, col // tn + j))
    wspec = pl.BlockSpec((None, kdim, tn), lambda j, i: (layer, 0, j))
    return pl.pallas_call(
        _merge_kernel,
        out_shape=jax.ShapeDtypeStruct((m, n), BF16),
        grid=(n // tn, m // tm),
        in_specs=[yspec, yspec, yspec, gspec(COL_GATE_SSM), gspec(COL_GATE_GLA), gspec(COL_GATE_MOBA),
                  wspec, wspec, wspec],
        out_specs=pl.BlockSpec((tm, tn), lambda j, i: (i, j)),
        scratch_shapes=[pltpu.VMEM((kdim, tn), BF16)] * 3,
        compiler_params=_cparams(_WEIGHT_OUTER),
        name="merge",
    )(y_ssm, y_gla, y_moba, proj, proj, proj, w_s, w_g, w_m)


def _out_proj_kernel(a_ref, w_ref, x_ref, g_ref, o_ref, h_ref, wbf_ref):
    @pl.when(pl.program_id(0) == 0)
    def _():
        wbf_ref[...] = w_ref[...].astype(BF16)

    xn = x_ref[...] + jnp.dot(a_ref[...], wbf_ref[...], preferred_element_type=F32)
    o_ref[...] = xn
    h_ref[...] = (xn * _rms_scale(xn) * g_ref[...]).astype(h_ref.dtype)


def out_proj_norm(a, w_all, x, g, layer, *, tm):
    m, kdim = a.shape
    n = w_all.shape[-1]
    row = lambda width: pl.BlockSpec((tm, width), lambda i: (i, 0))
    return pl.pallas_call(
        _out_proj_kernel,
        out_shape=(jax.ShapeDtypeStruct((m, n), F32), jax.ShapeDtypeStruct((m, n), BF16)),
        grid=(m // tm,),
        in_specs=[row(kdim), _const_spec((None, kdim, n), lambda i: (layer, 0, 0)), row(n),
                  _const_spec((None, 1, n), lambda i: (layer, 0, 0))],
        out_specs=(row(n), row(n)),
        scratch_shapes=[pltpu.VMEM((kdim, n), BF16)],
        compiler_params=_cparams(("arbitrary",)),
        name="out_proj",
    )(a, w_all, x, g)


def _resid_matmul_kernel(a_ref, w_ref, x_ref, o_ref, wbf_ref):
    @pl.when(pl.program_id(1) == 0)
    def _():
        wbf_ref[...] = w_ref[...].astype(BF16)

    o_ref[...] = x_ref[...] + jnp.dot(a_ref[...], wbf_ref[...], preferred_element_type=F32)


def resid_matmul(a, w_all, x, layer, *, tm, tn):
    m, kdim = a.shape
    n = w_all.shape[-1]
    return pl.pallas_call(
        _resid_matmul_kernel,
        out_shape=jax.ShapeDtypeStruct((m, n), F32),
        grid=(n // tn, m // tm),
        in_specs=[
            pl.BlockSpec((tm, kdim), lambda j, i: (i, 0)),
            _const_spec((None, kdim, tn), lambda j, i: (layer, 0, j)),
            pl.BlockSpec((tm, tn), lambda j, i: (i, j)),
        ],
        out_specs=pl.BlockSpec((tm, tn), lambda j, i: (i, j)),
        scratch_shapes=[pltpu.VMEM((kdim, tn), BF16)],
        compiler_params=_cparams(_WEIGHT_OUTER),
        name="resid_matmul",
    )(a, w_all, x)


def _ffn_up_kernel(h_ref, wg_ref, wu_ref, o_ref, wgb_ref, wub_ref):
    @pl.when(pl.program_id(1) == 0)
    def _():
        wgb_ref[...] = wg_ref[...].astype(BF16)
        wub_ref[...] = wu_ref[...].astype(BF16)

    h = h_ref[...]
    gate = jnp.dot(h, wgb_ref[...], preferred_element_type=F32)
    up = jnp.dot(h, wub_ref[...], preferred_element_type=F32)
    o_ref[...] = (_silu(gate) * up).astype(o_ref.dtype)


def ffn_up(h, w_gate, w_up, layer, *, tm, tn):
    m, d = h.shape
    n = w_gate.shape[-1]
    wspec = pl.BlockSpec((None, d, tn), lambda j, i: (layer, 0, j))
    return pl.pallas_call(
        _ffn_up_kernel,
        out_shape=jax.ShapeDtypeStruct((m, n), BF16),
        grid=(n // tn, m // tm),
        in_specs=[pl.BlockSpec((tm, d), lambda j, i: (i, 0)), wspec, wspec],
        out_specs=pl.BlockSpec((tm, tn), lambda j, i: (i, j)),
        scratch_shapes=[pltpu.VMEM((d, tn), BF16)] * 2,
        compiler_params=_cparams(_WEIGHT_OUTER),
        name="ffn_up",
    )(h, w_gate, w_up)


def _gla_tri(tg):
    r = np.arange(tg)
    same_chunk = (r[:, None] // GLA_CHUNK) == (r[None, :] // GLA_CHUNK)
    return jnp.asarray(np.where(same_chunk & (r[None, :] <= r[:, None]), 1.0, 0.0), dtype=BF16)


def kernel(x, norm1_g, w_in, ssm_lambda_re, ssm_lambda_im, ssm_log_dt, ssm_b_re, ssm_b_im, ssm_c_re, ssm_c_im, ssm_d, ssm_w_glu, ssm_b_glu, gla_w_gate2, gla_b_gate2, gla_norm_g, moba_q_norm_g, moba_k_norm_g, w_branch_ssm, w_branch_gla, w_branch_moba, w_out, norm2_g, ffn_w_gate, ffn_w_up, ffn_w_down):
    batch, seq, d = x.shape
    depth = w_in.shape[0]
    m = batch * seq

    w_in_t = jnp.swapaxes(w_in, 1, 2)
    wg2_pad = jnp.pad(gla_w_gate2, ((0, 0), (0, LANES - GLA_GATE_RANK), (0, 0))).astype(BF16)
    s5_win, s5_wout, a_re_t, a_im_t = s5_weights(ssm_lambda_re, ssm_lambda_im, ssm_log_dt, ssm_b_re, ssm_b_im,
                                                 ssm_c_re, ssm_c_im)
    row3 = lambda t: t[:, None, :]
    n1, n2 = row3(norm1_g), row3(norm2_g)
    ssm_d3, b_glu3 = row3(ssm_d), row3(ssm_b_glu)
    bg2, gng = row3(gla_b_gate2), row3(gla_norm_g)
    mqg, mkg = row3(moba_q_norm_g), row3(moba_k_norm_g)

    tg = 256
    tri = _gla_tri(tg)
    xf = x.reshape(m, d)
    for l in range(depth):
        xn = rmsnorm(xf, n1, l, tm=512)
        proj = in_proj(xn, w_in_t, l, tm=2048, tn=1024)
        y_ssm = s5_mixer(proj, s5_win, s5_wout, a_re_t, a_im_t, ssm_d3, ssm_w_glu, b_glu3, l,
                         batch=batch, seq=seq, tc=512)
        y_gla = gla_mixer(proj, xn, w_in_t, wg2_pad, bg2, gng, tri, l, batch=batch, seq=seq, tg=tg)
        y_moba = moba_mixer(proj, mqg, mkg, l, batch=batch, seq=seq)
        merged = merge_branches(y_ssm, y_gla, y_moba, proj, w_branch_ssm, w_branch_gla, w_branch_moba, l,
                                tm=512, tn=1024)
        xf, hn = out_proj_norm(merged, w_out, xf, n2, l, tm=512)
        act = ffn_up(hn, ffn_w_gate, ffn_w_up, l, tm=1024, tn=512)
        xf = resid_matmul(act, ffn_w_down, xf, l, tm=512, tn=1024)
    return xf.reshape(batch, seq, d)
```

```python
import functools

import jax
import jax.numpy as jnp
import numpy as np
from jax import lax
from jax.experimental import pallas as pl
from jax.experimental.pallas import tpu as pltpu

F32 = jnp.float32
BF16 = jnp.bfloat16

D_MODEL = 2048
DEPTH = 4
RMS_EPS = 1e-6
SSM_WIDTH = D_MODEL // 2
SSM_GROUP_SIZE = 16
SSM_GROUPS = SSM_WIDTH // SSM_GROUP_SIZE
SSM_STATE = 64
GLA_HEADS = 4
GLA_KEY_WIDTH = D_MODEL // 4
GLA_VALUE_WIDTH = D_MODEL // 2
GLA_DK = GLA_KEY_WIDTH // GLA_HEADS
GLA_DV = GLA_VALUE_WIDTH // GLA_HEADS
GLA_GATE_RANK = 16
GLA_GATE_NORM = 16.0
GLA_CHUNK = 64
MOBA_WIDTH = D_MODEL // 2
MOBA_HEAD_DIM = 128
MOBA_HEADS = MOBA_WIDTH // MOBA_HEAD_DIM
MOBA_BLOCK = 256
MOBA_TOPK = 3
FFN_HIDDEN = -(-8 * D_MODEL // (3 * 256)) * 256

LANES = 128
SUBLANES = 8
MXU_DIM = 256
VMEM_LIMIT = 56 * 1024 * 1024

COL_SSM = 0
COL_GLA_Q = COL_SSM + SSM_WIDTH
COL_GLA_K = COL_GLA_Q + GLA_KEY_WIDTH
COL_GLA_V = COL_GLA_K + GLA_KEY_WIDTH
COL_GLA_R = COL_GLA_V + GLA_VALUE_WIDTH
COL_MOBA_Q = COL_GLA_R + GLA_VALUE_WIDTH
COL_MOBA_K = COL_MOBA_Q + MOBA_WIDTH
COL_MOBA_V = COL_MOBA_K + MOBA_WIDTH
COL_GATE_SSM = COL_MOBA_V + MOBA_WIDTH
COL_GATE_GLA = COL_GATE_SSM + D_MODEL
COL_GATE_MOBA = COL_GATE_GLA + D_MODEL
PROJ_WIDTH = COL_GATE_MOBA + D_MODEL
ORIG_GLR = COL_MOBA_Q

S5_PAIRS = SSM_GROUPS // 2
S5_OCTETS = S5_PAIRS // SUBLANES
S5_OCT_CH = SSM_WIDTH // S5_OCTETS
S5_TILE_PAIRS = LANES // (2 * SSM_GROUP_SIZE)
S5_PITCH = S5_PAIRS + 4

NEG_BIG = -1e30
LOG2_E = float(np.log2(np.e))
BF16_ROWS = 2 * SUBLANES


def _cparams(sem):
    return pltpu.CompilerParams(dimension_semantics=sem, vmem_limit_bytes=VMEM_LIMIT)


def _const_spec(shape, index_map):
    return pl.BlockSpec(shape, index_map, pipeline_mode=pl.Buffered(1))


def _sigmoid(x):
    return 0.5 * jnp.tanh(0.5 * x) + 0.5


def _silu(x):
    return x * _sigmoid(x)


def _gelu_tanh(x):
    c = np.float32(np.sqrt(2.0 / np.pi))
    return 0.5 * x * (1.0 + jnp.tanh(c * (x + np.float32(0.044715) * (x * x * x))))


def _rms_scale(x):
    return lax.rsqrt(jnp.mean(x * x, axis=-1, keepdims=True) + RMS_EPS)


def _rmsnorm_kernel(x_ref, g_ref, o_ref):
    x = x_ref[...]
    o_ref[...] = (x * _rms_scale(x) * g_ref[...]).astype(o_ref.dtype)


def rmsnorm(x, g, layer, *, tm):
    m, d = x.shape
    return pl.pallas_call(
        _rmsnorm_kernel,
        out_shape=jax.ShapeDtypeStruct((m, d), BF16),
        grid=(m // tm,),
        in_specs=[pl.BlockSpec((tm, d), lambda i: (i, 0)),
                  _const_spec((None, 1, d), lambda i: (layer, 0, 0))],
        out_specs=pl.BlockSpec((tm, d), lambda i: (i, 0)),
        compiler_params=_cparams(("parallel",)),
        name="rmsnorm",
    )(x, g)


_NT_DIMS = (((1,), (1,)), ((), ()))


def _in_proj_kernel(xn_ref, wa_ref, wb_ref, o_ref, wbf_ref, *, n_aligned):
    j = pl.program_id(0)
    first = pl.program_id(1) == 0
    tn = wa_ref.shape[0]
    sh = GLA_GATE_RANK

    @pl.when(jnp.logical_and(first, j < n_aligned))
    def _():
        wbf_ref[...] = wa_ref[...].astype(BF16)

    @pl.when(jnp.logical_and(first, j >= n_aligned))
    def _():
        wbf_ref[0:tn - sh, :] = wa_ref[sh:tn, :].astype(BF16)
        wbf_ref[tn - sh:tn, :] = wb_ref[...].astype(BF16)

    o_ref[...] = lax.dot_general(xn_ref[...], wbf_ref[...], _NT_DIMS,
                                 preferred_element_type=F32).astype(o_ref.dtype)


def in_proj(xn, w_in_t, layer, *, tm, tn):
    m, d = xn.shape
    sh = GLA_GATE_RANK
    assert ORIG_GLR % tn == 0 and PROJ_WIDTH % tn == 0 and tn % sh == 0
    kern = functools.partial(_in_proj_kernel, n_aligned=ORIG_GLR // tn)
    return pl.pallas_call(
        kern,
        out_shape=jax.ShapeDtypeStruct((m, PROJ_WIDTH), BF16),
        grid=(PROJ_WIDTH // tn, m // tm),
        in_specs=[
            pl.BlockSpec((tm, d), lambda j, i: (i, 0)),
            pl.BlockSpec((None, tn, d), lambda j, i: (layer, j, 0)),
            pl.BlockSpec((None, sh, d), lambda j, i: (layer, (j + 1) * (tn // sh), 0)),
        ],
        out_specs=pl.BlockSpec((tm, tn), lambda j, i: (i, j)),
        scratch_shapes=[pltpu.VMEM((tn, d), BF16)],
        compiler_params=_cparams(("arbitrary", "arbitrary")),
        name="in_proj",
    )(xn, w_in_t, w_in_t)


def _s5_weights_kernel(lre_ref, lim_ref, ldt_ref, bre_ref, bim_ref, cre_ref, cim_ref, rep_ref, rept_ref,
                       win_ref, wout_ref, are_ref, aim_ref):
    lre = lre_ref[...]
    lim = lim_ref[...]
    dt = jnp.exp(ldt_ref[...])
    mag = jnp.exp(lre * dt)
    ang = lim * dt
    a_re = mag * jnp.cos(ang)
    a_im = mag * jnp.sin(ang)
    den = lre * lre + lim * lim
    n_re = a_re - 1.0
    n_im = a_im
    z_re = (n_re * lre + n_im * lim) / den
    z_im = (n_im * lre - n_re * lim) / den
    b_re = bre_ref[...]
    b_im = bim_ref[...]
    are_ref[...] = a_re
    aim_ref[...] = a_im
    bb_re = (z_re * b_re - z_im * b_im).astype(BF16)
    bb_im = (z_re * b_im + z_im * b_re).astype(BF16)

    def own_and_re(s, ch):
        pair = jnp.right_shift(s, 8)
        half = jnp.bitwise_and(jnp.right_shift(s, 6), 1)
        own = jnp.right_shift(ch, 4) == 2 * pair + half
        return own, jnp.bitwise_and(jnp.right_shift(s, 7), 1) == 0

    shape = win_ref.shape
    own, is_re = own_and_re(lax.broadcasted_iota(jnp.int32, shape, 1), lax.broadcasted_iota(jnp.int32, shape, 0))
    t_re = jnp.dot(bb_re, rep_ref[...], preferred_element_type=F32)
    t_im = jnp.dot(bb_im, rep_ref[...], preferred_element_type=F32)
    win_ref[...] = jnp.where(own, jnp.where(is_re, t_re, t_im), 0.0).astype(win_ref.dtype)

    shape = wout_ref.shape
    own, is_re = own_and_re(lax.broadcasted_iota(jnp.int32, shape, 0), lax.broadcasted_iota(jnp.int32, shape, 1))
    t_re = lax.dot_general(rept_ref[...], cre_ref[...].astype(BF16), _NT_DIMS, preferred_element_type=F32)
    t_im = lax.dot_general(rept_ref[...], cim_ref[...].astype(BF16), _NT_DIMS, preferred_element_type=F32)
    wout_ref[...] = jnp.where(own, jnp.where(is_re, t_re, -t_im), 0.0).astype(wout_ref.dtype)


def s5_weights(lam_re, lam_im, log_dt, b_re, b_im, c_re, c_im):
    depth, g, p = lam_re.shape
    s = b_re.shape[-1]
    gpo = g // S5_OCTETS
    blk_rows = gpo * s
    rows = depth * g * s
    rep_rows = lambda t: jnp.broadcast_to(t[:, :, None, :], (depth, g, s, p)).reshape(rows, p)
    ldt = jnp.broadcast_to(log_dt[:, :, None, None], (depth, g, s, p)).reshape(rows, p)
    b_gcp = lambda t: jnp.swapaxes(t, 2, 3).reshape(rows, p)
    span = SUBLANES * 2 * LANES
    rep = jnp.asarray((np.arange(span)[None, :] % p) == np.arange(p)[:, None], dtype=BF16)
    blk = pl.BlockSpec((blk_rows, p), lambda l, o: (l * S5_OCTETS + o, 0))
    w_in, w_out, a_re, a_im = pl.pallas_call(
        _s5_weights_kernel,
        out_shape=(jax.ShapeDtypeStruct((depth, S5_OCTETS, blk_rows, span), BF16),
                   jax.ShapeDtypeStruct((depth, S5_OCTETS, span, blk_rows), BF16),
                   jax.ShapeDtypeStruct((rows, p), F32), jax.ShapeDtypeStruct((rows, p), F32)),
        grid=(depth, S5_OCTETS),
        in_specs=[blk] * 7 + [_const_spec((p, span), lambda l, o: (0, 0)),
                              _const_spec((span, p), lambda l, o: (0, 0))],
        out_specs=(pl.BlockSpec((None, None, blk_rows, span), lambda l, o: (l, o, 0, 0)),
                   pl.BlockSpec((None, None, span, blk_rows), lambda l, o: (l, o, 0, 0)),
                   blk, blk),
        compiler_params=_cparams(("parallel", "parallel")),
        name="s5_weights",
    )(rep_rows(lam_re), rep_rows(lam_im), ldt, b_gcp(b_re), b_gcp(b_im),
      c_re.reshape(rows, p), c_im.reshape(rows, p), rep, rep.T)
    tab = lambda a: a.reshape(depth, g, s, p)[:, :, 0, :].reshape(depth, S5_PAIRS, LANES)
    return w_in, w_out, tab(a_re), tab(a_im)


def _s5_kernel(u_ref, win_ref, wout_ref, are_ref, aim_ref, d_ref, wglu_ref, bglu_ref, o_ref,
               bur_ref, bui_ref, sre_ref, sim_ref, y_ref, *, tc):
    @pl.when(pl.program_id(1) == 0)
    def _():
        sre_ref[...] = jnp.zeros_like(sre_ref)
        sim_ref[...] = jnp.zeros_like(sim_ref)

    half = tc // 2
    pair_w = 2 * LANES
    groups = S5_PAIRS // 2
    per = half // groups

    def in_map(j, h):
        o, jl = divmod(j, SUBLANES)
        t = jl // S5_TILE_PAIRS
        ch0 = o * S5_OCT_CH + t * LANES
        uo = u_ref[h * half:(h + 1) * half, ch0:ch0 + LANES]
        res = jnp.dot(uo, win_ref[o, t * LANES:(t + 1) * LANES, jl * pair_w:(jl + 1) * pair_w],
                      preferred_element_type=F32)
        base = h * half * S5_PITCH + j
        bur_ref[pl.ds(base, half, stride=S5_PITCH), :] = res[:, :LANES]
        bui_ref[pl.ds(base, half, stride=S5_PITCH), :] = res[:, LANES:]

    def out_map(j, h):
        o, jl = divmod(j, SUBLANES)
        t = jl // S5_TILE_PAIRS
        base = h * half * S5_PITCH + j
        xr_j = bur_ref[pl.ds(base, half, stride=S5_PITCH), :]
        xi_j = bui_ref[pl.ds(base, half, stride=S5_PITCH), :]
        lhs = jnp.concatenate([xr_j, xi_j], axis=1).astype(BF16)
        return jnp.dot(lhs, wout_ref[o, jl * pair_w:(jl + 1) * pair_w, t * LANES:(t + 1) * LANES],
                       preferred_element_type=F32)

    def accumulate(acc, j, part, rows):
        acc = part if j % S5_TILE_PAIRS == 0 else acc + part
        if j % S5_TILE_PAIRS == S5_TILE_PAIRS - 1:
            ch0 = (j // S5_TILE_PAIRS) * LANES
            y_ref[rows, ch0:ch0 + LANES] = acc
        return acc

    a_re = [are_ref[q * SUBLANES:(q + 1) * SUBLANES, :] for q in range(S5_OCTETS)]
    a_im = [aim_ref[q * SUBLANES:(q + 1) * SUBLANES, :] for q in range(S5_OCTETS)]

    def scan(h, t0, carry):
        xr, xi = carry
        for t in range(t0, t0 + per):
            nr, ni = [], []
            for q in range(S5_OCTETS):
                row = (h * half + t) * S5_PITCH + q * SUBLANES
                br = bur_ref[row:row + SUBLANES, :]
                bi = bui_ref[row:row + SUBLANES, :]
                r = a_re[q] * xr[q] - a_im[q] * xi[q] + br
                i = a_re[q] * xi[q] + a_im[q] * xr[q] + bi
                bur_ref[row:row + SUBLANES, :] = r
                bui_ref[row:row + SUBLANES, :] = i
                nr.append(r)
                ni.append(i)
            xr, xi = nr, ni
        return xr, xi

    for j in range(S5_PAIRS):
        in_map(j, 0)
    carry = ([sre_ref[q * SUBLANES:(q + 1) * SUBLANES, :] for q in range(S5_OCTETS)],
             [sim_ref[q * SUBLANES:(q + 1) * SUBLANES, :] for q in range(S5_OCTETS)])
    for g in range(groups):
        carry = scan(0, g * per, carry)
        in_map(2 * g, 1)
        in_map(2 * g + 1, 1)
    acc = None
    for g in range(groups):
        carry = scan(1, g * per, carry)
        for j in (2 * g, 2 * g + 1):
            acc = accumulate(acc, j, out_map(j, 0), slice(0, half))
    for q in range(S5_OCTETS):
        sre_ref[q * SUBLANES:(q + 1) * SUBLANES, :] = carry[0][q]
        sim_ref[q * SUBLANES:(q + 1) * SUBLANES, :] = carry[1][q]
    for j in range(S5_PAIRS):
        acc = accumulate(acc, j, out_map(j, 1), slice(half, tc))

    y = y_ref[...] + d_ref[...] * u_ref[...].astype(F32)
    z = _gelu_tanh(y)
    gate = jnp.dot(z.astype(BF16), wglu_ref[...].astype(BF16), preferred_element_type=F32) + bglu_ref[...]
    o_ref[...] = (z * _sigmoid(gate)).astype(o_ref.dtype)


def s5_mixer(proj, w_in_blk, w_out_blk, a_re, a_im, d_skip, w_glu, b_glu, layer, *, batch, seq, tc):
    nt = seq // tc
    w = SSM_WIDTH
    kern = functools.partial(_s5_kernel, tc=tc)
    return pl.pallas_call(
        kern,
        out_shape=jax.ShapeDtypeStruct((batch * seq, w), BF16),
        grid=(batch, nt),
        in_specs=[
            pl.BlockSpec((tc, w), lambda b, c: (b * nt + c, COL_SSM // w)),
            _const_spec((None,) + w_in_blk.shape[1:], lambda b, c: (layer, 0, 0, 0)),
            _const_spec((None,) + w_out_blk.shape[1:], lambda b, c: (layer, 0, 0, 0)),
            _const_spec((None, S5_PAIRS, LANES), lambda b, c: (layer, 0, 0)),
            _const_spec((None, S5_PAIRS, LANES), lambda b, c: (layer, 0, 0)),
            _const_spec((None, 1, w), lambda b, c: (layer, 0, 0)),
            _const_spec((None, w, w), lambda b, c: (layer, 0, 0)),
            _const_spec((None, 1, w), lambda b, c: (layer, 0, 0)),
        ],
        out_specs=pl.BlockSpec((tc, w), lambda b, c: (b * nt + c, 0)),
        scratch_shapes=[
            pltpu.VMEM((tc * S5_PITCH, LANES), F32),
            pltpu.VMEM((tc * S5_PITCH, LANES), F32),
            pltpu.VMEM((S5_PAIRS, LANES), F32),
            pltpu.VMEM((S5_PAIRS, LANES), F32),
            pltpu.VMEM((tc, w), F32),
        ],
        compiler_params=_cparams(("parallel", "arbitrary")),
        name="s5_mixer",
    )(proj, w_in_blk, w_out_blk, a_re, a_im, d_skip, w_glu, b_glu)


def _split3(x):
    hi = x.astype(BF16)
    r1 = x - hi.astype(F32)
    mid = r1.astype(BF16)
    lo = (r1 - mid.astype(F32)).astype(BF16)
    return hi, mid, lo


def _gla_kernel(q_ref, k_ref, v_ref, r_ref, xn_ref, wglr_ref, wg_ref, bg_ref, ng_ref, tri_ref, o_ref, st_ref,
                bcum_ref, *, tg):
    c = GLA_CHUNK

    @pl.when(pl.program_id(1) == 0)
    def _():
        st_ref[...] = jnp.zeros_like(st_ref)
        wl = wglr_ref[...].astype(BF16)
        wl = jnp.concatenate([wl, jnp.zeros((LANES - GLA_GATE_RANK, wl.shape[1]), BF16)], axis=0)
        tri = tri_ref[...]
        for t0 in range(0, xn_ref.shape[0], tg):
            ts = slice(t0, t0 + tg)
            glr = lax.dot_general(xn_ref[ts, :], wl, _NT_DIMS, preferred_element_type=F32).astype(BF16)
            glog = jnp.dot(glr, wg_ref[...], preferred_element_type=F32) + bg_ref[...]
            g = (jnp.minimum(glog, 0.0) - jnp.log(1.0 + jnp.exp(-jnp.abs(glog)))) * (1.0 / GLA_GATE_NORM)
            acc = None
            for piece in _split3(g):
                part = jnp.dot(tri, piece, preferred_element_type=F32)
                acc = part if acc is None else acc + part
            bcum_ref[ts, :] = acc

    bcum = bcum_ref[pl.ds(pl.multiple_of(pl.program_id(1) * tg, tg), tg), :]

    row = lax.broadcasted_iota(jnp.int32, (c, c), 0)
    col = lax.broadcasted_iota(jnp.int32, (c, c), 1)
    causal = col <= row
    scale = np.float32(GLA_DK ** -0.5)
    ng = ng_ref[...]
    for n in range(tg // c):
        rs = slice(n * c, (n + 1) * c)
        for h in range(GLA_HEADS):
            ks = slice(h * GLA_DK, (h + 1) * GLA_DK)
            vs = slice(h * GLA_DV, (h + 1) * GLA_DV)
            b = bcum[rs, ks]
            bl = b[c - 1:c, :]
            qh = q_ref[rs, ks].astype(F32) * scale
            kh = k_ref[rs, ks].astype(F32)
            vh = v_ref[rs, vs]
            q_dec = (qh * jnp.exp(b)).astype(BF16)
            k_inv = (kh * jnp.exp(-b)).astype(BF16)
            k_dec = (kh * jnp.exp(bl - b)).astype(BF16)
            attn = lax.dot_general(q_dec, k_inv, (((1,), (1,)), ((), ())), preferred_element_type=F32)
            attn = jnp.where(causal, attn, 0.0).astype(BF16)
            st = st_ref[h]
            o = (jnp.dot(attn, vh, preferred_element_type=F32)
                 + lax.dot_general(q_dec, st.astype(BF16), (((1,), (1,)), ((), ())),
                                   preferred_element_type=F32))
            kvt = lax.dot_general(vh, k_dec, (((0,), (0,)), ((), ())), preferred_element_type=F32)
            st_ref[h] = jnp.exp(bl) * st + kvt
            o = o * _rms_scale(o) * ng
            o_ref[rs, vs] = (o * _silu(r_ref[rs, vs].astype(F32))).astype(o_ref.dtype)


def gla_mixer(proj, xn, w_in_t, w_gate2_pad, b_gate2, norm_g, tri, layer, *, batch, seq, tg):
    nt = seq // tg
    kw, vw = GLA_KEY_WIDTH, GLA_VALUE_WIDTH
    d = xn.shape[1]
    sh = GLA_GATE_RANK
    kern = functools.partial(_gla_kernel, tg=tg)
    return pl.pallas_call(
        kern,
        out_shape=jax.ShapeDtypeStruct((batch * seq, vw), BF16),
        grid=(batch, nt),
        in_specs=[
            pl.BlockSpec((tg, kw), lambda b, c: (b * nt + c, COL_GLA_Q // kw)),
            pl.BlockSpec((tg, kw), lambda b, c: (b * nt + c, COL_GLA_K // kw)),
            pl.BlockSpec((tg, vw), lambda b, c: (b * nt + c, COL_GLA_V // vw)),
            pl.BlockSpec((tg, vw), lambda b, c: (b * nt + c, COL_GLA_R // vw)),
            pl.BlockSpec((seq, d), lambda b, c: (b, 0)),
            _const_spec((None, sh, d), lambda b, c: (layer, ORIG_GLR // sh, 0)),
            _const_spec((None, LANES, kw), lambda b, c: (layer, 0, 0)),
            _const_spec((None, 1, kw), lambda b, c: (layer, 0, 0)),
            _const_spec((None, 1, GLA_DV), lambda b, c: (layer, 0, 0)),
            _const_spec((tg, tg), lambda b, c: (0, 0)),
        ],
        out_specs=pl.BlockSpec((tg, vw), lambda b, c: (b * nt + c, 0)),
        scratch_shapes=[pltpu.VMEM((GLA_HEADS, GLA_DV, GLA_DK), F32), pltpu.VMEM((seq, kw), F32)],
        compiler_params=_cparams(("parallel", "arbitrary")),
        name="gla_mixer",
    )(proj, proj, proj, proj, xn, w_in_t, w_gate2_pad, b_gate2, norm_g, tri)


def _moba_kernel(q_ref, k_ref, v_ref, qg_ref, kg_ref, o_ref, kn_ref, vt_ref, km_ref, st_ref, pt_ref, *, nb):
    bs = MOBA_BLOCK
    kg = kg_ref[...]
    for jb in range(nb):
        kb = k_ref[jb * bs:(jb + 1) * bs, :].astype(F32)
        kn = kb * _rms_scale(kb) * kg
        kn_ref[jb * bs:(jb + 1) * bs, :] = kn.astype(BF16)
        km_ref[jb:jb + 1, :] = jnp.mean(kn, axis=0, keepdims=True)
    hd = MOBA_HEAD_DIM
    vt_ref[0:hd, :] = v_ref[...].astype(F32).T.astype(BF16)
    ones_row = lax.broadcasted_iota(jnp.int32, (vt_ref.shape[0] - hd, vt_ref.shape[1]), 0) == 0
    vt_ref[hd:, :] = jnp.where(ones_row, 1.0, 0.0).astype(BF16)

    qg = qg_ref[...]
    km = km_ref[...]
    blk = lax.broadcasted_iota(jnp.int32, (nb, bs), 0)
    krow = lax.broadcasted_iota(jnp.int32, (bs, bs), 0)
    qcol = lax.broadcasted_iota(jnp.int32, (bs, bs), 1)
    for i in range(nb):
        q = q_ref[i * bs:(i + 1) * bs, :].astype(F32)
        qn = (q * _rms_scale(q) * qg) * np.float32(MOBA_HEAD_DIM ** -0.5)
        qb = (qn * LOG2_E).astype(BF16)
        bias_t = None
        if i > MOBA_TOPK:
            sbt = lax.dot_general(km, qn, (((1,), (1,)), ((), ())),
                                  precision=lax.Precision.HIGHEST, preferred_element_type=F32)
            cnt = jnp.zeros((nb, bs), F32)
            for jp in range(i):
                sp = sbt[jp:jp + 1, :]
                beats = jnp.where(sp > sbt, 1.0, jnp.where(sp == sbt, jnp.where(blk > jp, 1.0, 0.0), 0.0))
                cnt = cnt + beats
            bias_t = jnp.where(cnt < MOBA_TOPK, 0.0, NEG_BIG)
        nk = (i + 1) * bs
        s_buf = st_ref.at[i % 2]
        p_buf = pt_ref.at[i % 2]
        mx8 = None
        for j in range(i + 1):
            ks = slice(j * bs, (j + 1) * bs)
            sj = lax.dot_general(kn_ref[ks, :], qb, _NT_DIMS, preferred_element_type=F32)
            if j == i:
                sj = jnp.where(krow <= qcol, sj, NEG_BIG)
            elif bias_t is not None:
                sj = sj + bias_t[j:j + 1, :]
            s_buf[ks, :] = sj
            pm = jnp.max(sj.reshape(bs // SUBLANES, SUBLANES, bs), axis=0)
            mx8 = pm if mx8 is None else jnp.maximum(mx8, pm)
        m = jnp.max(mx8, axis=0, keepdims=True)
        for j in range(i + 1):
            ks = slice(j * bs, (j + 1) * bs)
            p_buf[ks, :] = jnp.exp2(s_buf[ks, :] - m).astype(BF16)
        ot = jnp.dot(vt_ref[:, 0:nk], p_buf[0:nk, :], preferred_element_type=F32)
        o_ref[i * bs:(i + 1) * bs, :] = (ot[0:hd, :] / ot[hd:hd + 1, :]).T.astype(o_ref.dtype)


def moba_mixer(proj, q_g, k_g, layer, *, batch, seq):
    bs, hd, nh = MOBA_BLOCK, MOBA_HEAD_DIM, MOBA_HEADS
    nb = seq // bs
    kern = functools.partial(_moba_kernel, nb=nb)
    blk = lambda col: pl.BlockSpec((seq, hd), lambda b, h: (b, col // hd + h))
    return pl.pallas_call(
        kern,
        out_shape=jax.ShapeDtypeStruct((batch * seq, MOBA_WIDTH), BF16),
        grid=(batch, nh),
        in_specs=[
            blk(COL_MOBA_Q), blk(COL_MOBA_K), blk(COL_MOBA_V),
            _const_spec((None, 1, hd), lambda b, h: (layer, 0, 0)),
            _const_spec((None, 1, hd), lambda b, h: (layer, 0, 0)),
        ],
        out_specs=pl.BlockSpec((seq, hd), lambda b, h: (b, h)),
        scratch_shapes=[pltpu.VMEM((seq, hd), BF16), pltpu.VMEM((hd + BF16_ROWS, seq), BF16),
                        pltpu.VMEM((nb, hd), F32),
                        pltpu.VMEM((2, seq, bs), F32), pltpu.VMEM((2, seq, bs), BF16)],
        compiler_params=_cparams(("parallel", "parallel")),
        name="moba_mixer",
    )(proj, proj, proj, q_g, k_g)


_WEIGHT_OUTER = ("arbitrary", "arbitrary")


def _merge_kernel(ys_ref, yg_ref, ym_ref, gs_ref, gg_ref, gm_ref, ws_ref, wg_ref, wm_ref, o_ref,
                  wsb_ref, wgb_ref, wmb_ref):
    @pl.when(pl.program_id(1) == 0)
    def _():
        wsb_ref[...] = ws_ref[...].astype(BF16)
        wgb_ref[...] = wg_ref[...].astype(BF16)
        wmb_ref[...] = wm_ref[...].astype(BF16)

    def branch(g_ref, y_ref, w_ref):
        return _sigmoid(g_ref[...].astype(F32)) * jnp.dot(y_ref[...], w_ref[...], preferred_element_type=F32)

    acc = branch(gs_ref, ys_ref, wsb_ref) + branch(gg_ref, yg_ref, wgb_ref) + branch(gm_ref, ym_ref, wmb_ref)
    o_ref[...] = acc.astype(o_ref.dtype)


def merge_branches(y_ssm, y_gla, y_moba, proj, w_s, w_g, w_m, layer, *, tm, tn):
    m, kdim = y_ssm.shape
    n = w_s.shape[-1]
    yspec = pl.BlockSpec((tm, kdim), lambda j, i: (i, 0))
    gspec = lambda col: pl.BlockSpec((tm, tn), lambda j, i: (i, col // tn + j))
    wspec = pl.BlockSpec((None, kdim, tn), lambda j, i: (layer, 0, j))
    return pl.pallas_call(
        _merge_kernel,
        out_shape=jax.ShapeDtypeStruct((m, n), BF16),
        grid=(n // tn, m // tm),
        in_specs=[yspec, yspec, yspec, gspec(COL_GATE_SSM), gspec(COL_GATE_GLA), gspec(COL_GATE_MOBA),
                  wspec, wspec, wspec],
        out_specs=pl.BlockSpec((tm, tn), lambda j, i: (i, j)),
        scratch_shapes=[pltpu.VMEM((kdim, tn), BF16)] * 3,
        compiler_params=_cparams(_WEIGHT_OUTER),
        name="merge",
    )(y_ssm, y_gla, y_moba, proj, proj, proj, w_s, w_g, w_m)


def _out_proj_kernel(a_ref, w_ref, x_ref, g_ref, o_ref, h_ref, wbf_ref):
    @pl.when(pl.program_id(0) == 0)
    def _():
        wbf_ref[...] = w_ref[...].astype(BF16)

    xn = x_ref[...] + jnp.dot(a_ref[...], wbf_ref[...], preferred_element_type=F32)
    o_ref[...] = xn
    h_ref[...] = (xn * _rms_scale(xn) * g_ref[...]).astype(h_ref.dtype)


def out_proj_norm(a, w_all, x, g, layer, *, tm):
    m, kdim = a.shape
    n = w_all.shape[-1]
    row = lambda width: pl.BlockSpec((tm, width), lambda i: (i, 0))
    return pl.pallas_call(
        _out_proj_kernel,
        out_shape=(jax.ShapeDtypeStruct((m, n), F32), jax.ShapeDtypeStruct((m, n), BF16)),
        grid=(m // tm,),
        in_specs=[row(kdim), _const_spec((None, kdim, n), lambda i: (layer, 0, 0)), row(n),
                  _const_spec((None, 1, n), lambda i: (layer, 0, 0))],
        out_specs=(row(n), row(n)),
        scratch_shapes=[pltpu.VMEM((kdim, n), BF16)],
        compiler_params=_cparams(("arbitrary",)),
        name="out_proj",
    )(a, w_all, x, g)


def _resid_matmul_kernel(a_ref, w_ref, x_ref, o_ref, wbf_ref):
    @pl.when(pl.program_id(1) == 0)
    def _():
        wbf_ref[...] = w_ref[...].astype(BF16)

    o_ref[...] = x_ref[...] + jnp.dot(a_ref[...], wbf_ref[...], preferred_element_type=F32)


def resid_matmul(a, w_all, x, layer, *, tm, tn):
    m, kdim = a.shape
    n = w_all.shape[-1]
    return pl.pallas_call(
        _resid_matmul_kernel,
        out_shape=jax.ShapeDtypeStruct((m, n), F32),
        grid=(n // tn, m // tm),
        in_specs=[
            pl.BlockSpec((tm, kdim), lambda j, i: (i, 0)),
            _const_spec((None, kdim, tn), lambda j, i: (layer, 0, j)),
            pl.BlockSpec((tm, tn), lambda j, i: (i, j)),
        ],
        out_specs=pl.BlockSpec((tm, tn), lambda j, i: (i, j)),
        scratch_shapes=[pltpu.VMEM((kdim, tn), BF16)],
        compiler_params=_cparams(_WEIGHT_OUTER),
        name="resid_matmul",
    )(a, w_all, x)


def _ffn_up_kernel(h_ref, wg_ref, wu_ref, o_ref, wgb_ref, wub_ref):
    @pl.when(pl.program_id(1) == 0)
    def _():
        wgb_ref[...] = wg_ref[...].astype(BF16)
        wub_ref[...] = wu_ref[...].astype(BF16)

    h = h_ref[...]
    gate = jnp.dot(h, wgb_ref[...], preferred_element_type=F32)
    up = jnp.dot(h, wub_ref[...], preferred_element_type=F32)
    o_ref[...] = (_silu(gate) * up).astype(o_ref.dtype)


def ffn_up(h, w_gate, w_up, layer, *, tm, tn):
    m, d = h.shape
    n = w_gate.shape[-1]
    wspec = pl.BlockSpec((None, d, tn), lambda j, i: (layer, 0, j))
    return pl.pallas_call(
        _ffn_up_kernel,
        out_shape=jax.ShapeDtypeStruct((m, n), BF16),
        grid=(n // tn, m // tm),
        in_specs=[pl.BlockSpec((tm, d), lambda j, i: (i, 0)), wspec, wspec],
        out_specs=pl.BlockSpec((tm, tn), lambda j, i: (i, j)),
        scratch_shapes=[pltpu.VMEM((d, tn), BF16)] * 2,
        compiler_params=_cparams(_WEIGHT_OUTER),
        name="ffn_up",
    )(h, w_gate, w_up)


def _gla_tri(tg):
    r = np.arange(tg)
    same_chunk = (r[:, None] // GLA_CHUNK) == (r[None, :] // GLA_CHUNK)
    return jnp.asarray(np.where(same_chunk & (r[None, :] <= r[:, None]), 1.0, 0.0), dtype=BF16)


def kernel(x, norm1_g, w_in, ssm_lambda_re, ssm_lambda_im, ssm_log_dt, ssm_b_re, ssm_b_im, ssm_c_re, ssm_c_im, ssm_d, ssm_w_glu, ssm_b_glu, gla_w_gate2, gla_b_gate2, gla_norm_g, moba_q_norm_g, moba_k_norm_g, w_branch_ssm, w_branch_gla, w_branch_moba, w_out, norm2_g, ffn_w_gate, ffn_w_up, ffn_w_down):
    batch, seq, d = x.shape
    depth = w_in.shape[0]
    m = batch * seq

    w_in_t = jnp.swapaxes(w_in, 1, 2)
    wg2_pad = jnp.pad(gla_w_gate2, ((0, 0), (0, LANES - GLA_GATE_RANK), (0, 0))).astype(BF16)
    s5_win, s5_wout, a_re_t, a_im_t = s5_weights(ssm_lambda_re, ssm_lambda_im, ssm_log_dt, ssm_b_re, ssm_b_im,
                                                 ssm_c_re, ssm_c_im)
    row3 = lambda t: t[:, None, :]
    n1, n2 = row3(norm1_g), row3(norm2_g)
    ssm_d3, b_glu3 = row3(ssm_d), row3(ssm_b_glu)
    bg2, gng = row3(gla_b_gate2), row3(gla_norm_g)
    mqg, mkg = row3(moba_q_norm_g), row3(moba_k_norm_g)

    tg = 256
    tri = _gla_tri(tg)
    xf = x.reshape(m, d)
    for l in range(depth):
        xn = rmsnorm(xf, n1, l, tm=512)
        proj = in_proj(xn, w_in_t, l, tm=2048, tn=1024)
        y_ssm = s5_mixer(proj, s5_win, s5_wout, a_re_t, a_im_t, ssm_d3, ssm_w_glu, b_glu3, l,
                         batch=batch, seq=seq, tc=512)
        y_gla = gla_mixer(proj, xn, w_in_t, wg2_pad, bg2, gng, tri, l, batch=batch, seq=seq, tg=tg)
        y_moba = moba_mixer(proj, mqg, mkg, l, batch=batch, seq=seq)
        merged = merge_branches(y_ssm, y_gla, y_moba, proj, w_branch_ssm, w_branch_gla, w_branch_moba, l,
                                tm=512, tn=1024)
        xf, hn = out_proj_norm(merged, w_out, xf, n2, l, tm=512)
        act = ffn_up(hn, ffn_w_gate, ffn_w_up, l, tm=1024, tn=512)
        xf = resid_matmul(act, ffn_w_down, xf, l, tm=512, tn=1024)
    return xf.reshape(batch, seq, d)
```

```python
import functools

import jax
import jax.numpy as jnp
import numpy as np
from jax import lax
from jax.experimental import pallas as pl
from jax.experimental.pallas import tpu as pltpu

F32 = jnp.float32
BF16 = jnp.bfloat16

D_MODEL = 2048
DEPTH = 4
RMS_EPS = 1e-6
SSM_WIDTH = D_MODEL // 2
SSM_GROUP_SIZE = 16
SSM_GROUPS = SSM_WIDTH // SSM_GROUP_SIZE
SSM_STATE = 64
GLA_HEADS = 4
GLA_KEY_WIDTH = D_MODEL // 4
GLA_VALUE_WIDTH = D_MODEL // 2
GLA_DK = GLA_KEY_WIDTH // GLA_HEADS
GLA_DV = GLA_VALUE_WIDTH // GLA_HEADS
GLA_GATE_RANK = 16
GLA_GATE_NORM = 16.0
GLA_CHUNK = 64
MOBA_WIDTH = D_MODEL // 2
MOBA_HEAD_DIM = 128
MOBA_HEADS = MOBA_WIDTH // MOBA_HEAD_DIM
MOBA_BLOCK = 256
MOBA_TOPK = 3
FFN_HIDDEN = -(-8 * D_MODEL // (3 * 256)) * 256

LANES = 128
SUBLANES = 8
VMEM_LIMIT = 56 * 1024 * 1024

TILES = dict(
    rmsnorm=dict(tm=512),
    in_proj=dict(tm=2048, tn=1024),
    merge=dict(tm=512, tn=1024),
    out_proj=dict(tm=512),
    ffn_up=dict(tm=1024, tn=512),
    ffn_down=dict(tm=512, tn=1024),
    s5=dict(tc=512),
    gla=dict(tg=256),
)

COL_SSM = 0
COL_GLA_Q = COL_SSM + SSM_WIDTH
COL_GLA_K = COL_GLA_Q + GLA_KEY_WIDTH
COL_GLA_V = COL_GLA_K + GLA_KEY_WIDTH
COL_GLA_R = COL_GLA_V + GLA_VALUE_WIDTH
COL_MOBA_Q = COL_GLA_R + GLA_VALUE_WIDTH
COL_MOBA_K = COL_MOBA_Q + MOBA_WIDTH
COL_MOBA_V = COL_MOBA_K + MOBA_WIDTH
COL_GATE_SSM = COL_MOBA_V + MOBA_WIDTH
COL_GATE_GLA = COL_GATE_SSM + D_MODEL
COL_GATE_MOBA = COL_GATE_GLA + D_MODEL
PROJ_WIDTH = COL_GATE_MOBA + D_MODEL
ORIG_GLR = COL_MOBA_Q

S5_PAIRS = SSM_GROUPS // 2
S5_OCTETS = S5_PAIRS // SUBLANES
S5_OCT_CH = SSM_WIDTH // S5_OCTETS
S5_TILE_PAIRS = LANES // (2 * SSM_GROUP_SIZE)
S5_PITCH = S5_PAIRS + 4

NEG_BIG = -1e30
LOG2_E = float(np.log2(np.e))
BF16_ROWS = 2 * SUBLANES


def _cparams(sem):
    return pltpu.CompilerParams(dimension_semantics=sem, vmem_limit_bytes=VMEM_LIMIT)


def _const_spec(shape, index_map):
    return pl.BlockSpec(shape, index_map, pipeline_mode=pl.Buffered(1))


def _sigmoid(x):
    return 0.5 * jnp.tanh(0.5 * x) + 0.5


def _silu(x):
    return x * _sigmoid(x)


def _gelu_tanh(x):
    c = np.float32(np.sqrt(2.0 / np.pi))
    return 0.5 * x * (1.0 + jnp.tanh(c * (x + np.float32(0.044715) * (x * x * x))))


def _rms_scale(x):
    return lax.rsqrt(jnp.mean(x * x, axis=-1, keepdims=True) + RMS_EPS)


def _rmsnorm_kernel(x_ref, g_ref, o_ref):
    x = x_ref[...]
    o_ref[...] = (x * _rms_scale(x) * g_ref[...]).astype(o_ref.dtype)


def rmsnorm(x, g, layer, *, tm):
    m, d = x.shape
    return pl.pallas_call(
        _rmsnorm_kernel,
        out_shape=jax.ShapeDtypeStruct((m, d), BF16),
        grid=(m // tm,),
        in_specs=[pl.BlockSpec((tm, d), lambda i: (i, 0)),
                  _const_spec((None, 1, d), lambda i: (layer, 0, 0))],
        out_specs=pl.BlockSpec((tm, d), lambda i: (i, 0)),
        compiler_params=_cparams(("parallel",)),
        name="rmsnorm",
    )(x, g)


_NT_DIMS = (((1,), (1,)), ((), ()))


def _in_proj_kernel(xn_ref, wa_ref, wb_ref, o_ref, wbf_ref, *, n_aligned):
    j = pl.program_id(0)
    first = pl.program_id(1) == 0
    tn = wa_ref.shape[0]
    sh = GLA_GATE_RANK

    @pl.when(jnp.logical_and(first, j < n_aligned))
    def _():
        wbf_ref[...] = wa_ref[...].astype(BF16)

    @pl.when(jnp.logical_and(first, j >= n_aligned))
    def _():
        wbf_ref[0:tn - sh, :] = wa_ref[sh:tn, :].astype(BF16)
        wbf_ref[tn - sh:tn, :] = wb_ref[...].astype(BF16)

    o_ref[...] = lax.dot_general(xn_ref[...], wbf_ref[...], _NT_DIMS,
                                 preferred_element_type=F32).astype(o_ref.dtype)


def in_proj(xn, w_in_t, layer, *, tm, tn):
    m, d = xn.shape
    sh = GLA_GATE_RANK
    assert ORIG_GLR % tn == 0 and PROJ_WIDTH % tn == 0 and tn % sh == 0
    kern = functools.partial(_in_proj_kernel, n_aligned=ORIG_GLR // tn)
    return pl.pallas_call(
        kern,
        out_shape=jax.ShapeDtypeStruct((m, PROJ_WIDTH), BF16),
        grid=(PROJ_WIDTH // tn, m // tm),
        in_specs=[
            pl.BlockSpec((tm, d), lambda j, i: (i, 0)),
            pl.BlockSpec((None, tn, d), lambda j, i: (layer, j, 0)),
            pl.BlockSpec((None, sh, d), lambda j, i: (layer, (j + 1) * (tn // sh), 0)),
        ],
        out_specs=pl.BlockSpec((tm, tn), lambda j, i: (i, j)),
        scratch_shapes=[pltpu.VMEM((tn, d), BF16)],
        compiler_params=_cparams(("arbitrary", "arbitrary")),
        name="in_proj",
    )(xn, w_in_t, w_in_t)


def _s5_weights_kernel(lre_ref, lim_ref, ldt_ref, bre_ref, bim_ref, cre_ref, cim_ref, rep_ref, rept_ref,
                       win_ref, wout_ref, are_ref, aim_ref):
    lre = lre_ref[...]
    lim = lim_ref[...]
    dt = jnp.exp(ldt_ref[...])
    mag = jnp.exp(lre * dt)
    ang = lim * dt
    a_re = mag * jnp.cos(ang)
    a_im = mag * jnp.sin(ang)
    den = lre * lre + lim * lim
    n_re = a_re - 1.0
    n_im = a_im
    z_re = (n_re * lre + n_im * lim) / den
    z_im = (n_im * lre - n_re * lim) / den
    b_re = bre_ref[...]
    b_im = bim_ref[...]
    are_ref[...] = a_re
    aim_ref[...] = a_im
    bb_re = (z_re * b_re - z_im * b_im).astype(BF16)
    bb_im = (z_re * b_im + z_im * b_re).astype(BF16)

    def own_and_re(s, ch):
        pair = jnp.right_shift(s, 8)
        half = jnp.bitwise_and(jnp.right_shift(s, 6), 1)
        own = jnp.right_shift(ch, 4) == 2 * pair + half
        return own, jnp.bitwise_and(jnp.right_shift(s, 7), 1) == 0

    shape = win_ref.shape
    own, is_re = own_and_re(lax.broadcasted_iota(jnp.int32, shape, 1), lax.broadcasted_iota(jnp.int32, shape, 0))
    t_re = jnp.dot(bb_re, rep_ref[...], preferred_element_type=F32)
    t_im = jnp.dot(bb_im, rep_ref[...], preferred_element_type=F32)
    win_ref[...] = jnp.where(own, jnp.where(is_re, t_re, t_im), 0.0).astype(win_ref.dtype)

    shape = wout_ref.shape
    own, is_re = own_and_re(lax.broadcasted_iota(jnp.int32, shape, 0), lax.broadcasted_iota(jnp.int32, shape, 1))
    t_re = lax.dot_general(rept_ref[...], cre_ref[...].astype(BF16), _NT_DIMS, preferred_element_type=F32)
    t_im = lax.dot_general(rept_ref[...], cim_ref[...].astype(BF16), _NT_DIMS, preferred_element_type=F32)
    wout_ref[...] = jnp.where(own, jnp.where(is_re, t_re, -t_im), 0.0).astype(wout_ref.dtype)


def s5_weights(lam_re, lam_im, log_dt, b_re, b_im, c_re, c_im):
    depth, g, p = lam_re.shape
    s = b_re.shape[-1]
    gpo = g // S5_OCTETS
    blk_rows = gpo * s
    rows = depth * g * s
    rep_rows = lambda t: jnp.broadcast_to(t[:, :, None, :], (depth, g, s, p)).reshape(rows, p)
    ldt = jnp.broadcast_to(log_dt[:, :, None, None], (depth, g, s, p)).reshape(rows, p)
    b_gcp = lambda t: jnp.swapaxes(t, 2, 3).reshape(rows, p)
    span = SUBLANES * 2 * LANES
    rep = jnp.asarray((np.arange(span)[None, :] % p) == np.arange(p)[:, None], dtype=BF16)
    blk = pl.BlockSpec((blk_rows, p), lambda l, o: (l * S5_OCTETS + o, 0))
    w_in, w_out, a_re, a_im = pl.pallas_call(
        _s5_weights_kernel,
        out_shape=(jax.ShapeDtypeStruct((depth, S5_OCTETS, blk_rows, span), BF16),
                   jax.ShapeDtypeStruct((depth, S5_OCTETS, span, blk_rows), BF16),
                   jax.ShapeDtypeStruct((rows, p), F32), jax.ShapeDtypeStruct((rows, p), F32)),
        grid=(depth, S5_OCTETS),
        in_specs=[blk] * 7 + [_const_spec((p, span), lambda l, o: (0, 0)),
                              _const_spec((span, p), lambda l, o: (0, 0))],
        out_specs=(pl.BlockSpec((None, None, blk_rows, span), lambda l, o: (l, o, 0, 0)),
                   pl.BlockSpec((None, None, span, blk_rows), lambda l, o: (l, o, 0, 0)),
                   blk, blk),
        compiler_params=_cparams(("parallel", "parallel")),
        name="s5_weights",
    )(rep_rows(lam_re), rep_rows(lam_im), ldt, b_gcp(b_re), b_gcp(b_im),
      c_re.reshape(rows, p), c_im.reshape(rows, p), rep, rep.T)
    tab = lambda a: a.reshape(depth, g, s, p)[:, :, 0, :].reshape(depth, S5_PAIRS, LANES)
    return w_in, w_out, tab(a_re), tab(a_im)


def _s5_kernel(u_ref, win_ref, wout_ref, are_ref, aim_ref, d_ref, wglu_ref, bglu_ref, o_ref,
               bur_ref, bui_ref, sre_ref, sim_ref, y_ref, *, tc):
    @pl.when(pl.program_id(1) == 0)
    def _():
        sre_ref[...] = jnp.zeros_like(sre_ref)
        sim_ref[...] = jnp.zeros_like(sim_ref)

    half = tc // 2
    pair_w = 2 * LANES
    groups = S5_PAIRS // 2
    per = half // groups

    def in_map(j, h):
        o, jl = divmod(j, SUBLANES)
        t = jl // S5_TILE_PAIRS
        ch0 = o * S5_OCT_CH + t * LANES
        uo = u_ref[h * half:(h + 1) * half, ch0:ch0 + LANES]
        res = jnp.dot(uo, win_ref[o, t * LANES:(t + 1) * LANES, jl * pair_w:(jl + 1) * pair_w],
                      preferred_element_type=F32)
        base = h * half * S5_PITCH + j
        bur_ref[pl.ds(base, half, stride=S5_PITCH), :] = res[:, :LANES]
        bui_ref[pl.ds(base, half, stride=S5_PITCH), :] = res[:, LANES:]

    def out_map(j, h):
        o, jl = divmod(j, SUBLANES)
        t = jl // S5_TILE_PAIRS
        base = h * half * S5_PITCH + j
        xr_j = bur_ref[pl.ds(base, half, stride=S5_PITCH), :]
        xi_j = bui_ref[pl.ds(base, half, stride=S5_PITCH), :]
        lhs = jnp.concatenate([xr_j, xi_j], axis=1).astype(BF16)
        return jnp.dot(lhs, wout_ref[o, jl * pair_w:(jl + 1) * pair_w, t * LANES:(t + 1) * LANES],
                       preferred_element_type=F32)

    def accumulate(acc, j, part, rows):
        acc = part if j % S5_TILE_PAIRS == 0 else acc + part
        if j % S5_TILE_PAIRS == S5_TILE_PAIRS - 1:
            ch0 = (j // S5_TILE_PAIRS) * LANES
            y_ref[rows, ch0:ch0 + LANES] = acc
        return acc

    a_re = [are_ref[q * SUBLANES:(q + 1) * SUBLANES, :] for q in range(S5_OCTETS)]
    a_im = [aim_ref[q * SUBLANES:(q + 1) * SUBLANES, :] for q in range(S5_OCTETS)]

    def scan(h, t0, carry):
        xr, xi = carry
        for t in range(t0, t0 + per):
            nr, ni = [], []
            for q in range(S5_OCTETS):
                row = (h * half + t) * S5_PITCH + q * SUBLANES
                br = bur_ref[row:row + SUBLANES, :]
                bi = bui_ref[row:row + SUBLANES, :]
                r = a_re[q] * xr[q] - a_im[q] * xi[q] + br
                i = a_re[q] * xi[q] + a_im[q] * xr[q] + bi
                bur_ref[row:row + SUBLANES, :] = r
                bui_ref[row:row + SUBLANES, :] = i
                nr.append(r)
                ni.append(i)
            xr, xi = nr, ni
        return xr, xi

    for j in range(S5_PAIRS):
        in_map(j, 0)
    carry = ([sre_ref[q * SUBLANES:(q + 1) * SUBLANES, :] for q in range(S5_OCTETS)],
             [sim_ref[q * SUBLANES:(q + 1) * SUBLANES, :] for q in range(S5_OCTETS)])
    for g in range(groups):
        carry = scan(0, g * per, carry)
        in_map(2 * g, 1)
        in_map(2 * g + 1, 1)
    acc = None
    for g in range(groups):
        carry = scan(1, g * per, carry)
        for j in (2 * g, 2 * g + 1):
            acc = accumulate(acc, j, out_map(j, 0), slice(0, half))
    for q in range(S5_OCTETS):
        sre_ref[q * SUBLANES:(q + 1) * SUBLANES, :] = carry[0][q]
        sim_ref[q * SUBLANES:(q + 1) * SUBLANES, :] = carry[1][q]
    for j in range(S5_PAIRS):
        acc = accumulate(acc, j, out_map(j, 1), slice(half, tc))

    y = y_ref[...] + d_ref[...] * u_ref[...].astype(F32)
    z = _gelu_tanh(y)
    gate = jnp.dot(z.astype(BF16), wglu_ref[...].astype(BF16), preferred_element_type=F32) + bglu_ref[...]
    o_ref[...] = (z * _sigmoid(gate)).astype(o_ref.dtype)


def s5_mixer(proj, w_in_blk, w_out_blk, a_re, a_im, d_skip, w_glu, b_glu, layer, *, batch, seq, tc):
    nt = seq // tc
    w = SSM_WIDTH
    kern = functools.partial(_s5_kernel, tc=tc)
    return pl.pallas_call(
        kern,
        out_shape=jax.ShapeDtypeStruct((batch * seq, w), BF16),
        grid=(batch, nt),
        in_specs=[
            pl.BlockSpec((tc, w), lambda b, c: (b * nt + c, COL_SSM // w)),
            _const_spec((None,) + w_in_blk.shape[1:], lambda b, c: (layer, 0, 0, 0)),
            _const_spec((None,) + w_out_blk.shape[1:], lambda b, c: (layer, 0, 0, 0)),
            _const_spec((None, S5_PAIRS, LANES), lambda b, c: (layer, 0, 0)),
            _const_spec((None, S5_PAIRS, LANES), lambda b, c: (layer, 0, 0)),
            _const_spec((None, 1, w), lambda b, c: (layer, 0, 0)),
            _const_spec((None, w, w), lambda b, c: (layer, 0, 0)),
            _const_spec((None, 1, w), lambda b, c: (layer, 0, 0)),
        ],
        out_specs=pl.BlockSpec((tc, w), lambda b, c: (b * nt + c, 0)),
        scratch_shapes=[
            pltpu.VMEM((tc * S5_PITCH, LANES), F32),
            pltpu.VMEM((tc * S5_PITCH, LANES), F32),
            pltpu.VMEM((S5_PAIRS, LANES), F32),
            pltpu.VMEM((S5_PAIRS, LANES), F32),
            pltpu.VMEM((tc, w), F32),
        ],
        compiler_params=_cparams(("parallel", "arbitrary")),
        name="s5_mixer",
    )(proj, w_in_blk, w_out_blk, a_re, a_im, d_skip, w_glu, b_glu)


def _split3(x):
    hi = x.astype(BF16)
    r1 = x - hi.astype(F32)
    mid = r1.astype(BF16)
    lo = (r1 - mid.astype(F32)).astype(BF16)
    return hi, mid, lo


def _gla_kernel(q_ref, k_ref, v_ref, r_ref, xn_ref, wglr_ref, wg_ref, bg_ref, ng_ref, tri_ref, o_ref, st_ref,
                bcum_ref, *, tg):
    c = GLA_CHUNK

    @pl.when(pl.program_id(1) == 0)
    def _():
        st_ref[...] = jnp.zeros_like(st_ref)
        wl = wglr_ref[...].astype(BF16)
        wl = jnp.concatenate([wl, jnp.zeros((LANES - GLA_GATE_RANK, wl.shape[1]), BF16)], axis=0)
        tri = tri_ref[...]
        for t0 in range(0, xn_ref.shape[0], tg):
            ts = slice(t0, t0 + tg)
            glr = lax.dot_general(xn_ref[ts, :], wl, _NT_DIMS, preferred_element_type=F32).astype(BF16)
            glog = jnp.dot(glr, wg_ref[...], preferred_element_type=F32) + bg_ref[...]
            g = (jnp.minimum(glog, 0.0) - jnp.log(1.0 + jnp.exp(-jnp.abs(glog)))) * (1.0 / GLA_GATE_NORM)
            acc = None
            for piece in _split3(g):
                part = jnp.dot(tri, piece, preferred_element_type=F32)
                acc = part if acc is None else acc + part
            bcum_ref[ts, :] = acc

    bcum = bcum_ref[pl.ds(pl.multiple_of(pl.program_id(1) * tg, tg), tg), :]

    row = lax.broadcasted_iota(jnp.int32, (c, c), 0)
    col = lax.broadcasted_iota(jnp.int32, (c, c), 1)
    causal = col <= row
    scale = np.float32(GLA_DK ** -0.5)
    ng = ng_ref[...]
    for n in range(tg // c):
        rs = slice(n * c, (n + 1) * c)
        for h in range(GLA_HEADS):
            ks = slice(h * GLA_DK, (h + 1) * GLA_DK)
            vs = slice(h * GLA_DV, (h + 1) * GLA_DV)
            b = bcum[rs, ks]
            bl = b[c - 1:c, :]
            qh = q_ref[rs, ks].astype(F32) * scale
            kh = k_ref[rs, ks].astype(F32)
            vh = v_ref[rs, vs]
            q_dec = (qh * jnp.exp(b)).astype(BF16)
            k_inv = (kh * jnp.exp(-b)).astype(BF16)
            k_dec = (kh * jnp.exp(bl - b)).astype(BF16)
            attn = lax.dot_general(q_dec, k_inv, _NT_DIMS, preferred_element_type=F32)
            attn = jnp.where(causal, attn, 0.0).astype(BF16)
            st = st_ref[h]
            o = (jnp.dot(attn, vh, preferred_element_type=F32)
                 + lax.dot_general(q_dec, st.astype(BF16), _NT_DIMS, preferred_element_type=F32))
            kvt = lax.dot_general(vh, k_dec, (((0,), (0,)), ((), ())), preferred_element_type=F32)
            st_ref[h] = jnp.exp(bl) * st + kvt
            o = o * _rms_scale(o) * ng
            o_ref[rs, vs] = (o * _silu(r_ref[rs, vs].astype(F32))).astype(o_ref.dtype)


def gla_mixer(proj, xn, w_in_t, w_gate2_pad, b_gate2, norm_g, tri, layer, *, batch, seq, tg):
    nt = seq // tg
    kw, vw = GLA_KEY_WIDTH, GLA_VALUE_WIDTH
    d = xn.shape[1]
    sh = GLA_GATE_RANK
    kern = functools.partial(_gla_kernel, tg=tg)
    return pl.pallas_call(
        kern,
        out_shape=jax.ShapeDtypeStruct((batch * seq, vw), BF16),
        grid=(batch, nt),
        in_specs=[
            pl.BlockSpec((tg, kw), lambda b, c: (b * nt + c, COL_GLA_Q // kw)),
            pl.BlockSpec((tg, kw), lambda b, c: (b * nt + c, COL_GLA_K // kw)),
            pl.BlockSpec((tg, vw), lambda b, c: (b * nt + c, COL_GLA_V // vw)),
            pl.BlockSpec((tg, vw), lambda b, c: (b * nt + c, COL_GLA_R // vw)),
            pl.BlockSpec((seq, d), lambda b, c: (b, 0)),
            _const_spec((None, sh, d), lambda b, c: (layer, ORIG_GLR // sh, 0)),
            _const_spec((None, LANES, kw), lambda b, c: (layer, 0, 0)),
            _const_spec((None, 1, kw), lambda b, c: (layer, 0, 0)),
            _const_spec((None, 1, GLA_DV), lambda b, c: (layer, 0, 0)),
            _const_spec((tg, tg), lambda b, c: (0, 0)),
        ],
        out_specs=pl.BlockSpec((tg, vw), lambda b, c: (b * nt + c, 0)),
        scratch_shapes=[pltpu.VMEM((GLA_HEADS, GLA_DV, GLA_DK), F32), pltpu.VMEM((seq, kw), F32)],
        compiler_params=_cparams(("parallel", "arbitrary")),
        name="gla_mixer",
    )(proj, proj, proj, proj, xn, w_in_t, w_gate2_pad, b_gate2, norm_g, tri)


def _moba_kernel(q_ref, k_ref, v_ref, qg_ref, kg_ref, o_ref, kn_ref, vt_ref, km_ref, st_ref, pt_ref, *, nb):
    bs = MOBA_BLOCK
    kg = kg_ref[...]
    for jb in range(nb):
        kb = k_ref[jb * bs:(jb + 1) * bs, :].astype(F32)
        kn = kb * _rms_scale(kb) * kg
        kn_ref[jb * bs:(jb + 1) * bs, :] = kn.astype(BF16)
        km_ref[jb:jb + 1, :] = jnp.mean(kn, axis=0, keepdims=True)
    hd = MOBA_HEAD_DIM
    vt_ref[0:hd, :] = v_ref[...].astype(F32).T.astype(BF16)
    ones_row = lax.broadcasted_iota(jnp.int32, (vt_ref.shape[0] - hd, vt_ref.shape[1]), 0) == 0
    vt_ref[hd:, :] = jnp.where(ones_row, 1.0, 0.0).astype(BF16)

    qg = qg_ref[...]
    km = km_ref[...]
    blk = lax.broadcasted_iota(jnp.int32, (nb, bs), 0)
    krow = lax.broadcasted_iota(jnp.int32, (bs, bs), 0)
    qcol = lax.broadcasted_iota(jnp.int32, (bs, bs), 1)
    def scores_stage(i):
        q = q_ref[i * bs:(i + 1) * bs, :].astype(F32)
        qn = (q * _rms_scale(q) * qg) * np.float32(MOBA_HEAD_DIM ** -0.5)
        qb = (qn * LOG2_E).astype(BF16)
        bias_t = None
        if i > MOBA_TOPK:
            sbt = lax.dot_general(km, qn, (((1,), (1,)), ((), ())),
                                  precision=lax.Precision.HIGHEST, preferred_element_type=F32)
            cnt = jnp.zeros((nb, bs), F32)
            for jp in range(i):
                sp = sbt[jp:jp + 1, :]
                beats = jnp.where(sp > sbt, 1.0, jnp.where(sp == sbt, jnp.where(blk > jp, 1.0, 0.0), 0.0))
                cnt = cnt + beats
            bias_t = jnp.where(cnt < MOBA_TOPK, 0.0, NEG_BIG)
        s_buf = st_ref.at[i % 2]
        mx8 = None
        for j in range(i + 1):
            ks = slice(j * bs, (j + 1) * bs)
            sj = lax.dot_general(kn_ref[ks, :], qb, _NT_DIMS, preferred_element_type=F32)
            if j == i:
                sj = jnp.where(krow <= qcol, sj, NEG_BIG)
            elif bias_t is not None:
                sj = sj + bias_t[j:j + 1, :]
            s_buf[ks, :] = sj
            pm = jnp.max(sj.reshape(bs // SUBLANES, SUBLANES, bs), axis=0)
            mx8 = pm if mx8 is None else jnp.maximum(mx8, pm)
        return jnp.max(mx8, axis=0, keepdims=True)

    def exp_stage(i, m):
        s_buf = st_ref.at[i % 2]
        p_buf = pt_ref.at[i % 2]
        for j in range(i + 1):
            ks = slice(j * bs, (j + 1) * bs)
            p_buf[ks, :] = jnp.exp2(s_buf[ks, :] - m).astype(BF16)

    def pv_stage(i):
        nk = (i + 1) * bs
        ot = jnp.dot(vt_ref[:, 0:nk], pt_ref[i % 2, 0:nk, :], preferred_element_type=F32)
        o_ref[i * bs:(i + 1) * bs, :] = (ot[0:hd, :] / ot[hd:hd + 1, :]).T.astype(o_ref.dtype)

    maxes = {}
    for step in range(nb + 2):
        if step < nb:
            maxes[step] = scores_stage(step)
        if 0 <= step - 1 < nb:
            exp_stage(step - 1, maxes.pop(step - 1))
        if 0 <= step - 2 < nb:
            pv_stage(step - 2)


def moba_mixer(proj, q_g, k_g, layer, *, batch, seq):
    bs, hd, nh = MOBA_BLOCK, MOBA_HEAD_DIM, MOBA_HEADS
    nb = seq // bs
    kern = functools.partial(_moba_kernel, nb=nb)
    blk = lambda col: pl.BlockSpec((seq, hd), lambda b, h: (b, col // hd + h))
    return pl.pallas_call(
        kern,
        out_shape=jax.ShapeDtypeStruct((batch * seq, MOBA_WIDTH), BF16),
        grid=(batch, nh),
        in_specs=[
            blk(COL_MOBA_Q), blk(COL_MOBA_K), blk(COL_MOBA_V),
            _const_spec((None, 1, hd), lambda b, h: (layer, 0, 0)),
            _const_spec((None, 1, hd), lambda b, h: (layer, 0, 0)),
        ],
        out_specs=pl.BlockSpec((seq, hd), lambda b, h: (b, h)),
        scratch_shapes=[pltpu.VMEM((seq, hd), BF16), pltpu.VMEM((hd + BF16_ROWS, seq), BF16),
                        pltpu.VMEM((nb, hd), F32),
                        pltpu.VMEM((2, seq, bs), F32), pltpu.VMEM((2, seq, bs), BF16)],
        compiler_params=_cparams(("parallel", "parallel")),
        name="moba_mixer",
    )(proj, proj, proj, q_g, k_g)


_WEIGHT_OUTER = ("arbitrary", "arbitrary")


def _merge_kernel(ys_ref, yg_ref, ym_ref, gs_ref, gg_ref, gm_ref, ws_ref, wg_ref, wm_ref, o_ref,
                  wsb_ref, wgb_ref, wmb_ref):
    @pl.when(pl.program_id(1) == 0)
    def _():
        wsb_ref[...] = ws_ref[...].astype(BF16)
        wgb_ref[...] = wg_ref[...].astype(BF16)
        wmb_ref[...] = wm_ref[...].astype(BF16)

    def branch(g_ref, y_ref, w_ref):
        return _sigmoid(g_ref[...].astype(F32)) * jnp.dot(y_ref[...], w_ref[...], preferred_element_type=F32)

    acc = branch(gs_ref, ys_ref, wsb_ref) + branch(gg_ref, yg_ref, wgb_ref) + branch(gm_ref, ym_ref, wmb_ref)
    o_ref[...] = acc.astype(o_ref.dtype)


def merge_branches(y_ssm, y_gla, y_moba, proj, w_s, w_g, w_m, layer, *, tm, tn):
    m, kdim = y_ssm.shape
    n = w_s.shape[-1]
    yspec = pl.BlockSpec((tm, kdim), lambda j, i: (i, 0))
    gspec = lambda col: pl.BlockSpec((tm, tn), lambda j, i: (i, col // tn + j))
    wspec = pl.BlockSpec((None, kdim, tn), lambda j, i: (layer, 0, j))
    return pl.pallas_call(
        _merge_kernel,
        out_shape=jax.ShapeDtypeStruct((m, n), BF16),
        grid=(n // tn, m // tm),
        in_specs=[yspec, yspec, yspec, gspec(COL_GATE_SSM), gspec(COL_GATE_GLA), gspec(COL_GATE_MOBA),
                  wspec, wspec, wspec],
        out_specs=pl.BlockSpec((tm, tn), lambda j, i: (i, j)),
        scratch_shapes=[pltpu.VMEM((kdim, tn), BF16)] * 3,
        compiler_params=_cparams(_WEIGHT_OUTER),
        name="merge",
    )(y_ssm, y_gla, y_moba, proj, proj, proj, w_s, w_g, w_m)


def _out_proj_kernel(a_ref, w_ref, x_ref, g_ref, o_ref, h_ref, wbf_ref):
    @pl.when(pl.program_id(0) == 0)
    def _():
        wbf_ref[...] = w_ref[...].astype(BF16)

    xn = x_ref[...] + jnp.dot(a_ref[...], wbf_ref[...], preferred_element_type=F32)
    o_ref[...] = xn
    h_ref[...] = (xn * _rms_scale(xn) * g_ref[...]).astype(h_ref.dtype)


def out_proj_norm(a, w_all, x, g, layer, *, tm):
    m, kdim = a.shape
    n = w_all.shape[-1]
    row = lambda width: pl.BlockSpec((tm, width), lambda i: (i, 0))
    return pl.pallas_call(
        _out_proj_kernel,
        out_shape=(jax.ShapeDtypeStruct((m, n), F32), jax.ShapeDtypeStruct((m, n), BF16)),
        grid=(m // tm,),
        in_specs=[row(kdim), _const_spec((None, kdim, n), lambda i: (layer, 0, 0)), row(n),
                  _const_spec((None, 1, n), lambda i: (layer, 0, 0))],
        out_specs=(row(n), row(n)),
        scratch_shapes=[pltpu.VMEM((kdim, n), BF16)],
        compiler_params=_cparams(("arbitrary",)),
        name="out_proj",
    )(a, w_all, x, g)


def _resid_matmul_kernel(a_ref, w_ref, x_ref, o_ref, wbf_ref):
    @pl.when(pl.program_id(1) == 0)
    def _():
        wbf_ref[...] = w_ref[...].astype(BF16)

    o_ref[...] = x_ref[...] + jnp.dot(a_ref[...], wbf_ref[...], preferred_element_type=F32)


def resid_matmul(a, w_all, x, layer, *, tm, tn):
    m, kdim = a.shape
    n = w_all.shape[-1]
    return pl.pallas_call(
        _resid_matmul_kernel,
        out_shape=jax.ShapeDtypeStruct((m, n), F32),
        grid=(n // tn, m // tm),
        in_specs=[
            pl.BlockSpec((tm, kdim), lambda j, i: (i, 0)),
            _const_spec((None, kdim, tn), lambda j, i: (layer, 0, j)),
            pl.BlockSpec((tm, tn), lambda j, i: (i, j)),
        ],
        out_specs=pl.BlockSpec((tm, tn), lambda j, i: (i, j)),
        scratch_shapes=[pltpu.VMEM((kdim, tn), BF16)],
        compiler_params=_cparams(_WEIGHT_OUTER),
        name="resid_matmul",
    )(a, w_all, x)


def _ffn_up_kernel(h_ref, wg_ref, wu_ref, o_ref, wgb_ref, wub_ref):
    @pl.when(pl.program_id(1) == 0)
    def _():
        wgb_ref[...] = wg_ref[...].astype(BF16)
        wub_ref[...] = wu_ref[...].astype(BF16)

    h = h_ref[...]
    gate = jnp.dot(h, wgb_ref[...], preferred_element_type=F32)
    up = jnp.dot(h, wub_ref[...], preferred_element_type=F32)
    o_ref[...] = (_silu(gate) * up).astype(o_ref.dtype)


def ffn_up(h, w_gate, w_up, layer, *, tm, tn):
    m, d = h.shape
    n = w_gate.shape[-1]
    wspec = pl.BlockSpec((None, d, tn), lambda j, i: (layer, 0, j))
    return pl.pallas_call(
        _ffn_up_kernel,
        out_shape=jax.ShapeDtypeStruct((m, n), BF16),
        grid=(n // tn, m // tm),
        in_specs=[pl.BlockSpec((tm, d), lambda j, i: (i, 0)), wspec, wspec],
        out_specs=pl.BlockSpec((tm, tn), lambda j, i: (i, j)),
        scratch_shapes=[pltpu.VMEM((d, tn), BF16)] * 2,
        compiler_params=_cparams(_WEIGHT_OUTER),
        name="ffn_up",
    )(h, w_gate, w_up)


def _gla_tri(tg):
    r = np.arange(tg)
    same_chunk = (r[:, None] // GLA_CHUNK) == (r[None, :] // GLA_CHUNK)
    return jnp.asarray(np.where(same_chunk & (r[None, :] <= r[:, None]), 1.0, 0.0), dtype=BF16)


def kernel(x, norm1_g, w_in, ssm_lambda_re, ssm_lambda_im, ssm_log_dt, ssm_b_re, ssm_b_im, ssm_c_re, ssm_c_im, ssm_d, ssm_w_glu, ssm_b_glu, gla_w_gate2, gla_b_gate2, gla_norm_g, moba_q_norm_g, moba_k_norm_g, w_branch_ssm, w_branch_gla, w_branch_moba, w_out, norm2_g, ffn_w_gate, ffn_w_up, ffn_w_down):
    batch, seq, d = x.shape
    depth = w_in.shape[0]
    m = batch * seq
    assert d == D_MODEL and w_in.shape[1:] == (D_MODEL, PROJ_WIDTH + GLA_GATE_RANK)
    assert seq % MOBA_BLOCK == 0 and seq % TILES["s5"]["tc"] == 0 and seq % TILES["gla"]["tg"] == 0
    assert all(m % t["tm"] == 0 for t in TILES.values() if "tm" in t)

    w_in_t = jnp.swapaxes(w_in, 1, 2)
    wg2_pad = jnp.pad(gla_w_gate2, ((0, 0), (0, LANES - GLA_GATE_RANK), (0, 0))).astype(BF16)
    s5_win, s5_wout, a_re_t, a_im_t = s5_weights(ssm_lambda_re, ssm_lambda_im, ssm_log_dt, ssm_b_re, ssm_b_im,
                                                 ssm_c_re, ssm_c_im)
    row3 = lambda t: t[:, None, :]
    n1, n2 = row3(norm1_g), row3(norm2_g)
    ssm_d3, b_glu3 = row3(ssm_d), row3(ssm_b_glu)
    bg2, gng = row3(gla_b_gate2), row3(gla_norm_g)
    mqg, mkg = row3(moba_q_norm_g), row3(moba_k_norm_g)

    tri = _gla_tri(TILES["gla"]["tg"])
    xf = x.reshape(m, d)
    for l in range(depth):
        xn = rmsnorm(xf, n1, l, **TILES["rmsnorm"])
        proj = in_proj(xn, w_in_t, l, **TILES["in_proj"])
        y_ssm = s5_mixer(proj, s5_win, s5_wout, a_re_t, a_im_t, ssm_d3, ssm_w_glu, b_glu3, l,
                         batch=batch, seq=seq, **TILES["s5"])
        y_gla = gla_mixer(proj, xn, w_in_t, wg2_pad, bg2, gng, tri, l, batch=batch, seq=seq, **TILES["gla"])
        y_moba = moba_mixer(proj, mqg, mkg, l, batch=batch, seq=seq)
        merged = merge_branches(y_ssm, y_gla, y_moba, proj, w_branch_ssm, w_branch_gla, w_branch_moba, l,
                                **TILES["merge"])
        xf, hn = out_proj_norm(merged, w_out, xf, n2, l, **TILES["out_proj"])
        act = ffn_up(hn, ffn_w_gate, ffn_w_up, l, **TILES["ffn_up"])
        xf = resid_matmul(act, ffn_w_down, xf, l, **TILES["ffn_down"])
    return xf.reshape(batch, seq, d)
```

```python
import functools

import jax
import jax.numpy as jnp
import numpy as np
from jax import lax
from jax.experimental import pallas as pl
from jax.experimental.pallas import tpu as pltpu

F32 = jnp.float32
BF16 = jnp.bfloat16

D_MODEL = 2048
DEPTH = 4
RMS_EPS = 1e-6
SSM_WIDTH = D_MODEL // 2
SSM_GROUP_SIZE = 16
SSM_GROUPS = SSM_WIDTH // SSM_GROUP_SIZE
SSM_STATE = 64
GLA_HEADS = 4
GLA_KEY_WIDTH = D_MODEL // 4
GLA_VALUE_WIDTH = D_MODEL // 2
GLA_DK = GLA_KEY_WIDTH // GLA_HEADS
GLA_DV = GLA_VALUE_WIDTH // GLA_HEADS
GLA_GATE_RANK = 16
GLA_GATE_NORM = 16.0
GLA_CHUNK = 64
MOBA_WIDTH = D_MODEL // 2
MOBA_HEAD_DIM = 128
MOBA_HEADS = MOBA_WIDTH // MOBA_HEAD_DIM
MOBA_BLOCK = 256
MOBA_TOPK = 3
FFN_HIDDEN = -(-8 * D_MODEL // (3 * 256)) * 256

LANES = 128
SUBLANES = 8
VMEM_LIMIT = 56 * 1024 * 1024

TILES = dict(
    rmsnorm=dict(tm=512),
    in_proj=dict(tm=2048, tn=1024),
    merge=dict(tm=512, tn=1024),
    out_proj=dict(tm=512),
    ffn_up=dict(tm=1024, tn=512),
    ffn_down=dict(tm=512, tn=1024),
    s5=dict(tc=512),
    gla=dict(tg=256),
)

COL_SSM = 0
COL_GLA_Q = COL_SSM + SSM_WIDTH
COL_GLA_K = COL_GLA_Q + GLA_KEY_WIDTH
COL_GLA_V = COL_GLA_K + GLA_KEY_WIDTH
COL_GLA_R = COL_GLA_V + GLA_VALUE_WIDTH
COL_MOBA_Q = COL_GLA_R + GLA_VALUE_WIDTH
COL_MOBA_K = COL_MOBA_Q + MOBA_WIDTH
COL_MOBA_V = COL_MOBA_K + MOBA_WIDTH
COL_GATE_SSM = COL_MOBA_V + MOBA_WIDTH
COL_GATE_GLA = COL_GATE_SSM + D_MODEL
COL_GATE_MOBA = COL_GATE_GLA + D_MODEL
PROJ_WIDTH = COL_GATE_MOBA + D_MODEL
ORIG_GLR = COL_MOBA_Q

S5_PAIRS = SSM_GROUPS // 2
S5_OCTETS = S5_PAIRS // SUBLANES
S5_OCT_CH = SSM_WIDTH // S5_OCTETS
S5_TILE_PAIRS = LANES // (2 * SSM_GROUP_SIZE)
S5_PITCH = S5_PAIRS + 4

NEG_BIG = -1e30
LOG2_E = float(np.log2(np.e))
BF16_ROWS = 2 * SUBLANES


def _cparams(sem):
    return pltpu.CompilerParams(dimension_semantics=sem, vmem_limit_bytes=VMEM_LIMIT)


def _const_spec(shape, index_map):
    return pl.BlockSpec(shape, index_map, pipeline_mode=pl.Buffered(1))


def _sigmoid(x):
    return 0.5 * jnp.tanh(0.5 * x) + 0.5


def _silu(x):
    return x * _sigmoid(x)


def _gelu_tanh(x):
    c = np.float32(np.sqrt(2.0 / np.pi))
    return 0.5 * x * (1.0 + jnp.tanh(c * (x + np.float32(0.044715) * (x * x * x))))


def _rms_scale(x):
    return lax.rsqrt(jnp.mean(x * x, axis=-1, keepdims=True) + RMS_EPS)


def _rmsnorm_kernel(x_ref, g_ref, o_ref):
    x = x_ref[...]
    o_ref[...] = (x * _rms_scale(x) * g_ref[...]).astype(o_ref.dtype)


def rmsnorm(x, g, layer, *, tm):
    m, d = x.shape
    return pl.pallas_call(
        _rmsnorm_kernel,
        out_shape=jax.ShapeDtypeStruct((m, d), BF16),
        grid=(m // tm,),
        in_specs=[pl.BlockSpec((tm, d), lambda i: (i, 0)),
                  _const_spec((None, 1, d), lambda i: (layer, 0, 0))],
        out_specs=pl.BlockSpec((tm, d), lambda i: (i, 0)),
        compiler_params=_cparams(("parallel",)),
        name="rmsnorm",
    )(x, g)


_NT_DIMS = (((1,), (1,)), ((), ()))


def _in_proj_kernel(xn_ref, wa_ref, wb_ref, o_ref, wbf_ref, *, n_aligned):
    j = pl.program_id(0)
    first = pl.program_id(1) == 0
    tn = wa_ref.shape[0]
    sh = GLA_GATE_RANK

    @pl.when(jnp.logical_and(first, j < n_aligned))
    def _():
        wbf_ref[...] = wa_ref[...].astype(BF16)

    @pl.when(jnp.logical_and(first, j >= n_aligned))
    def _():
        wbf_ref[0:tn - sh, :] = wa_ref[sh:tn, :].astype(BF16)
        wbf_ref[tn - sh:tn, :] = wb_ref[...].astype(BF16)

    o_ref[...] = lax.dot_general(xn_ref[...], wbf_ref[...], _NT_DIMS,
                                 preferred_element_type=F32).astype(o_ref.dtype)


def in_proj(xn, w_in_t, layer, *, tm, tn):
    m, d = xn.shape
    sh = GLA_GATE_RANK
    assert ORIG_GLR % tn == 0 and PROJ_WIDTH % tn == 0 and tn % sh == 0
    kern = functools.partial(_in_proj_kernel, n_aligned=ORIG_GLR // tn)
    return pl.pallas_call(
        kern,
        out_shape=jax.ShapeDtypeStruct((m, PROJ_WIDTH), BF16),
        grid=(PROJ_WIDTH // tn, m // tm),
        in_specs=[
            pl.BlockSpec((tm, d), lambda j, i: (i, 0)),
            pl.BlockSpec((None, tn, d), lambda j, i: (layer, j, 0)),
            pl.BlockSpec((None, sh, d), lambda j, i: (layer, (j + 1) * (tn // sh), 0)),
        ],
        out_specs=pl.BlockSpec((tm, tn), lambda j, i: (i, j)),
        scratch_shapes=[pltpu.VMEM((tn, d), BF16)],
        compiler_params=_cparams(("arbitrary", "arbitrary")),
        name="in_proj",
    )(xn, w_in_t, w_in_t)


def _s5_weights_kernel(lre_ref, lim_ref, ldt_ref, bre_ref, bim_ref, cre_ref, cim_ref, rep_ref, rept_ref,
                       win_ref, wout_ref, are_ref, aim_ref):
    lre = lre_ref[...]
    lim = lim_ref[...]
    dt = jnp.exp(ldt_ref[...])
    mag = jnp.exp(lre * dt)
    ang = lim * dt
    a_re = mag * jnp.cos(ang)
    a_im = mag * jnp.sin(ang)
    den = lre * lre + lim * lim
    n_re = a_re - 1.0
    n_im = a_im
    z_re = (n_re * lre + n_im * lim) / den
    z_im = (n_im * lre - n_re * lim) / den
    b_re = bre_ref[...]
    b_im = bim_ref[...]
    are_ref[...] = a_re
    aim_ref[...] = a_im
    bb_re = (z_re * b_re - z_im * b_im).astype(BF16)
    bb_im = (z_re * b_im + z_im * b_re).astype(BF16)

    def own_and_re(s, ch):
        pair = jnp.right_shift(s, 8)
        half = jnp.bitwise_and(jnp.right_shift(s, 6), 1)
        own = jnp.right_shift(ch, 4) == 2 * pair + half
        return own, jnp.bitwise_and(jnp.right_shift(s, 7), 1) == 0

    shape = win_ref.shape
    own, is_re = own_and_re(lax.broadcasted_iota(jnp.int32, shape, 1), lax.broadcasted_iota(jnp.int32, shape, 0))
    t_re = jnp.dot(bb_re, rep_ref[...], preferred_element_type=F32)
    t_im = jnp.dot(bb_im, rep_ref[...], preferred_element_type=F32)
    win_ref[...] = jnp.where(own, jnp.where(is_re, t_re, t_im), 0.0).astype(win_ref.dtype)

    shape = wout_ref.shape
    own, is_re = own_and_re(lax.broadcasted_iota(jnp.int32, shape, 0), lax.broadcasted_iota(jnp.int32, shape, 1))
    t_re = lax.dot_general(rept_ref[...], cre_ref[...].astype(BF16), _NT_DIMS, preferred_element_type=F32)
    t_im = lax.dot_general(rept_ref[...], cim_ref[...].astype(BF16), _NT_DIMS, preferred_element_type=F32)
    wout_ref[...] = jnp.where(own, jnp.where(is_re, t_re, -t_im), 0.0).astype(wout_ref.dtype)


def s5_weights(lam_re, lam_im, log_dt, b_re, b_im, c_re, c_im):
    depth, g, p = lam_re.shape
    s = b_re.shape[-1]
    gpo = g // S5_OCTETS
    blk_rows = gpo * s
    rows = depth * g * s
    rep_rows = lambda t: jnp.broadcast_to(t[:, :, None, :], (depth, g, s, p)).reshape(rows, p)
    ldt = jnp.broadcast_to(log_dt[:, :, None, None], (depth, g, s, p)).reshape(rows, p)
    b_gcp = lambda t: jnp.swapaxes(t, 2, 3).reshape(rows, p)
    span = SUBLANES * 2 * LANES
    rep = jnp.asarray((np.arange(span)[None, :] % p) == np.arange(p)[:, None], dtype=BF16)
    blk = pl.BlockSpec((blk_rows, p), lambda l, o: (l * S5_OCTETS + o, 0))
    w_in, w_out, a_re, a_im = pl.pallas_call(
        _s5_weights_kernel,
        out_shape=(jax.ShapeDtypeStruct((depth, S5_OCTETS, blk_rows, span), BF16),
                   jax.ShapeDtypeStruct((depth, S5_OCTETS, span, blk_rows), BF16),
                   jax.ShapeDtypeStruct((rows, p), F32), jax.ShapeDtypeStruct((rows, p), F32)),
        grid=(depth, S5_OCTETS),
        in_specs=[blk] * 7 + [_const_spec((p, span), lambda l, o: (0, 0)),
                              _const_spec((span, p), lambda l, o: (0, 0))],
        out_specs=(pl.BlockSpec((None, None, blk_rows, span), lambda l, o: (l, o, 0, 0)),
                   pl.BlockSpec((None, None, span, blk_rows), lambda l, o: (l, o, 0, 0)),
                   blk, blk),
        compiler_params=_cparams(("parallel", "parallel")),
        name="s5_weights",
    )(rep_rows(lam_re), rep_rows(lam_im), ldt, b_gcp(b_re), b_gcp(b_im),
      c_re.reshape(rows, p), c_im.reshape(rows, p), rep, rep.T)
    tab = lambda a: a.reshape(depth, g, s, p)[:, :, 0, :].reshape(depth, S5_PAIRS, LANES)
    return w_in, w_out, tab(a_re), tab(a_im)


def _s5_kernel(u_ref, win_ref, wout_ref, are_ref, aim_ref, d_ref, wglu_ref, bglu_ref, o_ref,
               bur_ref, bui_ref, sre_ref, sim_ref, y_ref, *, tc):
    @pl.when(pl.program_id(1) == 0)
    def _():
        sre_ref[...] = jnp.zeros_like(sre_ref)
        sim_ref[...] = jnp.zeros_like(sim_ref)

    half = tc // 2
    pair_w = 2 * LANES
    groups = S5_PAIRS // 2
    per = half // groups

    def in_map(j, h):
        o, jl = divmod(j, SUBLANES)
        t = jl // S5_TILE_PAIRS
        ch0 = o * S5_OCT_CH + t * LANES
        uo = u_ref[h * half:(h + 1) * half, ch0:ch0 + LANES]
        res = jnp.dot(uo, win_ref[o, t * LANES:(t + 1) * LANES, jl * pair_w:(jl + 1) * pair_w],
                      preferred_element_type=F32)
        base = h * half * S5_PITCH + j
        bur_ref[pl.ds(base, half, stride=S5_PITCH), :] = res[:, :LANES]
        bui_ref[pl.ds(base, half, stride=S5_PITCH), :] = res[:, LANES:]

    def out_map(j, h):
        o, jl = divmod(j, SUBLANES)
        t = jl // S5_TILE_PAIRS
        base = h * half * S5_PITCH + j
        xr_j = bur_ref[pl.ds(base, half, stride=S5_PITCH), :]
        xi_j = bui_ref[pl.ds(base, half, stride=S5_PITCH), :]
        lhs = jnp.concatenate([xr_j, xi_j], axis=1).astype(BF16)
        return jnp.dot(lhs, wout_ref[o, jl * pair_w:(jl + 1) * pair_w, t * LANES:(t + 1) * LANES],
                       preferred_element_type=F32)

    def accumulate(acc, j, part, rows):
        acc = part if j % S5_TILE_PAIRS == 0 else acc + part
        if j % S5_TILE_PAIRS == S5_TILE_PAIRS - 1:
            ch0 = (j // S5_TILE_PAIRS) * LANES
            y_ref[rows, ch0:ch0 + LANES] = acc
        return acc

    a_re = [are_ref[q * SUBLANES:(q + 1) * SUBLANES, :] for q in range(S5_OCTETS)]
    a_im = [aim_ref[q * SUBLANES:(q + 1) * SUBLANES, :] for q in range(S5_OCTETS)]

    def scan(h, t0, carry):
        xr, xi = carry
        for t in range(t0, t0 + per):
            nr, ni = [], []
            for q in range(S5_OCTETS):
                row = (h * half + t) * S5_PITCH + q * SUBLANES
                br = bur_ref[row:row + SUBLANES, :]
                bi = bui_ref[row:row + SUBLANES, :]
                r = a_re[q] * xr[q] - a_im[q] * xi[q] + br
                i = a_re[q] * xi[q] + a_im[q] * xr[q] + bi
                bur_ref[row:row + SUBLANES, :] = r
                bui_ref[row:row + SUBLANES, :] = i
                nr.append(r)
                ni.append(i)
            xr, xi = nr, ni
        return xr, xi

    for j in range(S5_PAIRS):
        in_map(j, 0)
    carry = ([sre_ref[q * SUBLANES:(q + 1) * SUBLANES, :] for q in range(S5_OCTETS)],
             [sim_ref[q * SUBLANES:(q + 1) * SUBLANES, :] for q in range(S5_OCTETS)])
    for g in range(groups):
        carry = scan(0, g * per, carry)
        in_map(2 * g, 1)
        in_map(2 * g + 1, 1)
    acc = None
    for g in range(groups):
        carry = scan(1, g * per, carry)
        for j in (2 * g, 2 * g + 1):
            acc = accumulate(acc, j, out_map(j, 0), slice(0, half))
    for q in range(S5_OCTETS):
        sre_ref[q * SUBLANES:(q + 1) * SUBLANES, :] = carry[0][q]
        sim_ref[q * SUBLANES:(q + 1) * SUBLANES, :] = carry[1][q]
    for j in range(S5_PAIRS):
        acc = accumulate(acc, j, out_map(j, 1), slice(half, tc))

    y = y_ref[...] + d_ref[...] * u_ref[...].astype(F32)
    z = _gelu_tanh(y)
    gate = jnp.dot(z.astype(BF16), wglu_ref[...].astype(BF16), preferred_element_type=F32) + bglu_ref[...]
    o_ref[...] = (z * _sigmoid(gate)).astype(o_ref.dtype)


def s5_mixer(proj, w_in_blk, w_out_blk, a_re, a_im, d_skip, w_glu, b_glu, layer, *, batch, seq, tc):
    nt = seq // tc
    w = SSM_WIDTH
    kern = functools.partial(_s5_kernel, tc=tc)
    return pl.pallas_call(
        kern,
        out_shape=jax.ShapeDtypeStruct((batch * seq, w), BF16),
        grid=(batch, nt),
        in_specs=[
            pl.BlockSpec((tc, w), lambda b, c: (b * nt + c, COL_SSM // w)),
            _const_spec((None,) + w_in_blk.shape[1:], lambda b, c: (layer, 0, 0, 0)),
            _const_spec((None,) + w_out_blk.shape[1:], lambda b, c: (layer, 0, 0, 0)),
            _const_spec((None, S5_PAIRS, LANES), lambda b, c: (layer, 0, 0)),
            _const_spec((None, S5_PAIRS, LANES), lambda b, c: (layer, 0, 0)),
            _const_spec((None, 1, w), lambda b, c: (layer, 0, 0)),
            _const_spec((None, w, w), lambda b, c: (layer, 0, 0)),
            _const_spec((None, 1, w), lambda b, c: (layer, 0, 0)),
        ],
        out_specs=pl.BlockSpec((tc, w), lambda b, c: (b * nt + c, 0)),
        scratch_shapes=[
            pltpu.VMEM((tc * S5_PITCH, LANES), F32),
            pltpu.VMEM((tc * S5_PITCH, LANES), F32),
            pltpu.VMEM((S5_PAIRS, LANES), F32),
            pltpu.VMEM((S5_PAIRS, LANES), F32),
            pltpu.VMEM((tc, w), F32),
        ],
        compiler_params=_cparams(("parallel", "arbitrary")),
        name="s5_mixer",
    )(proj, w_in_blk, w_out_blk, a_re, a_im, d_skip, w_glu, b_glu)


def _split3(x):
    hi = x.astype(BF16)
    r1 = x - hi.astype(F32)
    mid = r1.astype(BF16)
    lo = (r1 - mid.astype(F32)).astype(BF16)
    return hi, mid, lo


def _gla_kernel(q_ref, k_ref, v_ref, r_ref, xn_ref, wglr_ref, wg_ref, bg_ref, ng_ref, tri_ref, o_ref, st_ref,
                bcum_ref, *, tg):
    c = GLA_CHUNK

    @pl.when(pl.program_id(1) == 0)
    def _():
        st_ref[...] = jnp.zeros_like(st_ref)
        wl = wglr_ref[...].astype(BF16)
        wl = jnp.concatenate([wl, jnp.zeros((LANES - GLA_GATE_RANK, wl.shape[1]), BF16)], axis=0)
        tri = tri_ref[...]
        for t0 in range(0, xn_ref.shape[0], tg):
            ts = slice(t0, t0 + tg)
            glr = lax.dot_general(xn_ref[ts, :], wl, _NT_DIMS, preferred_element_type=F32).astype(BF16)
            glog = jnp.dot(glr, wg_ref[...], preferred_element_type=F32) + bg_ref[...]
            g = (jnp.minimum(glog, 0.0) - jnp.log(1.0 + jnp.exp(-jnp.abs(glog)))) * (1.0 / GLA_GATE_NORM)
            acc = None
            for piece in _split3(g):
                part = jnp.dot(tri, piece, preferred_element_type=F32)
                acc = part if acc is None else acc + part
            bcum_ref[ts, :] = acc

    bcum = bcum_ref[pl.ds(pl.multiple_of(pl.program_id(1) * tg, tg), tg), :]

    row = lax.broadcasted_iota(jnp.int32, (c, c), 0)
    col = lax.broadcasted_iota(jnp.int32, (c, c), 1)
    causal = col <= row
    scale = np.float32(GLA_DK ** -0.5)
    ng = ng_ref[...]
    for n in range(tg // c):
        rs = slice(n * c, (n + 1) * c)
        for h in range(GLA_HEADS):
            ks = slice(h * GLA_DK, (h + 1) * GLA_DK)
            vs = slice(h * GLA_DV, (h + 1) * GLA_DV)
            b = bcum[rs, ks]
            bl = b[c - 1:c, :]
            qh = q_ref[rs, ks].astype(F32) * scale
            kh = k_ref[rs, ks].astype(F32)
            vh = v_ref[rs, vs]
            q_dec = (qh * jnp.exp(b)).astype(BF16)
            k_inv = (kh * jnp.exp(-b)).astype(BF16)
            k_dec = (kh * jnp.exp(bl - b)).astype(BF16)
            attn = lax.dot_general(q_dec, k_inv, _NT_DIMS, preferred_element_type=F32)
            attn = jnp.where(causal, attn, 0.0).astype(BF16)
            st = st_ref[h]
            o = (jnp.dot(attn, vh, preferred_element_type=F32)
                 + lax.dot_general(q_dec, st.astype(BF16), _NT_DIMS, preferred_element_type=F32))
            kvt = lax.dot_general(vh, k_dec, (((0,), (0,)), ((), ())), preferred_element_type=F32)
            st_ref[h] = jnp.exp(bl) * st + kvt
            o = o * _rms_scale(o) * ng
            o_ref[rs, vs] = (o * _silu(r_ref[rs, vs].astype(F32))).astype(o_ref.dtype)


def gla_mixer(proj, xn, w_in_t, w_gate2_pad, b_gate2, norm_g, tri, layer, *, batch, seq, tg):
    nt = seq // tg
    kw, vw = GLA_KEY_WIDTH, GLA_VALUE_WIDTH
    d = xn.shape[1]
    sh = GLA_GATE_RANK
    kern = functools.partial(_gla_kernel, tg=tg)
    return pl.pallas_call(
        kern,
        out_shape=jax.ShapeDtypeStruct((batch * seq, vw), BF16),
        grid=(batch, nt),
        in_specs=[
            pl.BlockSpec((tg, kw), lambda b, c: (b * nt + c, COL_GLA_Q // kw)),
            pl.BlockSpec((tg, kw), lambda b, c: (b * nt + c, COL_GLA_K // kw)),
            pl.BlockSpec((tg, vw), lambda b, c: (b * nt + c, COL_GLA_V // vw)),
            pl.BlockSpec((tg, vw), lambda b, c: (b * nt + c, COL_GLA_R // vw)),
            pl.BlockSpec((seq, d), lambda b, c: (b, 0)),
            _const_spec((None, sh, d), lambda b, c: (layer, ORIG_GLR // sh, 0)),
            _const_spec((None, LANES, kw), lambda b, c: (layer, 0, 0)),
            _const_spec((None, 1, kw), lambda b, c: (layer, 0, 0)),
            _const_spec((None, 1, GLA_DV), lambda b, c: (layer, 0, 0)),
            _const_spec((tg, tg), lambda b, c: (0, 0)),
        ],
        out_specs=pl.BlockSpec((tg, vw), lambda b, c: (b * nt + c, 0)),
        scratch_shapes=[pltpu.VMEM((GLA_HEADS, GLA_DV, GLA_DK), F32), pltpu.VMEM((seq, kw), F32)],
        compiler_params=_cparams(("parallel", "arbitrary")),
        name="gla_mixer",
    )(proj, proj, proj, proj, xn, w_in_t, w_gate2_pad, b_gate2, norm_g, tri)


def _moba_kernel(q_ref, k_ref, v_ref, qg_ref, kg_ref, o_ref, kn_ref, vt_ref, km_ref, st_ref, pt_ref, *, nb):
    bs = MOBA_BLOCK
    hd = MOBA_HEAD_DIM
    kg = kg_ref[...]
    ones_row = lax.broadcasted_iota(jnp.int32, (vt_ref.shape[0] - hd, vt_ref.shape[1]), 0) == 0
    vt_ref[hd:, :] = jnp.where(ones_row, 1.0, 0.0).astype(BF16)
    km_ref[...] = jnp.zeros_like(km_ref)

    def prepare_block(jb):
        rows = slice(jb * bs, (jb + 1) * bs)
        kb = k_ref[rows, :].astype(F32)
        kn = kb * _rms_scale(kb) * kg
        kn_ref[rows, :] = kn.astype(BF16)
        km_ref[jb:jb + 1, :] = jnp.mean(kn, axis=0, keepdims=True)
        vt_ref[0:hd, rows] = v_ref[rows, :].astype(F32).T.astype(BF16)

    qg = qg_ref[...]
    blk = lax.broadcasted_iota(jnp.int32, (nb, bs), 0)
    krow = lax.broadcasted_iota(jnp.int32, (bs, bs), 0)
    qcol = lax.broadcasted_iota(jnp.int32, (bs, bs), 1)
    def scores_stage(i):
        prepare_block(i)
        q = q_ref[i * bs:(i + 1) * bs, :].astype(F32)
        qn = (q * _rms_scale(q) * qg) * np.float32(MOBA_HEAD_DIM ** -0.5)
        qb = (qn * LOG2_E).astype(BF16)
        bias_t = None
        if i > MOBA_TOPK:
            sbt = lax.dot_general(km_ref[...], qn, _NT_DIMS,
                                  precision=lax.Precision.HIGHEST, preferred_element_type=F32)
            cnt = jnp.zeros((nb, bs), F32)
            for jp in range(i):
                sp = sbt[jp:jp + 1, :]
                beats = jnp.where(sp > sbt, 1.0, jnp.where(sp == sbt, jnp.where(blk > jp, 1.0, 0.0), 0.0))
                cnt = cnt + beats
            bias_t = jnp.where(cnt < MOBA_TOPK, 0.0, NEG_BIG)
        s_buf = st_ref.at[i % 2]
        mx8 = None
        for j in range(i + 1):
            ks = slice(j * bs, (j + 1) * bs)
            sj = lax.dot_general(kn_ref[ks, :], qb, _NT_DIMS, preferred_element_type=F32)
            if j == i:
                sj = jnp.where(krow <= qcol, sj, NEG_BIG)
            elif bias_t is not None:
                sj = sj + bias_t[j:j + 1, :]
            s_buf[ks, :] = sj
            pm = jnp.max(sj.reshape(bs // SUBLANES, SUBLANES, bs), axis=0)
            mx8 = pm if mx8 is None else jnp.maximum(mx8, pm)
        return jnp.max(mx8, axis=0, keepdims=True)

    def exp_stage(i, m):
        s_buf = st_ref.at[i % 2]
        p_buf = pt_ref.at[i % 2]
        for j in range(i + 1):
            ks = slice(j * bs, (j + 1) * bs)
            p_buf[ks, :] = jnp.exp2(s_buf[ks, :] - m).astype(BF16)

    def pv_stage(i):
        nk = (i + 1) * bs
        ot = jnp.dot(vt_ref[:, 0:nk], pt_ref[i % 2, 0:nk, :], preferred_element_type=F32)
        o_ref[i * bs:(i + 1) * bs, :] = (ot[0:hd, :] / ot[hd:hd + 1, :]).T.astype(o_ref.dtype)

    maxes = {}
    for step in range(nb + 2):
        if step < nb:
            maxes[step] = scores_stage(step)
        if 0 <= step - 1 < nb:
            exp_stage(step - 1, maxes.pop(step - 1))
        if 0 <= step - 2 < nb:
            pv_stage(step - 2)


def moba_mixer(proj, q_g, k_g, layer, *, batch, seq):
    bs, hd, nh = MOBA_BLOCK, MOBA_HEAD_DIM, MOBA_HEADS
    nb = seq // bs
    kern = functools.partial(_moba_kernel, nb=nb)
    blk = lambda col: pl.BlockSpec((seq, hd), lambda b, h: (b, col // hd + h))
    return pl.pallas_call(
        kern,
        out_shape=jax.ShapeDtypeStruct((batch * seq, MOBA_WIDTH), BF16),
        grid=(batch, nh),
        in_specs=[
            blk(COL_MOBA_Q), blk(COL_MOBA_K), blk(COL_MOBA_V),
            _const_spec((None, 1, hd), lambda b, h: (layer, 0, 0)),
            _const_spec((None, 1, hd), lambda b, h: (layer, 0, 0)),
        ],
        out_specs=pl.BlockSpec((seq, hd), lambda b, h: (b, h)),
        scratch_shapes=[pltpu.VMEM((seq, hd), BF16), pltpu.VMEM((hd + BF16_ROWS, seq), BF16),
                        pltpu.VMEM((nb, hd), F32),
                        pltpu.VMEM((2, seq, bs), F32), pltpu.VMEM((2, seq, bs), BF16)],
        compiler_params=_cparams(("parallel", "parallel")),
        name="moba_mixer",
    )(proj, proj, proj, q_g, k_g)


_WEIGHT_OUTER = ("arbitrary", "arbitrary")


def _merge_kernel(ys_ref, yg_ref, ym_ref, gs_ref, gg_ref, gm_ref, ws_ref, wg_ref, wm_ref, o_ref,
                  wsb_ref, wgb_ref, wmb_ref):
    @pl.when(pl.program_id(1) == 0)
    def _():
        wsb_ref[...] = ws_ref[...].astype(BF16)
        wgb_ref[...] = wg_ref[...].astype(BF16)
        wmb_ref[...] = wm_ref[...].astype(BF16)

    def branch(g_ref, y_ref, w_ref):
        return _sigmoid(g_ref[...].astype(F32)) * jnp.dot(y_ref[...], w_ref[...], preferred_element_type=F32)

    acc = branch(gs_ref, ys_ref, wsb_ref) + branch(gg_ref, yg_ref, wgb_ref) + branch(gm_ref, ym_ref, wmb_ref)
    o_ref[...] = acc.astype(o_ref.dtype)


def merge_branches(y_ssm, y_gla, y_moba, proj, w_s, w_g, w_m, layer, *, tm, tn):
    m, kdim = y_ssm.shape
    n = w_s.shape[-1]
    yspec = pl.BlockSpec((tm, kdim), lambda j, i: (i, 0))
    gspec = lambda col: pl.BlockSpec((tm, tn), lambda j, i: (i, col // tn + j))
    wspec = pl.BlockSpec((None, kdim, tn), lambda j, i: (layer, 0, j))
    return pl.pallas_call(
        _merge_kernel,
        out_shape=jax.ShapeDtypeStruct((m, n), BF16),
        grid=(n // tn, m // tm),
        in_specs=[yspec, yspec, yspec, gspec(COL_GATE_SSM), gspec(COL_GATE_GLA), gspec(COL_GATE_MOBA),
                  wspec, wspec, wspec],
        out_specs=pl.BlockSpec((tm, tn), lambda j, i: (i, j)),
        scratch_shapes=[pltpu.VMEM((kdim, tn), BF16)] * 3,
        compiler_params=_cparams(_WEIGHT_OUTER),
        name="merge",
    )(y_ssm, y_gla, y_moba, proj, proj, proj, w_s, w_g, w_m)


def _out_proj_kernel(a_ref, w_ref, x_ref, g_ref, o_ref, h_ref, wbf_ref):
    @pl.when(pl.program_id(0) == 0)
    def _():
        wbf_ref[...] = w_ref[...].astype(BF16)

    xn = x_ref[...] + jnp.dot(a_ref[...], wbf_ref[...], preferred_element_type=F32)
    o_ref[...] = xn
    h_ref[...] = (xn * _rms_scale(xn) * g_ref[...]).astype(h_ref.dtype)


def out_proj_norm(a, w_all, x, g, layer, *, tm):
    m, kdim = a.shape
    n = w_all.shape[-1]
    row = lambda width: pl.BlockSpec((tm, width), lambda i: (i, 0))
    return pl.pallas_call(
        _out_proj_kernel,
        out_shape=(jax.ShapeDtypeStruct((m, n), F32), jax.ShapeDtypeStruct((m, n), BF16)),
        grid=(m // tm,),
        in_specs=[row(kdim), _const_spec((None, kdim, n), lambda i: (layer, 0, 0)), row(n),
                  _const_spec((None, 1, n), lambda i: (layer, 0, 0))],
        out_specs=(row(n), row(n)),
        scratch_shapes=[pltpu.VMEM((kdim, n), BF16)],
        compiler_params=_cparams(("arbitrary",)),
        name="out_proj",
    )(a, w_all, x, g)


def _resid_matmul_kernel(a_ref, w_ref, x_ref, o_ref, wbf_ref):
    @pl.when(pl.program_id(1) == 0)
    def _():
        wbf_ref[...] = w_ref[...].astype(BF16)

    o_ref[...] = x_ref[...] + jnp.dot(a_ref[...], wbf_ref[...], preferred_element_type=F32)


def resid_matmul(a, w_all, x, layer, *, tm, tn):
    m, kdim = a.shape
    n = w_all.shape[-1]
    return pl.pallas_call(
        _resid_matmul_kernel,
        out_shape=jax.ShapeDtypeStruct((m, n), F32),
        grid=(n // tn, m // tm),
        in_specs=[
            pl.BlockSpec((tm, kdim), lambda j, i: (i, 0)),
            _const_spec((None, kdim, tn), lambda j, i: (layer, 0, j)),
            pl.BlockSpec((tm, tn), lambda j, i: (i, j)),
        ],
        out_specs=pl.BlockSpec((tm, tn), lambda j, i: (i, j)),
        scratch_shapes=[pltpu.VMEM((kdim, tn), BF16)],
        compiler_params=_cparams(_WEIGHT_OUTER),
        name="resid_matmul",
    )(a, w_all, x)


def _ffn_up_kernel(h_ref, wg_ref, wu_ref, o_ref, wgb_ref, wub_ref):
    @pl.when(pl.program_id(1) == 0)
    def _():
        wgb_ref[...] = wg_ref[...].astype(BF16)
        wub_ref[...] = wu_ref[...].astype(BF16)

    h = h_ref[...]
    gate = jnp.dot(h, wgb_ref[...], preferred_element_type=F32)
    up = jnp.dot(h, wub_ref[...], preferred_element_type=F32)
    o_ref[...] = (_silu(gate) * up).astype(o_ref.dtype)


def ffn_up(h, w_gate, w_up, layer, *, tm, tn):
    m, d = h.shape
    n = w_gate.shape[-1]
    wspec = pl.BlockSpec((None, d, tn), lambda j, i: (layer, 0, j))
    return pl.pallas_call(
        _ffn_up_kernel,
        out_shape=jax.ShapeDtypeStruct((m, n), BF16),
        grid=(n // tn, m // tm),
        in_specs=[pl.BlockSpec((tm, d), lambda j, i: (i, 0)), wspec, wspec],
        out_specs=pl.BlockSpec((tm, tn), lambda j, i: (i, j)),
        scratch_shapes=[pltpu.VMEM((d, tn), BF16)] * 2,
        compiler_params=_cparams(_WEIGHT_OUTER),
        name="ffn_up",
    )(h, w_gate, w_up)


def _gla_tri(tg):
    r = np.arange(tg)
    same_chunk = (r[:, None] // GLA_CHUNK) == (r[None, :] // GLA_CHUNK)
    return jnp.asarray(np.where(same_chunk & (r[None, :] <= r[:, None]), 1.0, 0.0), dtype=BF16)


def kernel(x, norm1_g, w_in, ssm_lambda_re, ssm_lambda_im, ssm_log_dt, ssm_b_re, ssm_b_im, ssm_c_re, ssm_c_im, ssm_d, ssm_w_glu, ssm_b_glu, gla_w_gate2, gla_b_gate2, gla_norm_g, moba_q_norm_g, moba_k_norm_g, w_branch_ssm, w_branch_gla, w_branch_moba, w_out, norm2_g, ffn_w_gate, ffn_w_up, ffn_w_down):
    batch, seq, d = x.shape
    depth = w_in.shape[0]
    m = batch * seq
    assert d == D_MODEL and w_in.shape[1:] == (D_MODEL, PROJ_WIDTH + GLA_GATE_RANK)
    assert seq % MOBA_BLOCK == 0 and seq % TILES["s5"]["tc"] == 0 and seq % TILES["gla"]["tg"] == 0
    assert all(m % t["tm"] == 0 for t in TILES.values() if "tm" in t)

    w_in_t = jnp.swapaxes(w_in, 1, 2)
    wg2_pad = jnp.pad(gla_w_gate2, ((0, 0), (0, LANES - GLA_GATE_RANK), (0, 0))).astype(BF16)
    s5_win, s5_wout, a_re_t, a_im_t = s5_weights(ssm_lambda_re, ssm_lambda_im, ssm_log_dt, ssm_b_re, ssm_b_im,
                                                 ssm_c_re, ssm_c_im)
    row3 = lambda t: t[:, None, :]
    n1, n2 = row3(norm1_g), row3(norm2_g)
    ssm_d3, b_glu3 = row3(ssm_d), row3(ssm_b_glu)
    bg2, gng = row3(gla_b_gate2), row3(gla_norm_g)
    mqg, mkg = row3(moba_q_norm_g), row3(moba_k_norm_g)

    tri = _gla_tri(TILES["gla"]["tg"])
    xf = x.reshape(m, d)
    for l in range(depth):
        xn = rmsnorm(xf, n1, l, **TILES["rmsnorm"])
        proj = in_proj(xn, w_in_t, l, **TILES["in_proj"])
        y_ssm = s5_mixer(proj, s5_win, s5_wout, a_re_t, a_im_t, ssm_d3, ssm_w_glu, b_glu3, l,
                         batch=batch, seq=seq, **TILES["s5"])
        y_gla = gla_mixer(proj, xn, w_in_t, wg2_pad, bg2, gng, tri, l, batch=batch, seq=seq, **TILES["gla"])
        y_moba = moba_mixer(proj, mqg, mkg, l, batch=batch, seq=seq)
        merged = merge_branches(y_ssm, y_gla, y_moba, proj, w_branch_ssm, w_branch_gla, w_branch_moba, l,
                                **TILES["merge"])
        xf, hn = out_proj_norm(merged, w_out, xf, n2, l, **TILES["out_proj"])
        act = ffn_up(hn, ffn_w_gate, ffn_w_up, l, **TILES["ffn_up"])
        xf = resid_matmul(act, ffn_w_down, xf, l, **TILES["ffn_down"])
    return xf.reshape(batch, seq, d)
```

```python
import functools

import jax
import jax.numpy as jnp
import numpy as np
from jax import lax
from jax.experimental import pallas as pl
from jax.experimental.pallas import tpu as pltpu

F32 = jnp.float32
BF16 = jnp.bfloat16

D_MODEL = 2048
DEPTH = 4
RMS_EPS = 1e-6
SSM_WIDTH = D_MODEL // 2
SSM_GROUP_SIZE = 16
SSM_GROUPS = SSM_WIDTH // SSM_GROUP_SIZE
SSM_STATE = 64
GLA_HEADS = 4
GLA_KEY_WIDTH = D_MODEL // 4
GLA_VALUE_WIDTH = D_MODEL // 2
GLA_DK = GLA_KEY_WIDTH // GLA_HEADS
GLA_DV = GLA_VALUE_WIDTH // GLA_HEADS
GLA_GATE_RANK = 16
GLA_GATE_NORM = 16.0
GLA_CHUNK = 64
MOBA_WIDTH = D_MODEL // 2
MOBA_HEAD_DIM = 128
MOBA_HEADS = MOBA_WIDTH // MOBA_HEAD_DIM
MOBA_BLOCK = 256
MOBA_TOPK = 3
FFN_HIDDEN = -(-8 * D_MODEL // (3 * 256)) * 256

LANES = 128
SUBLANES = 8
VMEM_LIMIT = 56 * 1024 * 1024

TILES = dict(
    rmsnorm=dict(tm=512),
    in_proj=dict(tm=2048, tn=1024),
    merge=dict(tm=512, tn=1024),
    out_proj=dict(tm=512),
    ffn_up=dict(tm=1024, tn=512),
    ffn_down=dict(tm=512, tn=1024),
    s5=dict(tc=512),
    gla=dict(tg=256),
)

COL_SSM = 0
COL_GLA_Q = COL_SSM + SSM_WIDTH
COL_GLA_K = COL_GLA_Q + GLA_KEY_WIDTH
COL_GLA_V = COL_GLA_K + GLA_KEY_WIDTH
COL_GLA_R = COL_GLA_V + GLA_VALUE_WIDTH
COL_MOBA_Q = COL_GLA_R + GLA_VALUE_WIDTH
COL_MOBA_K = COL_MOBA_Q + MOBA_WIDTH
COL_MOBA_V = COL_MOBA_K + MOBA_WIDTH
COL_GATE_SSM = COL_MOBA_V + MOBA_WIDTH
COL_GATE_GLA = COL_GATE_SSM + D_MODEL
COL_GATE_MOBA = COL_GATE_GLA + D_MODEL
PROJ_WIDTH = COL_GATE_MOBA + D_MODEL
ORIG_GLR = COL_MOBA_Q

S5_PAIRS = SSM_GROUPS // 2
S5_OCTETS = S5_PAIRS // SUBLANES
S5_OCT_CH = SSM_WIDTH // S5_OCTETS
S5_TILE_PAIRS = LANES // (2 * SSM_GROUP_SIZE)
S5_PITCH = S5_PAIRS + 4

NEG_BIG = -1e30
LOG2_E = float(np.log2(np.e))
BF16_ROWS = 2 * SUBLANES


def _cparams(sem):
    return pltpu.CompilerParams(dimension_semantics=sem, vmem_limit_bytes=VMEM_LIMIT)


def _const_spec(shape, index_map):
    return pl.BlockSpec(shape, index_map, pipeline_mode=pl.Buffered(1))


def _sigmoid(x):
    return 0.5 * jnp.tanh(0.5 * x) + 0.5


def _silu(x):
    return x * _sigmoid(x)


def _gelu_tanh(x):
    c = np.float32(np.sqrt(2.0 / np.pi))
    return 0.5 * x * (1.0 + jnp.tanh(c * (x + np.float32(0.044715) * (x * x * x))))


def _rms_scale(x):
    return lax.rsqrt(jnp.mean(x * x, axis=-1, keepdims=True) + RMS_EPS)


def _rmsnorm_kernel(x_ref, g_ref, o_ref):
    x = x_ref[...]
    o_ref[...] = (x * _rms_scale(x) * g_ref[...]).astype(o_ref.dtype)


def rmsnorm(x, g, layer, *, tm):
    m, d = x.shape
    return pl.pallas_call(
        _rmsnorm_kernel,
        out_shape=jax.ShapeDtypeStruct((m, d), BF16),
        grid=(m // tm,),
        in_specs=[pl.BlockSpec((tm, d), lambda i: (i, 0)),
                  _const_spec((None, 1, d), lambda i: (layer, 0, 0))],
        out_specs=pl.BlockSpec((tm, d), lambda i: (i, 0)),
        compiler_params=_cparams(("parallel",)),
        name="rmsnorm",
    )(x, g)


_NT_DIMS = (((1,), (1,)), ((), ()))


def _in_proj_kernel(xn_ref, wa_ref, wb_ref, o_ref, wbf_ref, *, n_aligned):
    j = pl.program_id(0)
    first = pl.program_id(1) == 0
    tn = wa_ref.shape[0]
    sh = GLA_GATE_RANK

    @pl.when(jnp.logical_and(first, j < n_aligned))
    def _():
        wbf_ref[...] = wa_ref[...].astype(BF16)

    @pl.when(jnp.logical_and(first, j >= n_aligned))
    def _():
        wbf_ref[0:tn - sh, :] = wa_ref[sh:tn, :].astype(BF16)
        wbf_ref[tn - sh:tn, :] = wb_ref[...].astype(BF16)

    o_ref[...] = lax.dot_general(xn_ref[...], wbf_ref[...], _NT_DIMS,
                                 preferred_element_type=F32).astype(o_ref.dtype)


def in_proj(xn, w_in_t, layer, *, tm, tn):
    m, d = xn.shape
    sh = GLA_GATE_RANK
    assert ORIG_GLR % tn == 0 and PROJ_WIDTH % tn == 0 and tn % sh == 0
    kern = functools.partial(_in_proj_kernel, n_aligned=ORIG_GLR // tn)
    return pl.pallas_call(
        kern,
        out_shape=jax.ShapeDtypeStruct((m, PROJ_WIDTH), BF16),
        grid=(PROJ_WIDTH // tn, m // tm),
        in_specs=[
            pl.BlockSpec((tm, d), lambda j, i: (i, 0)),
            pl.BlockSpec((None, tn, d), lambda j, i: (layer, j, 0)),
            pl.BlockSpec((None, sh, d), lambda j, i: (layer, (j + 1) * (tn // sh), 0)),
        ],
        out_specs=pl.BlockSpec((tm, tn), lambda j, i: (i, j)),
        scratch_shapes=[pltpu.VMEM((tn, d), BF16)],
        compiler_params=_cparams(("arbitrary", "arbitrary")),
        name="in_proj",
    )(xn, w_in_t, w_in_t)


def _s5_weights_kernel(lre_ref, lim_ref, ldt_ref, bre_ref, bim_ref, cre_ref, cim_ref, rep_ref, rept_ref,
                       win_ref, wout_ref, are_ref, aim_ref):
    lre = lre_ref[...]
    lim = lim_ref[...]
    dt = jnp.exp(ldt_ref[...])
    mag = jnp.exp(lre * dt)
    ang = lim * dt
    a_re = mag * jnp.cos(ang)
    a_im = mag * jnp.sin(ang)
    den = lre * lre + lim * lim
    n_re = a_re - 1.0
    n_im = a_im
    z_re = (n_re * lre + n_im * lim) / den
    z_im = (n_im * lre - n_re * lim) / den
    b_re = bre_ref[...]
    b_im = bim_ref[...]
    are_ref[...] = a_re
    aim_ref[...] = a_im
    bb_re = (z_re * b_re - z_im * b_im).astype(BF16)
    bb_im = (z_re * b_im + z_im * b_re).astype(BF16)

    def own_and_re(s, ch):
        pair = jnp.right_shift(s, 8)
        half = jnp.bitwise_and(jnp.right_shift(s, 6), 1)
        own = jnp.right_shift(ch, 4) == 2 * pair + half
        return own, jnp.bitwise_and(jnp.right_shift(s, 7), 1) == 0

    shape = win_ref.shape
    own, is_re = own_and_re(lax.broadcasted_iota(jnp.int32, shape, 1), lax.broadcasted_iota(jnp.int32, shape, 0))
    t_re = jnp.dot(bb_re, rep_ref[...], preferred_element_type=F32)
    t_im = jnp.dot(bb_im, rep_ref[...], preferred_element_type=F32)
    win_ref[...] = jnp.where(own, jnp.where(is_re, t_re, t_im), 0.0).astype(win_ref.dtype)

    shape = wout_ref.shape
    own, is_re = own_and_re(lax.broadcasted_iota(jnp.int32, shape, 0), lax.broadcasted_iota(jnp.int32, shape, 1))
    t_re = lax.dot_general(rept_ref[...], cre_ref[...].astype(BF16), _NT_DIMS, preferred_element_type=F32)
    t_im = lax.dot_general(rept_ref[...], cim_ref[...].astype(BF16), _NT_DIMS, preferred_element_type=F32)
    wout_ref[...] = jnp.where(own, jnp.where(is_re, t_re, -t_im), 0.0).astype(wout_ref.dtype)


def s5_weights(lam_re, lam_im, log_dt, b_re, b_im, c_re, c_im):
    depth, g, p = lam_re.shape
    s = b_re.shape[-1]
    gpo = g // S5_OCTETS
    blk_rows = gpo * s
    rows = depth * g * s
    rep_rows = lambda t: jnp.broadcast_to(t[:, :, None, :], (depth, g, s, p)).reshape(rows, p)
    ldt = jnp.broadcast_to(log_dt[:, :, None, None], (depth, g, s, p)).reshape(rows, p)
    b_gcp = lambda t: jnp.swapaxes(t, 2, 3).reshape(rows, p)
    span = SUBLANES * 2 * LANES
    rep = jnp.asarray((np.arange(span)[None, :] % p) == np.arange(p)[:, None], dtype=BF16)
    blk = pl.BlockSpec((blk_rows, p), lambda l, o: (l * S5_OCTETS + o, 0))
    w_in, w_out, a_re, a_im = pl.pallas_call(
        _s5_weights_kernel,
        out_shape=(jax.ShapeDtypeStruct((depth, S5_OCTETS, blk_rows, span), BF16),
                   jax.ShapeDtypeStruct((depth, S5_OCTETS, span, blk_rows), BF16),
                   jax.ShapeDtypeStruct((rows, p), F32), jax.ShapeDtypeStruct((rows, p), F32)),
        grid=(depth, S5_OCTETS),
        in_specs=[blk] * 7 + [_const_spec((p, span), lambda l, o: (0, 0)),
                              _const_spec((span, p), lambda l, o: (0, 0))],
        out_specs=(pl.BlockSpec((None, None, blk_rows, span), lambda l, o: (l, o, 0, 0)),
                   pl.BlockSpec((None, None, span, blk_rows), lambda l, o: (l, o, 0, 0)),
                   blk, blk),
        compiler_params=_cparams(("parallel", "parallel")),
        name="s5_weights",
    )(rep_rows(lam_re), rep_rows(lam_im), ldt, b_gcp(b_re), b_gcp(b_im),
      c_re.reshape(rows, p), c_im.reshape(rows, p), rep, rep.T)
    tab = lambda a: a.reshape(depth, g, s, p)[:, :, 0, :].reshape(depth, S5_PAIRS, LANES)
    return w_in, w_out, tab(a_re), tab(a_im)


def _s5_kernel(u_ref, win_ref, wout_ref, are_ref, aim_ref, d_ref, wglu_ref, bglu_ref, o_ref,
               bur_ref, bui_ref, sre_ref, sim_ref, y_ref, *, tc):
    @pl.when(pl.program_id(1) == 0)
    def _():
        sre_ref[...] = jnp.zeros_like(sre_ref)
        sim_ref[...] = jnp.zeros_like(sim_ref)

    half = tc // 2
    pair_w = 2 * LANES
    groups = S5_PAIRS // 2
    per = half // groups

    def in_map(j, h):
        o, jl = divmod(j, SUBLANES)
        t = jl // S5_TILE_PAIRS
        ch0 = o * S5_OCT_CH + t * LANES
        uo = u_ref[h * half:(h + 1) * half, ch0:ch0 + LANES]
        res = jnp.dot(uo, win_ref[o, t * LANES:(t + 1) * LANES, jl * pair_w:(jl + 1) * pair_w],
                      preferred_element_type=F32)
        base = h * half * S5_PITCH + j
        bur_ref[pl.ds(base, half, stride=S5_PITCH), :] = res[:, :LANES]
        bui_ref[pl.ds(base, half, stride=S5_PITCH), :] = res[:, LANES:]

    def out_map(j, h):
        o, jl = divmod(j, SUBLANES)
        t = jl // S5_TILE_PAIRS
        base = h * half * S5_PITCH + j
        xr_j = bur_ref[pl.ds(base, half, stride=S5_PITCH), :]
        xi_j = bui_ref[pl.ds(base, half, stride=S5_PITCH), :]
        lhs = jnp.concatenate([xr_j, xi_j], axis=1).astype(BF16)
        return jnp.dot(lhs, wout_ref[o, jl * pair_w:(jl + 1) * pair_w, t * LANES:(t + 1) * LANES],
                       preferred_element_type=F32)

    def accumulate(acc, j, part, rows):
        acc = part if j % S5_TILE_PAIRS == 0 else acc + part
        if j % S5_TILE_PAIRS == S5_TILE_PAIRS - 1:
            ch0 = (j // S5_TILE_PAIRS) * LANES
            y_ref[rows, ch0:ch0 + LANES] = acc
        return acc

    a_re = [are_ref[q * SUBLANES:(q + 1) * SUBLANES, :] for q in range(S5_OCTETS)]
    a_im = [aim_ref[q * SUBLANES:(q + 1) * SUBLANES, :] for q in range(S5_OCTETS)]

    def scan(h, t0, carry):
        xr, xi = carry
        for t in range(t0, t0 + per):
            nr, ni = [], []
            for q in range(S5_OCTETS):
                row = (h * half + t) * S5_PITCH + q * SUBLANES
                br = bur_ref[row:row + SUBLANES, :]
                bi = bui_ref[row:row + SUBLANES, :]
                r = a_re[q] * xr[q] - a_im[q] * xi[q] + br
                i = a_re[q] * xi[q] + a_im[q] * xr[q] + bi
                bur_ref[row:row + SUBLANES, :] = r
                bui_ref[row:row + SUBLANES, :] = i
                nr.append(r)
                ni.append(i)
            xr, xi = nr, ni
        return xr, xi

    for j in range(S5_PAIRS):
        in_map(j, 0)
    carry = ([sre_ref[q * SUBLANES:(q + 1) * SUBLANES, :] for q in range(S5_OCTETS)],
             [sim_ref[q * SUBLANES:(q + 1) * SUBLANES, :] for q in range(S5_OCTETS)])
    for g in range(groups):
        carry = scan(0, g * per, carry)
        in_map(2 * g, 1)
        in_map(2 * g + 1, 1)
    acc = None
    for g in range(groups):
        carry = scan(1, g * per, carry)
        for j in (2 * g, 2 * g + 1):
            acc = accumulate(acc, j, out_map(j, 0), slice(0, half))
    for q in range(S5_OCTETS):
        sre_ref[q * SUBLANES:(q + 1) * SUBLANES, :] = carry[0][q]
        sim_ref[q * SUBLANES:(q + 1) * SUBLANES, :] = carry[1][q]
    for j in range(S5_PAIRS):
        acc = accumulate(acc, j, out_map(j, 1), slice(half, tc))

    y = y_ref[...] + d_ref[...] * u_ref[...].astype(F32)
    z = _gelu_tanh(y)
    gate = jnp.dot(z.astype(BF16), wglu_ref[...].astype(BF16), preferred_element_type=F32) + bglu_ref[...]
    o_ref[...] = (z * _sigmoid(gate)).astype(o_ref.dtype)


def s5_mixer(proj, w_in_blk, w_out_blk, a_re, a_im, d_skip, w_glu, b_glu, layer, *, batch, seq, tc):
    nt = seq // tc
    w = SSM_WIDTH
    kern = functools.partial(_s5_kernel, tc=tc)
    return pl.pallas_call(
        kern,
        out_shape=jax.ShapeDtypeStruct((batch * seq, w), BF16),
        grid=(batch, nt),
        in_specs=[
            pl.BlockSpec((tc, w), lambda b, c: (b * nt + c, COL_SSM // w)),
            _const_spec((None,) + w_in_blk.shape[1:], lambda b, c: (layer, 0, 0, 0)),
            _const_spec((None,) + w_out_blk.shape[1:], lambda b, c: (layer, 0, 0, 0)),
            _const_spec((None, S5_PAIRS, LANES), lambda b, c: (layer, 0, 0)),
            _const_spec((None, S5_PAIRS, LANES), lambda b, c: (layer, 0, 0)),
            _const_spec((None, 1, w), lambda b, c: (layer, 0, 0)),
            _const_spec((None, w, w), lambda b, c: (layer, 0, 0)),
            _const_spec((None, 1, w), lambda b, c: (layer, 0, 0)),
        ],
        out_specs=pl.BlockSpec((tc, w), lambda b, c: (b * nt + c, 0)),
        scratch_shapes=[
            pltpu.VMEM((tc * S5_PITCH, LANES), F32),
            pltpu.VMEM((tc * S5_PITCH, LANES), F32),
            pltpu.VMEM((S5_PAIRS, LANES), F32),
            pltpu.VMEM((S5_PAIRS, LANES), F32),
            pltpu.VMEM((tc, w), F32),
        ],
        compiler_params=_cparams(("parallel", "arbitrary")),
        name="s5_mixer",
    )(proj, w_in_blk, w_out_blk, a_re, a_im, d_skip, w_glu, b_glu)


def _split3(x):
    hi = x.astype(BF16)
    r1 = x - hi.astype(F32)
    mid = r1.astype(BF16)
    lo = (r1 - mid.astype(F32)).astype(BF16)
    return hi, mid, lo


def _gla_kernel(q_ref, k_ref, v_ref, r_ref, xn_ref, wglr_ref, wg_ref, bg_ref, ng_ref, tri_ref, o_ref, st_ref,
                bcum_ref, *, tg):
    c = GLA_CHUNK

    @pl.when(pl.program_id(1) == 0)
    def _():
        st_ref[...] = jnp.zeros_like(st_ref)
        wl = wglr_ref[...].astype(BF16)
        wl = jnp.concatenate([wl, jnp.zeros((LANES - GLA_GATE_RANK, wl.shape[1]), BF16)], axis=0)
        tri = tri_ref[...]
        for t0 in range(0, xn_ref.shape[0], tg):
            ts = slice(t0, t0 + tg)
            glr = lax.dot_general(xn_ref[ts, :], wl, _NT_DIMS, preferred_element_type=F32).astype(BF16)
            glog = jnp.dot(glr, wg_ref[...], preferred_element_type=F32) + bg_ref[...]
            g = (jnp.minimum(glog, 0.0) - jnp.log(1.0 + jnp.exp(-jnp.abs(glog)))) * (1.0 / GLA_GATE_NORM)
            acc = None
            for piece in _split3(g):
                part = jnp.dot(tri, piece, preferred_element_type=F32)
                acc = part if acc is None else acc + part
            bcum_ref[ts, :] = acc

    bcum = bcum_ref[pl.ds(pl.multiple_of(pl.program_id(1) * tg, tg), tg), :]

    row = lax.broadcasted_iota(jnp.int32, (c, c), 0)
    col = lax.broadcasted_iota(jnp.int32, (c, c), 1)
    causal = col <= row
    scale = np.float32(GLA_DK ** -0.5)
    ng = ng_ref[...]
    for n in range(tg // c):
        rs = slice(n * c, (n + 1) * c)
        for h in range(GLA_HEADS):
            ks = slice(h * GLA_DK, (h + 1) * GLA_DK)
            vs = slice(h * GLA_DV, (h + 1) * GLA_DV)
            b = bcum[rs, ks]
            bl = b[c - 1:c, :]
            qh = q_ref[rs, ks].astype(F32) * scale
            kh = k_ref[rs, ks].astype(F32)
            vh = v_ref[rs, vs]
            q_dec = (qh * jnp.exp(b)).astype(BF16)
            k_inv = (kh * jnp.exp(-b)).astype(BF16)
            k_dec = (kh * jnp.exp(bl - b)).astype(BF16)
            attn = lax.dot_general(q_dec, k_inv, _NT_DIMS, preferred_element_type=F32)
            attn = jnp.where(causal, attn, 0.0).astype(BF16)
            st = st_ref[h]
            o = (jnp.dot(attn, vh, preferred_element_type=F32)
                 + lax.dot_general(q_dec, st.astype(BF16), _NT_DIMS, preferred_element_type=F32))
            kvt = lax.dot_general(vh, k_dec, (((0,), (0,)), ((), ())), preferred_element_type=F32)
            st_ref[h] = jnp.exp(bl) * st + kvt
            o = o * _rms_scale(o) * ng
            o_ref[rs, vs] = (o * _silu(r_ref[rs, vs].astype(F32))).astype(o_ref.dtype)


def gla_mixer(proj, xn, w_in_t, w_gate2_pad, b_gate2, norm_g, tri, layer, *, batch, seq, tg):
    nt = seq // tg
    kw, vw = GLA_KEY_WIDTH, GLA_VALUE_WIDTH
    d = xn.shape[1]
    sh = GLA_GATE_RANK
    kern = functools.partial(_gla_kernel, tg=tg)
    return pl.pallas_call(
        kern,
        out_shape=jax.ShapeDtypeStruct((batch * seq, vw), BF16),
        grid=(batch, nt),
        in_specs=[
            pl.BlockSpec((tg, kw), lambda b, c: (b * nt + c, COL_GLA_Q // kw)),
            pl.BlockSpec((tg, kw), lambda b, c: (b * nt + c, COL_GLA_K // kw)),
            pl.BlockSpec((tg, vw), lambda b, c: (b * nt + c, COL_GLA_V // vw)),
            pl.BlockSpec((tg, vw), lambda b, c: (b * nt + c, COL_GLA_R // vw)),
            pl.BlockSpec((seq, d), lambda b, c: (b, 0)),
            _const_spec((None, sh, d), lambda b, c: (layer, ORIG_GLR // sh, 0)),
            _const_spec((None, LANES, kw), lambda b, c: (layer, 0, 0)),
            _const_spec((None, 1, kw), lambda b, c: (layer, 0, 0)),
            _const_spec((None, 1, GLA_DV), lambda b, c: (layer, 0, 0)),
            _const_spec((tg, tg), lambda b, c: (0, 0)),
        ],
        out_specs=pl.BlockSpec((tg, vw), lambda b, c: (b * nt + c, 0)),
        scratch_shapes=[pltpu.VMEM((GLA_HEADS, GLA_DV, GLA_DK), F32), pltpu.VMEM((seq, kw), F32)],
        compiler_params=_cparams(("parallel", "arbitrary")),
        name="gla_mixer",
    )(proj, proj, proj, proj, xn, w_in_t, w_gate2_pad, b_gate2, norm_g, tri)


def _moba_kernel(q_ref, k_ref, v_ref, qg_ref, kg_ref, o_ref, kn_ref, vt_ref, km_ref, st_ref, pt_ref, *, nb):
    bs = MOBA_BLOCK
    kg = kg_ref[...]
    for jb in range(nb):
        kb = k_ref[jb * bs:(jb + 1) * bs, :].astype(F32)
        kn = kb * _rms_scale(kb) * kg
        kn_ref[jb * bs:(jb + 1) * bs, :] = kn.astype(BF16)
        km_ref[jb:jb + 1, :] = jnp.mean(kn, axis=0, keepdims=True)
    hd = MOBA_HEAD_DIM
    vt_ref[0:hd, :] = v_ref[...].astype(F32).T.astype(BF16)
    ones_row = lax.broadcasted_iota(jnp.int32, (vt_ref.shape[0] - hd, vt_ref.shape[1]), 0) == 0
    vt_ref[hd:, :] = jnp.where(ones_row, 1.0, 0.0).astype(BF16)

    qg = qg_ref[...]
    km = km_ref[...]
    blk = lax.broadcasted_iota(jnp.int32, (nb, bs), 0)
    krow = lax.broadcasted_iota(jnp.int32, (bs, bs), 0)
    qcol = lax.broadcasted_iota(jnp.int32, (bs, bs), 1)
    def scores_stage(i):
        q = q_ref[i * bs:(i + 1) * bs, :].astype(F32)
        qn = (q * _rms_scale(q) * qg) * np.float32(MOBA_HEAD_DIM ** -0.5)
        qb = (qn * LOG2_E).astype(BF16)
        bias_t = None
        if i > MOBA_TOPK:
            sbt = lax.dot_general(km, qn, (((1,), (1,)), ((), ())),
                                  precision=lax.Precision.HIGHEST, preferred_element_type=F32)
            cnt = jnp.zeros((nb, bs), F32)
            for jp in range(i):
                sp = sbt[jp:jp + 1, :]
                beats = jnp.where(sp > sbt, 1.0, jnp.where(sp == sbt, jnp.where(blk > jp, 1.0, 0.0), 0.0))
                cnt = cnt + beats
            bias_t = jnp.where(cnt < MOBA_TOPK, 0.0, NEG_BIG)
        s_buf = st_ref.at[i % 2]
        mx8 = None
        for j in range(i + 1):
            ks = slice(j * bs, (j + 1) * bs)
            sj = lax.dot_general(kn_ref[ks, :], qb, _NT_DIMS, preferred_element_type=F32)
            if j == i:
                sj = jnp.where(krow <= qcol, sj, NEG_BIG)
            elif bias_t is not None:
                sj = sj + bias_t[j:j + 1, :]
            s_buf[ks, :] = sj
            pm = jnp.max(sj.reshape(bs // SUBLANES, SUBLANES, bs), axis=0)
            mx8 = pm if mx8 is None else jnp.maximum(mx8, pm)
        return jnp.max(mx8, axis=0, keepdims=True)

    def exp_stage(i, m):
        s_buf = st_ref.at[i % 2]
        p_buf = pt_ref.at[i % 2]
        for j in range(i + 1):
            ks = slice(j * bs, (j + 1) * bs)
            p_buf[ks, :] = jnp.exp2(s_buf[ks, :] - m).astype(BF16)

    def pv_stage(i):
        nk = (i + 1) * bs
        ot = jnp.dot(vt_ref[:, 0:nk], pt_ref[i % 2, 0:nk, :], preferred_element_type=F32)
        o_ref[i * bs:(i + 1) * bs, :] = (ot[0:hd, :] / ot[hd:hd + 1, :]).T.astype(o_ref.dtype)

    maxes = {}
    order = list(range(nb))[::-1]
    for step in range(nb + 2):
        if step < nb:
            maxes[order[step]] = scores_stage(order[step])
        if 0 <= step - 1 < nb:
            exp_stage(order[step - 1], maxes.pop(order[step - 1]))
        if 0 <= step - 2 < nb:
            pv_stage(order[step - 2])


def moba_mixer(proj, q_g, k_g, layer, *, batch, seq):
    bs, hd, nh = MOBA_BLOCK, MOBA_HEAD_DIM, MOBA_HEADS
    nb = seq // bs
    kern = functools.partial(_moba_kernel, nb=nb)
    blk = lambda col: pl.BlockSpec((seq, hd), lambda b, h: (b, col // hd + h))
    return pl.pallas_call(
        kern,
        out_shape=jax.ShapeDtypeStruct((batch * seq, MOBA_WIDTH), BF16),
        grid=(batch, nh),
        in_specs=[
            blk(COL_MOBA_Q), blk(COL_MOBA_K), blk(COL_MOBA_V),
            _const_spec((None, 1, hd), lambda b, h: (layer, 0, 0)),
            _const_spec((None, 1, hd), lambda b, h: (layer, 0, 0)),
        ],
        out_specs=pl.BlockSpec((seq, hd), lambda b, h: (b, h)),
        scratch_shapes=[pltpu.VMEM((seq, hd), BF16), pltpu.VMEM((hd + BF16_ROWS, seq), BF16),
                        pltpu.VMEM((nb, hd), F32),
                        pltpu.VMEM((2, seq, bs), F32), pltpu.VMEM((2, seq, bs), BF16)],
        compiler_params=_cparams(("parallel", "parallel")),
        name="moba_mixer",
    )(proj, proj, proj, q_g, k_g)


_WEIGHT_OUTER = ("arbitrary", "arbitrary")


def _merge_kernel(ys_ref, yg_ref, ym_ref, gs_ref, gg_ref, gm_ref, ws_ref, wg_ref, wm_ref, o_ref,
                  wsb_ref, wgb_ref, wmb_ref):
    @pl.when(pl.program_id(1) == 0)
    def _():
        wsb_ref[...] = ws_ref[...].astype(BF16)
        wgb_ref[...] = wg_ref[...].astype(BF16)
        wmb_ref[...] = wm_ref[...].astype(BF16)

    def branch(g_ref, y_ref, w_ref):
        return _sigmoid(g_ref[...].astype(F32)) * jnp.dot(y_ref[...], w_ref[...], preferred_element_type=F32)

    acc = branch(gs_ref, ys_ref, wsb_ref) + branch(gg_ref, yg_ref, wgb_ref) + branch(gm_ref, ym_ref, wmb_ref)
    o_ref[...] = acc.astype(o_ref.dtype)


def merge_branches(y_ssm, y_gla, y_moba, proj, w_s, w_g, w_m, layer, *, tm, tn):
    m, kdim = y_ssm.shape
    n = w_s.shape[-1]
    yspec = pl.BlockSpec((tm, kdim), lambda j, i: (i, 0))
    gspec = lambda col: pl.BlockSpec((tm, tn), lambda j, i: (i, col // tn + j))
    wspec = pl.BlockSpec((None, kdim, tn), lambda j, i: (layer, 0, j))
    return pl.pallas_call(
        _merge_kernel,
        out_shape=jax.ShapeDtypeStruct((m, n), BF16),
        grid=(n // tn, m // tm),
        in_specs=[yspec, yspec, yspec, gspec(COL_GATE_SSM), gspec(COL_GATE_GLA), gspec(COL_GATE_MOBA),
                  wspec, wspec, wspec],
        out_specs=pl.BlockSpec((tm, tn), lambda j, i: (i, j)),
        scratch_shapes=[pltpu.VMEM((kdim, tn), BF16)] * 3,
        compiler_params=_cparams(_WEIGHT_OUTER),
        name="merge",
    )(y_ssm, y_gla, y_moba, proj, proj, proj, w_s, w_g, w_m)


def _out_proj_kernel(a_ref, w_ref, x_ref, g_ref, o_ref, h_ref, wbf_ref):
    @pl.when(pl.program_id(0) == 0)
    def _():
        wbf_ref[...] = w_ref[...].astype(BF16)

    xn = x_ref[...] + jnp.dot(a_ref[...], wbf_ref[...], preferred_element_type=F32)
    o_ref[...] = xn
    h_ref[...] = (xn * _rms_scale(xn) * g_ref[...]).astype(h_ref.dtype)


def out_proj_norm(a, w_all, x, g, layer, *, tm):
    m, kdim = a.shape
    n = w_all.shape[-1]
    row = lambda width: pl.BlockSpec((tm, width), lambda i: (i, 0))
    return pl.pallas_call(
        _out_proj_kernel,
        out_shape=(jax.ShapeDtypeStruct((m, n), F32), jax.ShapeDtypeStruct((m, n), BF16)),
        grid=(m // tm,),
        in_specs=[row(kdim), _const_spec((None, kdim, n), lambda i: (layer, 0, 0)), row(n),
                  _const_spec((None, 1, n), lambda i: (layer, 0, 0))],
        out_specs=(row(n), row(n)),
        scratch_shapes=[pltpu.VMEM((kdim, n), BF16)],
        compiler_params=_cparams(("arbitrary",)),
        name="out_proj",
    )(a, w_all, x, g)


def _resid_matmul_kernel(a_ref, w_ref, x_ref, o_ref, wbf_ref):
    @pl.when(pl.program_id(1) == 0)
    def _():
        wbf_ref[...] = w_ref[...].astype(BF16)

    o_ref[...] = x_ref[...] + jnp.dot(a_ref[...], wbf_ref[...], preferred_element_type=F32)


def resid_matmul(a, w_all, x, layer, *, tm, tn):
    m, kdim = a.shape
    n = w_all.shape[-1]
    return pl.pallas_call(
        _resid_matmul_kernel,
        out_shape=jax.ShapeDtypeStruct((m, n), F32),
        grid=(n // tn, m // tm),
        in_specs=[
            pl.BlockSpec((tm, kdim), lambda j, i: (i, 0)),
            _const_spec((None, kdim, tn), lambda j, i: (layer, 0, j)),
            pl.BlockSpec((tm, tn), lambda j, i: (i, j)),
        ],
        out_specs=pl.BlockSpec((tm, tn), lambda j, i: (i, j)),
        scratch_shapes=[pltpu.VMEM((kdim, tn), BF16)],
        compiler_params=_cparams(_WEIGHT_OUTER),
        name="resid_matmul",
    )(a, w_all, x)


def _ffn_up_kernel(h_ref, wg_ref, wu_ref, o_ref, wgb_ref, wub_ref):
    @pl.when(pl.program_id(1) == 0)
    def _():
        wgb_ref[...] = wg_ref[...].astype(BF16)
        wub_ref[...] = wu_ref[...].astype(BF16)

    h = h_ref[...]
    gate = jnp.dot(h, wgb_ref[...], preferred_element_type=F32)
    up = jnp.dot(h, wub_ref[...], preferred_element_type=F32)
    o_ref[...] = (_silu(gate) * up).astype(o_ref.dtype)


def ffn_up(h, w_gate, w_up, layer, *, tm, tn):
    m, d = h.shape
    n = w_gate.shape[-1]
    wspec = pl.BlockSpec((None, d, tn), lambda j, i: (layer, 0, j))
    return pl.pallas_call(
        _ffn_up_kernel,
        out_shape=jax.ShapeDtypeStruct((m, n), BF16),
        grid=(n // tn, m // tm),
        in_specs=[pl.BlockSpec((tm, d), lambda j, i: (i, 0)), wspec, wspec],
        out_specs=pl.BlockSpec((tm, tn), lambda j, i: (i, j)),
        scratch_shapes=[pltpu.VMEM((d, tn), BF16)] * 2,
        compiler_params=_cparams(_WEIGHT_OUTER),
        name="ffn_up",
    )(h, w_gate, w_up)


def _gla_tri(tg):
    r = np.arange(tg)
    same_chunk = (r[:, None] // GLA_CHUNK) == (r[None, :] // GLA_CHUNK)
    return jnp.asarray(np.where(same_chunk & (r[None, :] <= r[:, None]), 1.0, 0.0), dtype=BF16)


def kernel(x, norm1_g, w_in, ssm_lambda_re, ssm_lambda_im, ssm_log_dt, ssm_b_re, ssm_b_im, ssm_c_re, ssm_c_im, ssm_d, ssm_w_glu, ssm_b_glu, gla_w_gate2, gla_b_gate2, gla_norm_g, moba_q_norm_g, moba_k_norm_g, w_branch_ssm, w_branch_gla, w_branch_moba, w_out, norm2_g, ffn_w_gate, ffn_w_up, ffn_w_down):
    batch, seq, d = x.shape
    depth = w_in.shape[0]
    m = batch * seq
    assert d == D_MODEL and w_in.shape[1:] == (D_MODEL, PROJ_WIDTH + GLA_GATE_RANK)
    assert seq % MOBA_BLOCK == 0 and seq % TILES["s5"]["tc"] == 0 and seq % TILES["gla"]["tg"] == 0
    assert all(m % t["tm"] == 0 for t in TILES.values() if "tm" in t)

    w_in_t = jnp.swapaxes(w_in, 1, 2)
    wg2_pad = jnp.pad(gla_w_gate2, ((0, 0), (0, LANES - GLA_GATE_RANK), (0, 0))).astype(BF16)
    s5_win, s5_wout, a_re_t, a_im_t = s5_weights(ssm_lambda_re, ssm_lambda_im, ssm_log_dt, ssm_b_re, ssm_b_im,
                                                 ssm_c_re, ssm_c_im)
    row3 = lambda t: t[:, None, :]
    n1, n2 = row3(norm1_g), row3(norm2_g)
    ssm_d3, b_glu3 = row3(ssm_d), row3(ssm_b_glu)
    bg2, gng = row3(gla_b_gate2), row3(gla_norm_g)
    mqg, mkg = row3(moba_q_norm_g), row3(moba_k_norm_g)

    tri = _gla_tri(TILES["gla"]["tg"])
    xf = x.reshape(m, d)
    for l in range(depth):
        xn = rmsnorm(xf, n1, l, **TILES["rmsnorm"])
        proj = in_proj(xn, w_in_t, l, **TILES["in_proj"])
        y_ssm = s5_mixer(proj, s5_win, s5_wout, a_re_t, a_im_t, ssm_d3, ssm_w_glu, b_glu3, l,
                         batch=batch, seq=seq, **TILES["s5"])
        y_gla = gla_mixer(proj, xn, w_in_t, wg2_pad, bg2, gng, tri, l, batch=batch, seq=seq, **TILES["gla"])
        y_moba = moba_mixer(proj, mqg, mkg, l, batch=batch, seq=seq)
        merged = merge_branches(y_ssm, y_gla, y_moba, proj, w_branch_ssm, w_branch_gla, w_branch_moba, l,
                                **TILES["merge"])
        xf, hn = out_proj_norm(merged, w_out, xf, n2, l, **TILES["out_proj"])
        act = ffn_up(hn, ffn_w_gate, ffn_w_up, l, **TILES["ffn_up"])
        xf = resid_matmul(act, ffn_w_down, xf, l, **TILES["ffn_down"])
    return xf.reshape(batch, seq, d)
```

```python
import functools

import jax
import jax.numpy as jnp
import numpy as np
from jax import lax
from jax.experimental import pallas as pl
from jax.experimental.pallas import tpu as pltpu

F32 = jnp.float32
BF16 = jnp.bfloat16

D_MODEL = 2048
DEPTH = 4
RMS_EPS = 1e-6
SSM_WIDTH = D_MODEL // 2
SSM_GROUP_SIZE = 16
SSM_GROUPS = SSM_WIDTH // SSM_GROUP_SIZE
SSM_STATE = 64
GLA_HEADS = 4
GLA_KEY_WIDTH = D_MODEL // 4
GLA_VALUE_WIDTH = D_MODEL // 2
GLA_DK = GLA_KEY_WIDTH // GLA_HEADS
GLA_DV = GLA_VALUE_WIDTH // GLA_HEADS
GLA_GATE_RANK = 16
GLA_GATE_NORM = 16.0
GLA_CHUNK = 64
MOBA_WIDTH = D_MODEL // 2
MOBA_HEAD_DIM = 128
MOBA_HEADS = MOBA_WIDTH // MOBA_HEAD_DIM
MOBA_BLOCK = 256
MOBA_TOPK = 3
FFN_HIDDEN = -(-8 * D_MODEL // (3 * 256)) * 256

LANES = 128
SUBLANES = 8
VMEM_LIMIT = 56 * 1024 * 1024

TILES = dict(
    rmsnorm=dict(tm=512),
    in_proj=dict(tm=2048, tn=1024),
    merge=dict(tm=512, tn=1024),
    out_proj=dict(tm=512),
    ffn_up=dict(tm=1024, tn=512),
    ffn_down=dict(tm=512, tn=512),
    s5=dict(tc=512),
    gla=dict(tg=256),
)

COL_SSM = 0
COL_GLA_Q = COL_SSM + SSM_WIDTH
COL_GLA_K = COL_GLA_Q + GLA_KEY_WIDTH
COL_GLA_V = COL_GLA_K + GLA_KEY_WIDTH
COL_GLA_R = COL_GLA_V + GLA_VALUE_WIDTH
COL_MOBA_Q = COL_GLA_R + GLA_VALUE_WIDTH
COL_MOBA_K = COL_MOBA_Q + MOBA_WIDTH
COL_MOBA_V = COL_MOBA_K + MOBA_WIDTH
COL_GATE_SSM = COL_MOBA_V + MOBA_WIDTH
COL_GATE_GLA = COL_GATE_SSM + D_MODEL
COL_GATE_MOBA = COL_GATE_GLA + D_MODEL
PROJ_WIDTH = COL_GATE_MOBA + D_MODEL
ORIG_GLR = COL_MOBA_Q

S5_PAIRS = SSM_GROUPS // 2
S5_OCTETS = S5_PAIRS // SUBLANES
S5_OCT_CH = SSM_WIDTH // S5_OCTETS
S5_TILE_PAIRS = LANES // (2 * SSM_GROUP_SIZE)
S5_PITCH = S5_PAIRS + 4

NEG_BIG = -1e30
LOG2_E = float(np.log2(np.e))
BF16_ROWS = 2 * SUBLANES


def _cparams(sem):
    return pltpu.CompilerParams(dimension_semantics=sem, vmem_limit_bytes=VMEM_LIMIT)


def _const_spec(shape, index_map):
    return pl.BlockSpec(shape, index_map, pipeline_mode=pl.Buffered(1))


def _sigmoid(x):
    return 0.5 * jnp.tanh(0.5 * x) + 0.5


def _silu(x):
    return x * _sigmoid(x)


def _gelu_tanh(x):
    c = np.float32(np.sqrt(2.0 / np.pi))
    return 0.5 * x * (1.0 + jnp.tanh(c * (x + np.float32(0.044715) * (x * x * x))))


def _rms_scale(x):
    return lax.rsqrt(jnp.mean(x * x, axis=-1, keepdims=True) + RMS_EPS)


def _rmsnorm_kernel(x_ref, g_ref, o_ref):
    x = x_ref[...]
    o_ref[...] = (x * _rms_scale(x) * g_ref[...]).astype(o_ref.dtype)


def rmsnorm(x, g, layer, *, tm):
    m, d = x.shape
    return pl.pallas_call(
        _rmsnorm_kernel,
        out_shape=jax.ShapeDtypeStruct((m, d), BF16),
        grid=(m // tm,),
        in_specs=[pl.BlockSpec((tm, d), lambda i: (i, 0)),
                  _const_spec((None, 1, d), lambda i: (layer, 0, 0))],
        out_specs=pl.BlockSpec((tm, d), lambda i: (i, 0)),
        compiler_params=_cparams(("parallel",)),
        name="rmsnorm",
    )(x, g)


_NT_DIMS = (((1,), (1,)), ((), ()))


def _in_proj_kernel(xn_ref, wa_ref, wb_ref, o_ref, wbf_ref, *, n_aligned):
    j = pl.program_id(0)
    first = pl.program_id(1) == 0
    tn = wa_ref.shape[0]
    sh = GLA_GATE_RANK

    @pl.when(jnp.logical_and(first, j < n_aligned))
    def _():
        wbf_ref[...] = wa_ref[...].astype(BF16)

    @pl.when(jnp.logical_and(first, j >= n_aligned))
    def _():
        wbf_ref[0:tn - sh, :] = wa_ref[sh:tn, :].astype(BF16)
        wbf_ref[tn - sh:tn, :] = wb_ref[...].astype(BF16)

    o_ref[...] = lax.dot_general(xn_ref[...], wbf_ref[...], _NT_DIMS,
                                 preferred_element_type=F32).astype(o_ref.dtype)


def in_proj(xn, w_in_t, layer, *, tm, tn):
    m, d = xn.shape
    sh = GLA_GATE_RANK
    assert ORIG_GLR % tn == 0 and PROJ_WIDTH % tn == 0 and tn % sh == 0
    kern = functools.partial(_in_proj_kernel, n_aligned=ORIG_GLR // tn)
    return pl.pallas_call(
        kern,
        out_shape=jax.ShapeDtypeStruct((m, PROJ_WIDTH), BF16),
        grid=(PROJ_WIDTH // tn, m // tm),
        in_specs=[
            pl.BlockSpec((tm, d), lambda j, i: (i, 0)),
            pl.BlockSpec((None, tn, d), lambda j, i: (layer, j, 0)),
            pl.BlockSpec((None, sh, d), lambda j, i: (layer, (j + 1) * (tn // sh), 0)),
        ],
        out_specs=pl.BlockSpec((tm, tn), lambda j, i: (i, j)),
        scratch_shapes=[pltpu.VMEM((tn, d), BF16)],
        compiler_params=_cparams(("arbitrary", "arbitrary")),
        name="in_proj",
    )(xn, w_in_t, w_in_t)


def _s5_weights_kernel(lre_ref, lim_ref, ldt_ref, bre_ref, bim_ref, cre_ref, cim_ref, rep_ref, rept_ref,
                       win_ref, wout_ref, are_ref, aim_ref):
    lre = lre_ref[...]
    lim = lim_ref[...]
    dt = jnp.exp(ldt_ref[...])
    mag = jnp.exp(lre * dt)
    ang = lim * dt
    a_re = mag * jnp.cos(ang)
    a_im = mag * jnp.sin(ang)
    den = lre * lre + lim * lim
    n_re = a_re - 1.0
    n_im = a_im
    z_re = (n_re * lre + n_im * lim) / den
    z_im = (n_im * lre - n_re * lim) / den
    b_re = bre_ref[...]
    b_im = bim_ref[...]
    are_ref[...] = a_re
    aim_ref[...] = a_im
    bb_re = (z_re * b_re - z_im * b_im).astype(BF16)
    bb_im = (z_re * b_im + z_im * b_re).astype(BF16)

    def own_and_re(s, ch):
        pair = jnp.right_shift(s, 8)
        half = jnp.bitwise_and(jnp.right_shift(s, 6), 1)
        own = jnp.right_shift(ch, 4) == 2 * pair + half
        return own, jnp.bitwise_and(jnp.right_shift(s, 7), 1) == 0

    shape = win_ref.shape
    own, is_re = own_and_re(lax.broadcasted_iota(jnp.int32, shape, 1), lax.broadcasted_iota(jnp.int32, shape, 0))
    t_re = jnp.dot(bb_re, rep_ref[...], preferred_element_type=F32)
    t_im = jnp.dot(bb_im, rep_ref[...], preferred_element_type=F32)
    win_ref[...] = jnp.where(own, jnp.where(is_re, t_re, t_im), 0.0).astype(win_ref.dtype)

    shape = wout_ref.shape
    own, is_re = own_and_re(lax.broadcasted_iota(jnp.int32, shape, 0), lax.broadcasted_iota(jnp.int32, shape, 1))
    t_re = lax.dot_general(rept_ref[...], cre_ref[...].astype(BF16), _NT_DIMS, preferred_element_type=F32)
    t_im = lax.dot_general(rept_ref[...], cim_ref[...].astype(BF16), _NT_DIMS, preferred_element_type=F32)
    wout_ref[...] = jnp.where(own, jnp.where(is_re, t_re, -t_im), 0.0).astype(wout_ref.dtype)


def s5_weights(lam_re, lam_im, log_dt, b_re, b_im, c_re, c_im):
    depth, g, p = lam_re.shape
    s = b_re.shape[-1]
    gpo = g // S5_OCTETS
    blk_rows = gpo * s
    rows = depth * g * s
    rep_rows = lambda t: jnp.broadcast_to(t[:, :, None, :], (depth, g, s, p)).reshape(rows, p)
    ldt = jnp.broadcast_to(log_dt[:, :, None, None], (depth, g, s, p)).reshape(rows, p)
    b_gcp = lambda t: jnp.swapaxes(t, 2, 3).reshape(rows, p)
    span = SUBLANES * 2 * LANES
    rep = jnp.asarray((np.arange(span)[None, :] % p) == np.arange(p)[:, None], dtype=BF16)
    blk = pl.BlockSpec((blk_rows, p), lambda l, o: (l * S5_OCTETS + o, 0))
    w_in, w_out, a_re, a_im = pl.pallas_call(
        _s5_weights_kernel,
        out_shape=(jax.ShapeDtypeStruct((depth, S5_OCTETS, blk_rows, span), BF16),
                   jax.ShapeDtypeStruct((depth, S5_OCTETS, span, blk_rows), BF16),
                   jax.ShapeDtypeStruct((rows, p), F32), jax.ShapeDtypeStruct((rows, p), F32)),
        grid=(depth, S5_OCTETS),
        in_specs=[blk] * 7 + [_const_spec((p, span), lambda l, o: (0, 0)),
                              _const_spec((span, p), lambda l, o: (0, 0))],
        out_specs=(pl.BlockSpec((None, None, blk_rows, span), lambda l, o: (l, o, 0, 0)),
                   pl.BlockSpec((None, None, span, blk_rows), lambda l, o: (l, o, 0, 0)),
                   blk, blk),
        compiler_params=_cparams(("parallel", "parallel")),
        name="s5_weights",
    )(rep_rows(lam_re), rep_rows(lam_im), ldt, b_gcp(b_re), b_gcp(b_im),
      c_re.reshape(rows, p), c_im.reshape(rows, p), rep, rep.T)
    tab = lambda a: a.reshape(depth, g, s, p)[:, :, 0, :].reshape(depth, S5_PAIRS, LANES)
    return w_in, w_out, tab(a_re), tab(a_im)


def _s5_kernel(u_ref, win_ref, wout_ref, are_ref, aim_ref, d_ref, wglu_ref, bglu_ref, o_ref,
               bur_ref, bui_ref, sre_ref, sim_ref, y_ref, *, tc):
    @pl.when(pl.program_id(1) == 0)
    def _():
        sre_ref[...] = jnp.zeros_like(sre_ref)
        sim_ref[...] = jnp.zeros_like(sim_ref)

    half = tc // 2
    pair_w = 2 * LANES
    groups = S5_PAIRS // 2
    per = half // groups

    def in_map(j, h):
        o, jl = divmod(j, SUBLANES)
        t = jl // S5_TILE_PAIRS
        ch0 = o * S5_OCT_CH + t * LANES
        uo = u_ref[h * half:(h + 1) * half, ch0:ch0 + LANES]
        res = jnp.dot(uo, win_ref[o, t * LANES:(t + 1) * LANES, jl * pair_w:(jl + 1) * pair_w],
                      preferred_element_type=F32)
        base = h * half * S5_PITCH + j
        bur_ref[pl.ds(base, half, stride=S5_PITCH), :] = res[:, :LANES]
        bui_ref[pl.ds(base, half, stride=S5_PITCH), :] = res[:, LANES:]

    def out_map(j, h):
        o, jl = divmod(j, SUBLANES)
        t = jl // S5_TILE_PAIRS
        base = h * half * S5_PITCH + j
        xr_j = bur_ref[pl.ds(base, half, stride=S5_PITCH), :]
        xi_j = bui_ref[pl.ds(base, half, stride=S5_PITCH), :]
        lhs = jnp.concatenate([xr_j, xi_j], axis=1).astype(BF16)
        return jnp.dot(lhs, wout_ref[o, jl * pair_w:(jl + 1) * pair_w, t * LANES:(t + 1) * LANES],
                       preferred_element_type=F32)

    def accumulate(acc, j, part, rows):
        acc = part if j % S5_TILE_PAIRS == 0 else acc + part
        if j % S5_TILE_PAIRS == S5_TILE_PAIRS - 1:
            ch0 = (j // S5_TILE_PAIRS) * LANES
            y_ref[rows, ch0:ch0 + LANES] = acc
        return acc

    a_re = [are_ref[q * SUBLANES:(q + 1) * SUBLANES, :] for q in range(S5_OCTETS)]
    a_im = [aim_ref[q * SUBLANES:(q + 1) * SUBLANES, :] for q in range(S5_OCTETS)]

    def scan(h, t0, carry):
        xr, xi = carry
        for t in range(t0, t0 + per):
            nr, ni = [], []
            for q in range(S5_OCTETS):
                row = (h * half + t) * S5_PITCH + q * SUBLANES
                br = bur_ref[row:row + SUBLANES, :]
                bi = bui_ref[row:row + SUBLANES, :]
                r = a_re[q] * xr[q] - a_im[q] * xi[q] + br
                i = a_re[q] * xi[q] + a_im[q] * xr[q] + bi
                bur_ref[row:row + SUBLANES, :] = r
                bui_ref[row:row + SUBLANES, :] = i
                nr.append(r)
                ni.append(i)
            xr, xi = nr, ni
        return xr, xi

    for j in range(S5_PAIRS):
        in_map(j, 0)
    carry = ([sre_ref[q * SUBLANES:(q + 1) * SUBLANES, :] for q in range(S5_OCTETS)],
             [sim_ref[q * SUBLANES:(q + 1) * SUBLANES, :] for q in range(S5_OCTETS)])
    for g in range(groups):
        carry = scan(0, g * per, carry)
        in_map(2 * g, 1)
        in_map(2 * g + 1, 1)
    acc = None
    for g in range(groups):
        carry = scan(1, g * per, carry)
        for j in (2 * g, 2 * g + 1):
            acc = accumulate(acc, j, out_map(j, 0), slice(0, half))
    for q in range(S5_OCTETS):
        sre_ref[q * SUBLANES:(q + 1) * SUBLANES, :] = carry[0][q]
        sim_ref[q * SUBLANES:(q + 1) * SUBLANES, :] = carry[1][q]
    for j in range(S5_PAIRS):
        acc = accumulate(acc, j, out_map(j, 1), slice(half, tc))

    y = y_ref[...] + d_ref[...] * u_ref[...].astype(F32)
    z = _gelu_tanh(y)
    gate = jnp.dot(z.astype(BF16), wglu_ref[...].astype(BF16), preferred_element_type=F32) + bglu_ref[...]
    o_ref[...] = (z * _sigmoid(gate)).astype(o_ref.dtype)


def s5_mixer(proj, w_in_blk, w_out_blk, a_re, a_im, d_skip, w_glu, b_glu, layer, *, batch, seq, tc):
    nt = seq // tc
    w = SSM_WIDTH
    kern = functools.partial(_s5_kernel, tc=tc)
    return pl.pallas_call(
        kern,
        out_shape=jax.ShapeDtypeStruct((batch * seq, w), BF16),
        grid=(batch, nt),
        in_specs=[
            pl.BlockSpec((tc, w), lambda b, c: (b * nt + c, COL_SSM // w)),
            _const_spec((None,) + w_in_blk.shape[1:], lambda b, c: (layer, 0, 0, 0)),
            _const_spec((None,) + w_out_blk.shape[1:], lambda b, c: (layer, 0, 0, 0)),
            _const_spec((None, S5_PAIRS, LANES), lambda b, c: (layer, 0, 0)),
            _const_spec((None, S5_PAIRS, LANES), lambda b, c: (layer, 0, 0)),
            _const_spec((None, 1, w), lambda b, c: (layer, 0, 0)),
            _const_spec((None, w, w), lambda b, c: (layer, 0, 0)),
            _const_spec((None, 1, w), lambda b, c: (layer, 0, 0)),
        ],
        out_specs=pl.BlockSpec((tc, w), lambda b, c: (b * nt + c, 0)),
        scratch_shapes=[
            pltpu.VMEM((tc * S5_PITCH, LANES), F32),
            pltpu.VMEM((tc * S5_PITCH, LANES), F32),
            pltpu.VMEM((S5_PAIRS, LANES), F32),
            pltpu.VMEM((S5_PAIRS, LANES), F32),
            pltpu.VMEM((tc, w), F32),
        ],
        compiler_params=_cparams(("parallel", "arbitrary")),
        name="s5_mixer",
    )(proj, w_in_blk, w_out_blk, a_re, a_im, d_skip, w_glu, b_glu)


def _split3(x):
    hi = x.astype(BF16)
    r1 = x - hi.astype(F32)
    mid = r1.astype(BF16)
    lo = (r1 - mid.astype(F32)).astype(BF16)
    return hi, mid, lo


def _gla_kernel(q_ref, k_ref, v_ref, r_ref, xn_ref, wglr_ref, wg_ref, bg_ref, ng_ref, tri_ref, o_ref, st_ref,
                bcum_ref, *, tg):
    c = GLA_CHUNK

    @pl.when(pl.program_id(1) == 0)
    def _():
        st_ref[...] = jnp.zeros_like(st_ref)
        wl = wglr_ref[...].astype(BF16)
        wl = jnp.concatenate([wl, jnp.zeros((LANES - GLA_GATE_RANK, wl.shape[1]), BF16)], axis=0)
        tri = tri_ref[...]
        for t0 in range(0, xn_ref.shape[0], tg):
            ts = slice(t0, t0 + tg)
            glr = lax.dot_general(xn_ref[ts, :], wl, _NT_DIMS, preferred_element_type=F32).astype(BF16)
            glog = jnp.dot(glr, wg_ref[...], preferred_element_type=F32) + bg_ref[...]
            g = (jnp.minimum(glog, 0.0) - jnp.log(1.0 + jnp.exp(-jnp.abs(glog)))) * (1.0 / GLA_GATE_NORM)
            acc = None
            for piece in _split3(g):
                part = jnp.dot(tri, piece, preferred_element_type=F32)
                acc = part if acc is None else acc + part
            bcum_ref[ts, :] = acc

    bcum = bcum_ref[pl.ds(pl.multiple_of(pl.program_id(1) * tg, tg), tg), :]

    row = lax.broadcasted_iota(jnp.int32, (c, c), 0)
    col = lax.broadcasted_iota(jnp.int32, (c, c), 1)
    causal = col <= row
    scale = np.float32(GLA_DK ** -0.5)
    ng = ng_ref[...]
    for n in range(tg // c):
        rs = slice(n * c, (n + 1) * c)
        for h in range(GLA_HEADS):
            ks = slice(h * GLA_DK, (h + 1) * GLA_DK)
            vs = slice(h * GLA_DV, (h + 1) * GLA_DV)
            b = bcum[rs, ks]
            bl = b[c - 1:c, :]
            qh = q_ref[rs, ks].astype(F32) * scale
            kh = k_ref[rs, ks].astype(F32)
            vh = v_ref[rs, vs]
            q_dec = (qh * jnp.exp(b)).astype(BF16)
            k_inv = (kh * jnp.exp(-b)).astype(BF16)
            k_dec = (kh * jnp.exp(bl - b)).astype(BF16)
            attn = lax.dot_general(q_dec, k_inv, _NT_DIMS, preferred_element_type=F32)
            attn = jnp.where(causal, attn, 0.0).astype(BF16)
            st = st_ref[h]
            o = (jnp.dot(attn, vh, preferred_element_type=F32)
                 + lax.dot_general(q_dec, st.astype(BF16), _NT_DIMS, preferred_element_type=F32))
            kvt = lax.dot_general(vh, k_dec, (((0,), (0,)), ((), ())), preferred_element_type=F32)
            st_ref[h] = jnp.exp(bl) * st + kvt
            o = o * _rms_scale(o) * ng
            o_ref[rs, vs] = (o * _silu(r_ref[rs, vs].astype(F32))).astype(o_ref.dtype)


def gla_mixer(proj, xn, w_in_t, w_gate2_pad, b_gate2, norm_g, tri, layer, *, batch, seq, tg):
    nt = seq // tg
    kw, vw = GLA_KEY_WIDTH, GLA_VALUE_WIDTH
    d = xn.shape[1]
    sh = GLA_GATE_RANK
    kern = functools.partial(_gla_kernel, tg=tg)
    return pl.pallas_call(
        kern,
        out_shape=jax.ShapeDtypeStruct((batch * seq, vw), BF16),
        grid=(batch, nt),
        in_specs=[
            pl.BlockSpec((tg, kw), lambda b, c: (b * nt + c, COL_GLA_Q // kw)),
            pl.BlockSpec((tg, kw), lambda b, c: (b * nt + c, COL_GLA_K // kw)),
            pl.BlockSpec((tg, vw), lambda b, c: (b * nt + c, COL_GLA_V // vw)),
            pl.BlockSpec((tg, vw), lambda b, c: (b * nt + c, COL_GLA_R // vw)),
            pl.BlockSpec((seq, d), lambda b, c: (b, 0)),
            _const_spec((None, sh, d), lambda b, c: (layer, ORIG_GLR // sh, 0)),
            _const_spec((None, LANES, kw), lambda b, c: (layer, 0, 0)),
            _const_spec((None, 1, kw), lambda b, c: (layer, 0, 0)),
            _const_spec((None, 1, GLA_DV), lambda b, c: (layer, 0, 0)),
            _const_spec((tg, tg), lambda b, c: (0, 0)),
        ],
        out_specs=pl.BlockSpec((tg, vw), lambda b, c: (b * nt + c, 0)),
        scratch_shapes=[pltpu.VMEM((GLA_HEADS, GLA_DV, GLA_DK), F32), pltpu.VMEM((seq, kw), F32)],
        compiler_params=_cparams(("parallel", "arbitrary")),
        name="gla_mixer",
    )(proj, proj, proj, proj, xn, w_in_t, w_gate2_pad, b_gate2, norm_g, tri)


def _moba_kernel(q_ref, k_ref, v_ref, qg_ref, kg_ref, o_ref, kn_ref, vt_ref, km_ref, st_ref, pt_ref, *, nb):
    bs = MOBA_BLOCK
    kg = kg_ref[...]
    for jb in range(nb):
        kb = k_ref[jb * bs:(jb + 1) * bs, :].astype(F32)
        kn = kb * _rms_scale(kb) * kg
        kn_ref[jb * bs:(jb + 1) * bs, :] = kn.astype(BF16)
        km_ref[jb:jb + 1, :] = jnp.mean(kn, axis=0, keepdims=True)
    hd = MOBA_HEAD_DIM
    vt_ref[0:hd, :] = v_ref[...].astype(F32).T.astype(BF16)
    ones_row = lax.broadcasted_iota(jnp.int32, (vt_ref.shape[0] - hd, vt_ref.shape[1]), 0) == 0
    vt_ref[hd:, :] = jnp.where(ones_row, 1.0, 0.0).astype(BF16)

    qg = qg_ref[...]
    km = km_ref[...]
    blk = lax.broadcasted_iota(jnp.int32, (nb, bs), 0)
    krow = lax.broadcasted_iota(jnp.int32, (bs, bs), 0)
    qcol = lax.broadcasted_iota(jnp.int32, (bs, bs), 1)
    def scores_stage(i):
        q = q_ref[i * bs:(i + 1) * bs, :].astype(F32)
        qn = (q * _rms_scale(q) * qg) * np.float32(MOBA_HEAD_DIM ** -0.5)
        qb = (qn * LOG2_E).astype(BF16)
        bias_t = None
        if i > MOBA_TOPK:
            sbt = lax.dot_general(km, qn, (((1,), (1,)), ((), ())),
                                  precision=lax.Precision.HIGHEST, preferred_element_type=F32)
            cnt = jnp.zeros((nb, bs), F32)
            for jp in range(i):
                sp = sbt[jp:jp + 1, :]
                beats = jnp.where(sp > sbt, 1.0, jnp.where(sp == sbt, jnp.where(blk > jp, 1.0, 0.0), 0.0))
                cnt = cnt + beats
            bias_t = jnp.where(cnt < MOBA_TOPK, 0.0, NEG_BIG)
        s_buf = st_ref.at[i % 2]
        mx8 = None
        for j in range(i + 1):
            ks = slice(j * bs, (j + 1) * bs)
            sj = lax.dot_general(kn_ref[ks, :], qb, _NT_DIMS, preferred_element_type=F32)
            if j == i:
                sj = jnp.where(krow <= qcol, sj, NEG_BIG)
            elif bias_t is not None:
                sj = sj + bias_t[j:j + 1, :]
            s_buf[ks, :] = sj
            pm = jnp.max(sj.reshape(bs // SUBLANES, SUBLANES, bs), axis=0)
            mx8 = pm if mx8 is None else jnp.maximum(mx8, pm)
        return jnp.max(mx8, axis=0, keepdims=True)

    def exp_stage(i, m):
        s_buf = st_ref.at[i % 2]
        p_buf = pt_ref.at[i % 2]
        for j in range(i + 1):
            ks = slice(j * bs, (j + 1) * bs)
            p_buf[ks, :] = jnp.exp2(s_buf[ks, :] - m).astype(BF16)

    def pv_stage(i):
        nk = (i + 1) * bs
        ot = jnp.dot(vt_ref[:, 0:nk], pt_ref[i % 2, 0:nk, :], preferred_element_type=F32)
        o_ref[i * bs:(i + 1) * bs, :] = (ot[0:hd, :] / ot[hd:hd + 1, :]).T.astype(o_ref.dtype)

    maxes = {}
    for step in range(nb + 2):
        if step < nb:
            maxes[step] = scores_stage(step)
        if 0 <= step - 1 < nb:
            exp_stage(step - 1, maxes.pop(step - 1))
        if 0 <= step - 2 < nb:
            pv_stage(step - 2)


def moba_mixer(proj, q_g, k_g, layer, *, batch, seq):
    bs, hd, nh = MOBA_BLOCK, MOBA_HEAD_DIM, MOBA_HEADS
    nb = seq // bs
    kern = functools.partial(_moba_kernel, nb=nb)
    blk = lambda col: pl.BlockSpec((seq, hd), lambda b, h: (b, col // hd + h))
    return pl.pallas_call(
        kern,
        out_shape=jax.ShapeDtypeStruct((batch * seq, MOBA_WIDTH), BF16),
        grid=(batch, nh),
        in_specs=[
            blk(COL_MOBA_Q), blk(COL_MOBA_K), blk(COL_MOBA_V),
            _const_spec((None, 1, hd), lambda b, h: (layer, 0, 0)),
            _const_spec((None, 1, hd), lambda b, h: (layer, 0, 0)),
        ],
        out_specs=pl.BlockSpec((seq, hd), lambda b, h: (b, h)),
        scratch_shapes=[pltpu.VMEM((seq, hd), BF16), pltpu.VMEM((hd + BF16_ROWS, seq), BF16),
                        pltpu.VMEM((nb, hd), F32),
                        pltpu.VMEM((2, seq, bs), F32), pltpu.VMEM((2, seq, bs), BF16)],
        compiler_params=_cparams(("parallel", "parallel")),
        name="moba_mixer",
    )(proj, proj, proj, q_g, k_g)


_WEIGHT_OUTER = ("arbitrary", "arbitrary")


def _merge_kernel(ys_ref, yg_ref, ym_ref, gs_ref, gg_ref, gm_ref, ws_ref, wg_ref, wm_ref, o_ref,
                  wsb_ref, wgb_ref, wmb_ref):
    @pl.when(pl.program_id(1) == 0)
    def _():
        wsb_ref[...] = ws_ref[...].astype(BF16)
        wgb_ref[...] = wg_ref[...].astype(BF16)
        wmb_ref[...] = wm_ref[...].astype(BF16)

    def branch(g_ref, y_ref, w_ref):
        return _sigmoid(g_ref[...].astype(F32)) * jnp.dot(y_ref[...], w_ref[...], preferred_element_type=F32)

    acc = branch(gs_ref, ys_ref, wsb_ref) + branch(gg_ref, yg_ref, wgb_ref) + branch(gm_ref, ym_ref, wmb_ref)
    o_ref[...] = acc.astype(o_ref.dtype)


def merge_branches(y_ssm, y_gla, y_moba, proj, w_s, w_g, w_m, layer, *, tm, tn):
    m, kdim = y_ssm.shape
    n = w_s.shape[-1]
    yspec = pl.BlockSpec((tm, kdim), lambda j, i: (i, 0))
    gspec = lambda col: pl.BlockSpec((tm, tn), lambda j, i: (i, col // tn + j))
    wspec = pl.BlockSpec((None, kdim, tn), lambda j, i: (layer, 0, j))
    return pl.pallas_call(
        _merge_kernel,
        out_shape=jax.ShapeDtypeStruct((m, n), BF16),
        grid=(n // tn, m // tm),
        in_specs=[yspec, yspec, yspec, gspec(COL_GATE_SSM), gspec(COL_GATE_GLA), gspec(COL_GATE_MOBA),
                  wspec, wspec, wspec],
        out_specs=pl.BlockSpec((tm, tn), lambda j, i: (i, j)),
        scratch_shapes=[pltpu.VMEM((kdim, tn), BF16)] * 3,
        compiler_params=_cparams(_WEIGHT_OUTER),
        name="merge",
    )(y_ssm, y_gla, y_moba, proj, proj, proj, w_s, w_g, w_m)


def _out_proj_kernel(a_ref, w_ref, x_ref, g_ref, o_ref, h_ref, wbf_ref):
    @pl.when(pl.program_id(0) == 0)
    def _():
        wbf_ref[...] = w_ref[...].astype(BF16)

    xn = x_ref[...] + jnp.dot(a_ref[...], wbf_ref[...], preferred_element_type=F32)
    o_ref[...] = xn
    h_ref[...] = (xn * _rms_scale(xn) * g_ref[...]).astype(h_ref.dtype)


def out_proj_norm(a, w_all, x, g, layer, *, tm):
    m, kdim = a.shape
    n = w_all.shape[-1]
    row = lambda width: pl.BlockSpec((tm, width), lambda i: (i, 0))
    return pl.pallas_call(
        _out_proj_kernel,
        out_shape=(jax.ShapeDtypeStruct((m, n), F32), jax.ShapeDtypeStruct((m, n), BF16)),
        grid=(m // tm,),
        in_specs=[row(kdim), _const_spec((None, kdim, n), lambda i: (layer, 0, 0)), row(n),
                  _const_spec((None, 1, n), lambda i: (layer, 0, 0))],
        out_specs=(row(n), row(n)),
        scratch_shapes=[pltpu.VMEM((kdim, n), BF16)],
        compiler_params=_cparams(("arbitrary",)),
        name="out_proj",
    )(a, w_all, x, g)


def _resid_matmul_kernel(a_ref, w_ref, x_ref, o_ref, wbf_ref):
    @pl.when(pl.program_id(1) == 0)
    def _():
        wbf_ref[...] = w_ref[...].astype(BF16)

    o_ref[...] = x_ref[...] + jnp.dot(a_ref[...], wbf_ref[...], preferred_element_type=F32)


def resid_matmul(a, w_all, x, layer, *, tm, tn):
    m, kdim = a.shape
    n = w_all.shape[-1]
    return pl.pallas_call(
        _resid_matmul_kernel,
        out_shape=jax.ShapeDtypeStruct((m, n), F32),
        grid=(n // tn, m // tm),
        in_specs=[
            pl.BlockSpec((tm, kdim), lambda j, i: (i, 0)),
            pl.BlockSpec((None, kdim, tn), lambda j, i: (layer, 0, j)),
            pl.BlockSpec((tm, tn), lambda j, i: (i, j)),
        ],
        out_specs=pl.BlockSpec((tm, tn), lambda j, i: (i, j)),
        scratch_shapes=[pltpu.VMEM((kdim, tn), BF16)],
        compiler_params=_cparams(_WEIGHT_OUTER),
        name="resid_matmul",
    )(a, w_all, x)


def _ffn_up_kernel(h_ref, wg_ref, wu_ref, o_ref, wgb_ref, wub_ref):
    @pl.when(pl.program_id(1) == 0)
    def _():
        wgb_ref[...] = wg_ref[...].astype(BF16)
        wub_ref[...] = wu_ref[...].astype(BF16)

    h = h_ref[...]
    gate = jnp.dot(h, wgb_ref[...], preferred_element_type=F32)
    up = jnp.dot(h, wub_ref[...], preferred_element_type=F32)
    o_ref[...] = (_silu(gate) * up).astype(o_ref.dtype)


def ffn_up(h, w_gate, w_up, layer, *, tm, tn):
    m, d = h.shape
    n = w_gate.shape[-1]
    wspec = pl.BlockSpec((None, d, tn), lambda j, i: (layer, 0, j))
    return pl.pallas_call(
        _ffn_up_kernel,
        out_shape=jax.ShapeDtypeStruct((m, n), BF16),
        grid=(n // tn, m // tm),
        in_specs=[pl.BlockSpec((tm, d), lambda j, i: (i, 0)), wspec, wspec],
        out_specs=pl.BlockSpec((tm, tn), lambda j, i: (i, j)),
        scratch_shapes=[pltpu.VMEM((d, tn), BF16)] * 2,
        compiler_params=_cparams(_WEIGHT_OUTER),
        name="ffn_up",
    )(h, w_gate, w_up)


def _gla_tri(tg):
    r = np.arange(tg)
    same_chunk = (r[:, None] // GLA_CHUNK) == (r[None, :] // GLA_CHUNK)
    return jnp.asarray(np.where(same_chunk & (r[None, :] <= r[:, None]), 1.0, 0.0), dtype=BF16)


def kernel(x, norm1_g, w_in, ssm_lambda_re, ssm_lambda_im, ssm_log_dt, ssm_b_re, ssm_b_im, ssm_c_re, ssm_c_im, ssm_d, ssm_w_glu, ssm_b_glu, gla_w_gate2, gla_b_gate2, gla_norm_g, moba_q_norm_g, moba_k_norm_g, w_branch_ssm, w_branch_gla, w_branch_moba, w_out, norm2_g, ffn_w_gate, ffn_w_up, ffn_w_down):
    batch, seq, d = x.shape
    depth = w_in.shape[0]
    m = batch * seq
    assert d == D_MODEL and w_in.shape[1:] == (D_MODEL, PROJ_WIDTH + GLA_GATE_RANK)
    assert seq % MOBA_BLOCK == 0 and seq % TILES["s5"]["tc"] == 0 and seq % TILES["gla"]["tg"] == 0
    assert all(m % t["tm"] == 0 for t in TILES.values() if "tm" in t)

    w_in_t = jnp.swapaxes(w_in, 1, 2)
    wg2_pad = jnp.pad(gla_w_gate2, ((0, 0), (0, LANES - GLA_GATE_RANK), (0, 0))).astype(BF16)
    s5_win, s5_wout, a_re_t, a_im_t = s5_weights(ssm_lambda_re, ssm_lambda_im, ssm_log_dt, ssm_b_re, ssm_b_im,
                                                 ssm_c_re, ssm_c_im)
    row3 = lambda t: t[:, None, :]
    n1, n2 = row3(norm1_g), row3(norm2_g)
    ssm_d3, b_glu3 = row3(ssm_d), row3(ssm_b_glu)
    bg2, gng = row3(gla_b_gate2), row3(gla_norm_g)
    mqg, mkg = row3(moba_q_norm_g), row3(moba_k_norm_g)

    tri = _gla_tri(TILES["gla"]["tg"])
    xf = x.reshape(m, d)
    for l in range(depth):
        xn = rmsnorm(xf, n1, l, **TILES["rmsnorm"])
        proj = in_proj(xn, w_in_t, l, **TILES["in_proj"])
        y_ssm = s5_mixer(proj, s5_win, s5_wout, a_re_t, a_im_t, ssm_d3, ssm_w_glu, b_glu3, l,
                         batch=batch, seq=seq, **TILES["s5"])
        y_gla = gla_mixer(proj, xn, w_in_t, wg2_pad, bg2, gng, tri, l, batch=batch, seq=seq, **TILES["gla"])
        y_moba = moba_mixer(proj, mqg, mkg, l, batch=batch, seq=seq)
        merged = merge_branches(y_ssm, y_gla, y_moba, proj, w_branch_ssm, w_branch_gla, w_branch_moba, l,
                                **TILES["merge"])
        xf, hn = out_proj_norm(merged, w_out, xf, n2, l, **TILES["out_proj"])
        act = ffn_up(hn, ffn_w_gate, ffn_w_up, l, **TILES["ffn_up"])
        xf = resid_matmul(act, ffn_w_down, xf, l, **TILES["ffn_down"])
    return xf.reshape(batch, seq, d)
```
